```python
import jax, jax.numpy as jnp
from jax import lax
import numpy as np

D_MODEL = 1024
BATCH = 2
SEQ = 16384
DEPTH = 2

GRID_W = 64
CTX_LEN = 256
N_MIXERS = 2
N_EVEN_LAYERS = (DEPTH + 1) // 2
N_ODD_LAYERS = DEPTH // 2
POOL_WINDOWS = (2, 4, 8, 16)
N_POOL_GROUPS = len(POOL_WINDOWS)
POOL_GROUP_DIM = D_MODEL // N_POOL_GROUPS
HEAD_DIM = 64
N_HEADS = D_MODEL // HEAD_DIM
N_KV_HEADS = 4
GROUP = N_HEADS // N_KV_HEADS
Q_DIM = N_HEADS * HEAD_DIM
KV_DIM = N_KV_HEADS * HEAD_DIM
ROPE_THETA = 10000.0
Q_BLOCK = 128
D_FF = 2816
N_EXPERTS = 8
TOP_K = 2
MOE_BLOCK = 256
NORM_EPS = 1e-6
N_MOD = 6

kernel_name = "hybrid_pool_gqa_moe_dit_block"


def rmsnorm(x, g):
    xf = x.astype(jnp.float32)
    y = xf * lax.rsqrt(jnp.mean(xf * xf, axis=-1, keepdims=True) + NORM_EPS)
    return (y * g.astype(jnp.float32)).astype(x.dtype)


def adaln(cvec, w, b):
    m = jax.nn.silu(cvec.astype(jnp.float32)) @ w.astype(jnp.float32) + b.astype(jnp.float32)
    m = m.astype(cvec.dtype)[:, None, :]
    return jnp.split(m, N_MOD, axis=-1)


def modulate(h, shift, scale):
    return h * (1.0 + scale) + shift


def pool_mix(h, w_grp, scale):
    B, L, D = h.shape
    h4 = h.reshape(B, L, N_POOL_GROUPS, POOL_GROUP_DIM)
    cs = jnp.cumsum(h4.astype(jnp.float32), axis=1)
    cs = jnp.concatenate([jnp.zeros_like(cs[:, :1]), cs], axis=1)
    t = jnp.arange(L, dtype=jnp.int32)[:, None]
    half = jnp.asarray(POOL_WINDOWS, dtype=jnp.int32) // 2
    lo = jnp.clip(t - half, 0, L)
    hi = jnp.clip(t + half, 0, L)
    gid = jnp.arange(N_POOL_GROUPS, dtype=jnp.int32)[None, :]
    win_sum = cs[:, hi, gid] - cs[:, lo, gid]
    mean = win_sum / (hi - lo).astype(jnp.float32)[None, :, :, None]
    d = (mean - h4.astype(jnp.float32)).astype(h.dtype)
    y = jnp.einsum('blgc,gce->blge', d, w_grp).reshape(B, L, D)
    return y * scale


def rope_1d(x, pos):
    half = x.shape[-1] // 2
    freqs = ROPE_THETA ** (-jnp.arange(half, dtype=jnp.float32) / half)
    ang = pos.astype(jnp.float32)[:, None] * freqs
    shape = (ang.shape[0],) + (1,) * (x.ndim - 3) + (half,)
    cos = jnp.cos(ang).reshape(shape)
    sin = jnp.sin(ang).reshape(shape)
    x1, x2 = x[..., :half], x[..., half:]
    return jnp.concatenate([x1 * cos - x2 * sin, x2 * cos + x1 * sin], axis=-1).astype(x.dtype)


def rope_2d(x, row, col):
    d = HEAD_DIM // 2
    return jnp.concatenate([rope_1d(x[..., :d], row), rope_1d(x[..., d:], col)], axis=-1)


def heads_q(q_flat, g):
    B, L = q_flat.shape[:2]
    return rmsnorm(q_flat.reshape(B, L, N_KV_HEADS, GROUP, HEAD_DIM), g)


def heads_kv(kv_flat, g):
    B, L = kv_flat.shape[:2]
    kv = kv_flat.reshape(B, L, 2, N_KV_HEADS, HEAD_DIM)
    return rmsnorm(kv[:, :, 0], g), kv[:, :, 1]


def attend(q, k, v):
    B, Lq = q.shape[:2]
    nb = Lq // Q_BLOCK
    qb = q.reshape(B, nb, Q_BLOCK, N_KV_HEADS, GROUP, HEAD_DIM).swapaxes(0, 1)
    sm_scale = HEAD_DIM ** -0.5

    def one_block(qblk):
        s = jnp.einsum('bqkgd,bnkd->bkgqn', qblk, k, preferred_element_type=jnp.float32) * sm_scale
        p = jax.nn.softmax(s, axis=-1)
        return jnp.einsum('bkgqn,bnkd->bqkgd', p.astype(v.dtype), v)

    o = lax.map(one_block, qb)
    return o.swapaxes(0, 1).reshape(B, Lq, Q_DIM)


def swiglu(h, w1, w3, w2):
    return (jax.nn.silu(h @ w1) * (h @ w3)) @ w2


def moe_swiglu(h, router_w, w1, w3, w2):
    B, L, D = h.shape
    T = B * L
    hf = h.reshape(T, D)
    logits = hf.astype(jnp.float32) @ router_w.astype(jnp.float32)
    top_logit, top_e = lax.top_k(logits, TOP_K)
    gates = jax.nn.softmax(top_logit, axis=-1)
    n_assign = T * TOP_K
    flat_e = top_e.reshape(-1)
    flat_tok = jnp.arange(n_assign, dtype=jnp.int32) // TOP_K
    flat_g = gates.reshape(-1)
    order = jnp.argsort(flat_e)
    e_sorted = flat_e[order]
    sizes = jnp.bincount(flat_e, length=N_EXPERTS)
    padded = (sizes + MOE_BLOCK - 1) // MOE_BLOCK * MOE_BLOCK
    start = jnp.cumsum(sizes) - sizes
    ends = jnp.cumsum(padded)
    pstart = ends - padded
    dest = pstart[e_sorted] + (jnp.arange(n_assign, dtype=jnp.int32) - start[e_sorted])
    cap = n_assign + N_EXPERTS * MOE_BLOCK
    n_blk = cap // MOE_BLOCK
    buf_tok = jnp.zeros((cap,), jnp.int32).at[dest].set(flat_tok[order])
    buf_gate = jnp.zeros((cap,), jnp.float32).at[dest].set(flat_g[order])
    blk_start = jnp.arange(n_blk, dtype=jnp.int32) * MOE_BLOCK
    blk_e = jnp.clip(jnp.searchsorted(ends, blk_start, side='right'), 0, N_EXPERTS - 1)

    def expert_block(args):
        tok, g, e = args
        xb = hf[tok]
        return swiglu(xb, w1[e], w3[e], w2[e]) * g[:, None].astype(xb.dtype)

    y = lax.map(expert_block, (buf_tok.reshape(n_blk, MOE_BLOCK),
                               buf_gate.reshape(n_blk, MOE_BLOCK), blk_e))
    out = jnp.zeros_like(hf).at[buf_tok].add(y.reshape(cap, D))
    return out.reshape(B, L, D)


def setup_inputs(seed: int = 0) -> dict:
    key = jax.random.key(seed)
    ks = jax.random.split(key, 21)

    def nrm(k, shape, s):
        return jax.random.normal(k, shape, jnp.float32) * s

    D = D_MODEL
    return {
        "x": nrm(ks[0], (BATCH, SEQ, D), 1.0),
        "c": nrm(ks[1], (BATCH, D), 1.0),
        "ctx": nrm(ks[2], (BATCH, CTX_LEN, D), 1.0),
        "c_ctx": nrm(ks[3], (D,), 1.0),
        "ada_w": nrm(ks[4], (DEPTH, D, N_MOD * D), 0.5 * D ** -0.5),
        "ada_b": nrm(ks[5], (DEPTH, N_MOD * D), 0.02),
        "norm_g": 1.0 + nrm(ks[6], (DEPTH, 2, D), 0.1),
        "pool_w": nrm(ks[7], (N_EVEN_LAYERS, N_POOL_GROUPS, POOL_GROUP_DIM, POOL_GROUP_DIM), POOL_GROUP_DIM ** -0.5),
        "pool_scale": 1.0 + nrm(ks[8], (N_EVEN_LAYERS, D), 0.1),
        "ffn_w1": nrm(ks[9], (N_EVEN_LAYERS, D, D_FF), D ** -0.5),
        "ffn_w3": nrm(ks[10], (N_EVEN_LAYERS, D, D_FF), D ** -0.5),
        "ffn_w2": nrm(ks[11], (N_EVEN_LAYERS, D_FF, D), D_FF ** -0.5),
        "w_qkv": nrm(ks[12], (N_ODD_LAYERS, D, Q_DIM + 2 * KV_DIM), D ** -0.5),
        "w_o": nrm(ks[13], (N_ODD_LAYERS, Q_DIM, D), Q_DIM ** -0.5),
        "q_norm_g": 1.0 + nrm(ks[14], (N_ODD_LAYERS, HEAD_DIM), 0.1),
        "k_norm_g": 1.0 + nrm(ks[15], (N_ODD_LAYERS, HEAD_DIM), 0.1),
        "router_w": nrm(ks[16], (N_ODD_LAYERS, D, N_EXPERTS), D ** -0.5),
        "moe_w1": nrm(ks[17], (N_ODD_LAYERS, N_EXPERTS, D, D_FF), D ** -0.5),
        "moe_w3": nrm(ks[18], (N_ODD_LAYERS, N_EXPERTS, D, D_FF), D ** -0.5),
        "moe_w2": nrm(ks[19], (N_ODD_LAYERS, N_EXPERTS, D_FF, D), D_FF ** -0.5),
        "final_g": 1.0 + nrm(ks[20], (D,), 0.1),
    }


def reference(x, c, ctx, c_ctx, ada_w, ada_b, norm_g, pool_w, pool_scale, ffn_w1, ffn_w3, ffn_w2,
              w_qkv, w_o, q_norm_g, k_norm_g, router_w, moe_w1, moe_w3, moe_w2, final_g):
    L = x.shape[1]
    ROWS = L // GRID_W
    t_row = jnp.repeat(jnp.arange(ROWS, dtype=jnp.int32), GRID_W)
    t_col = jnp.tile(jnp.arange(GRID_W, dtype=jnp.int32), ROWS)

    for i in range(DEPTH):
        j = i // N_MIXERS
        last = i == DEPTH - 1
        sh1, sc1, g1, sh2, sc2, g2 = adaln(c, ada_w[i], ada_b[i])
        ksh1, ksc1, kg1, ksh2, ksc2, kg2 = adaln(c_ctx[None, :], ada_w[i], ada_b[i])

        hx = modulate(rmsnorm(x, norm_g[i, 0]), sh1, sc1)
        hk = modulate(rmsnorm(ctx, norm_g[i, 0]), ksh1, ksc1)
        if i % N_MIXERS == 0:
            x = x + g1 * pool_mix(hx, pool_w[j], pool_scale[j])
            if not last:
                ctx = ctx + kg1 * pool_mix(hk, pool_w[j], pool_scale[j])
        else:
            qkv = hx @ w_qkv[j]
            qx = rope_2d(heads_q(qkv[..., :Q_DIM], q_norm_g[j]), t_row, t_col)
            kx, vx = heads_kv(qkv[..., Q_DIM:], k_norm_g[j])
            kx = rope_2d(kx, t_row, t_col)
            if last:
                kk, vk = heads_kv(hk @ w_qkv[j][:, Q_DIM:], k_norm_g[j])
            else:
                qkv_c = hk @ w_qkv[j]
                kk, vk = heads_kv(qkv_c[..., Q_DIM:], k_norm_g[j])
                qk = heads_q(qkv_c[..., :Q_DIM], q_norm_g[j])
                ctx = ctx + kg1 * (attend(qk, kk, vk) @ w_o[j])
            k_all = jnp.concatenate([kx, kk], axis=1)
            v_all = jnp.concatenate([vx, vk], axis=1)
            x = x + g1 * (attend(qx, k_all, v_all) @ w_o[j])

        hx = modulate(rmsnorm(x, norm_g[i, 1]), sh2, sc2)
        if i % N_MIXERS == 0:
            x = x + g2 * swiglu(hx, ffn_w1[j], ffn_w3[j], ffn_w2[j])
            if not last:
                hk = modulate(rmsnorm(ctx, norm_g[i, 1]), ksh2, ksc2)
                ctx = ctx + kg2 * swiglu(hk, ffn_w1[j], ffn_w3[j], ffn_w2[j])
        else:
            x = x + g2 * moe_swiglu(hx, router_w[j], moe_w1[j], moe_w3[j], moe_w2[j])
            if not last:
                hk = modulate(rmsnorm(ctx, norm_g[i, 1]), ksh2, ksc2)
                ctx = ctx + kg2 * moe_swiglu(hk, router_w[j], moe_w1[j], moe_w3[j], moe_w2[j])

    return rmsnorm(x, final_g)
```

```python
import functools

import jax
import jax.numpy as jnp
from jax import lax
from jax.experimental import pallas as pl
from jax.experimental.pallas import tpu as pltpu

F32 = jnp.float32
BF16 = jnp.bfloat16

D_MODEL = 1024
GRID_W = 64
POOL_WINDOWS = (2, 4, 8, 16)
POOL_GROUP_DIM = D_MODEL // len(POOL_WINDOWS)
HEAD_DIM = 64
N_KV_HEADS = 4
GROUP = 4
KV_SLAB = N_KV_HEADS * HEAD_DIM
Q_DIM = N_KV_HEADS * GROUP * HEAD_DIM
ROPE_THETA = 10000.0
D_FF = 2816
N_EXPERTS = 8
TOP_K = 2
NORM_EPS = 1e-6
N_MOD = 6

POOL_HALO = 16
ROW_TILE = 512
FF_TILE = 1408
MOE_ROWS = 512
COPY_ROWS = 512
ATT_TQ = 512
ATT_TK = 1280
MIB = 1024 * 1024
LANES = 128
ROW_SLABS = D_MODEL // LANES


def _dot(a, b):
    return jnp.dot(a, b, preferred_element_type=F32)


def _rms(x, g):
    ms = jnp.mean(x * x, axis=-1, keepdims=True)
    return x * lax.rsqrt(ms + NORM_EPS) * g


def _silu(x):
    return x * (1.0 / (1.0 + jnp.exp(-x)))


def _load_rows(ref, first=0):
    return jnp.concatenate([ref[:, first + s, :] for s in range(ROW_SLABS)], axis=1)


def _store_rows(ref, val):
    for s in range(ROW_SLABS):
        ref[:, s, :] = val[:, s * LANES:(s + 1) * LANES]


def _params(semantics, vmem_mib):
    return pltpu.CompilerParams(dimension_semantics=semantics, vmem_limit_bytes=vmem_mib * MIB)


def _adaln_kernel(c_ref, w_ref, b_ref, o_ref):
    s = _silu(c_ref[...])
    o_ref[0] = jnp.dot(s, w_ref[0], preferred_element_type=F32, precision=lax.Precision.HIGHEST) + b_ref[0]


def _adaln(cvecs, ada_w, ada_b):
    depth, d, n = ada_w.shape
    tn = n // 4
    return pl.pallas_call(
        _adaln_kernel,
        grid=(depth, n // tn),
        in_specs=[
            pl.BlockSpec((8, d), lambda i, j: (0, 0)),
            pl.BlockSpec((1, d, tn), lambda i, j: (i, 0, j)),
            pl.BlockSpec((1, 1, tn), lambda i, j: (i, 0, j)),
        ],
        out_specs=pl.BlockSpec((1, 8, tn), lambda i, j: (i, 0, j)),
        out_shape=jax.ShapeDtypeStruct((depth, 8, n), F32),
        compiler_params=_params(("parallel", "parallel"), 32),
        name="adaln",
    )(cvecs, ada_w, ada_b.reshape(depth, 1, n))


def _pool_kernel(xp_ref, xc_ref, xn_ref, mod_ref, ng_ref, pw_ref, ps_ref, o_ref, *, seq, tm):
    t = pl.program_id(1)
    xc = xc_ref[0]
    xa = jnp.concatenate([xp_ref[0], xc, xn_ref[0]], axis=0)
    sh, sc, gate = mod_ref[0, 0:1, :], mod_ref[0, 1:2, :], mod_ref[0, 2:3, :]
    h = _rms(xa, ng_ref[...]) * (1.0 + sc) + sh
    hb = h.astype(BF16)
    wide = tm + 2 * POOL_HALO
    row = lax.broadcasted_iota(jnp.int32, (tm, wide), 0) + t * tm
    col = lax.broadcasted_iota(jnp.int32, (tm, wide), 1) + (t * tm - POOL_HALO)
    row1 = lax.broadcasted_iota(jnp.int32, (tm, 1), 0) + t * tm
    ys = []
    for g, w in enumerate(POOL_WINDOWS):
        half = w // 2
        lo = jnp.maximum(row - half, 0)
        hi = jnp.minimum(row + half, seq)
        band = jnp.where((col >= lo) & (col < hi), 1.0, 0.0).astype(BF16)
        cnt = (jnp.minimum(row1 + half, seq) - jnp.maximum(row1 - half, 0)).astype(F32)
        lanes = slice(g * POOL_GROUP_DIM, (g + 1) * POOL_GROUP_DIM)
        win_sum = _dot(band, hb[:, lanes])
        d = win_sum / cnt - h[POOL_HALO:POOL_HALO + tm, lanes]
        ys.append(_dot(d.astype(BF16), pw_ref[g]))
    y = jnp.concatenate(ys, axis=1) * ps_ref[...]
    o_ref[0] = xc + gate * y


def _pool_layer(x, mods, norm_g, pool_w, pool_scale):
    b, seq, d = x.shape
    tm = min(ROW_TILE, seq)
    per = tm // POOL_HALO
    last = seq // POOL_HALO - 1
    return pl.pallas_call(
        functools.partial(_pool_kernel, seq=seq, tm=tm),
        grid=(b, seq // tm),
        in_specs=[
            pl.BlockSpec((1, POOL_HALO, d), lambda i, t: (i, jnp.maximum(t * per - 1, 0), 0)),
            pl.BlockSpec((1, tm, d), lambda i, t: (i, t, 0)),
            pl.BlockSpec((1, POOL_HALO, d), lambda i, t: (i, jnp.minimum((t + 1) * per, last), 0)),
            pl.BlockSpec((1, N_MOD, d), lambda i, t: (i, 0, 0)),
            pl.BlockSpec((1, d), lambda i, t: (0, 0)),
            pl.BlockSpec(pool_w.shape, lambda i, t: (0, 0, 0)),
            pl.BlockSpec((1, d), lambda i, t: (0, 0)),
        ],
        out_specs=pl.BlockSpec((1, tm, d), lambda i, t: (i, t, 0)),
        out_shape=jax.ShapeDtypeStruct(x.shape, F32),
        compiler_params=_params(("parallel", "parallel"), 48),
        name="pool_mix",
    )(x, x, x, mods, norm_g.reshape(1, d), pool_w, pool_scale.reshape(1, d))


def _ffn_dense_kernel(x_ref, mod_ref, ng_ref, w1_ref, w3_ref, w2_ref, o_ref, h_scr, acc_scr):
    f = pl.program_id(1)

    @pl.when(f == 0)
    def _():
        sh, sc = mod_ref[0, 3:4, :], mod_ref[0, 4:5, :]
        h_scr[...] = (_rms(x_ref[...], ng_ref[...]) * (1.0 + sc) + sh).astype(BF16)
        acc_scr[...] = jnp.zeros_like(acc_scr)

    h = h_scr[...]
    u = _silu(_dot(h, w1_ref[...])) * _dot(h, w3_ref[...])
    acc_scr[...] += _dot(u.astype(BF16), w2_ref[...])

    @pl.when(f == pl.num_programs(1) - 1)
    def _():
        o_ref[...] = x_ref[...] + mod_ref[0, 5:6, :] * acc_scr[...]


def _ffn_dense_layer(x, mods, norm_g, w1, w3, w2):
    b, seq, d = x.shape
    t = b * seq
    tm = min(ROW_TILE, seq)
    per_batch = seq // tm
    out = pl.pallas_call(
        _ffn_dense_kernel,
        grid=(t // tm, D_FF // FF_TILE),
        in_specs=[
            pl.BlockSpec((tm, d), lambda i, f: (i, 0)),
            pl.BlockSpec((1, N_MOD, d), lambda i, f: (i // per_batch, 0, 0)),
            pl.BlockSpec((1, d), lambda i, f: (0, 0)),
            pl.BlockSpec((d, FF_TILE), lambda i, f: (0, f)),
            pl.BlockSpec((d, FF_TILE), lambda i, f: (0, f)),
            pl.BlockSpec((FF_TILE, d), lambda i, f: (f, 0)),
        ],
        out_specs=pl.BlockSpec((tm, d), lambda i, f: (i, 0)),
        out_shape=jax.ShapeDtypeStruct((t, d), F32),
        scratch_shapes=[pltpu.VMEM((tm, d), BF16), pltpu.VMEM((tm, d), F32)],
        compiler_params=_params(("parallel", "arbitrary"), 56),
        name="ffn_dense",
    )(x.reshape(t, d), mods, norm_g.reshape(1, d), w1, w3, w2)
    return out.reshape(b, seq, d)


def _head_norm_rope(z, ones_blk, gain, cos, sin):
    sq = z * z
    hi = sq.astype(BF16)
    lo = (sq - hi.astype(F32)).astype(BF16)
    ss = _dot(hi, ones_blk) + _dot(lo, ones_blk)
    zn = z * lax.rsqrt(ss * (1.0 / HEAD_DIM) + NORM_EPS) * gain
    if cos is None:
        return zn
    width = z.shape[1]
    fwd = pltpu.roll(zn, width - 16, 1)
    bwd = pltpu.roll(zn, 16, 1)
    lane = lax.broadcasted_iota(jnp.int32, zn.shape, 1)
    partner = jnp.where((lane & 31) < 16, fwd, bwd)
    return zn * cos + partner * sin


def _qkv_kernel(*refs, n_q, rope):
    x_ref, mod_ref, ng_ref, w_ref, e_ref, qg_ref, kg_ref = refs[:7]
    refs = refs[7:]
    if rope:
        cos_ref, sin_ref = refs[:2]
        refs = refs[2:]
        cos, sin = cos_ref[...], sin_ref[...]
    else:
        cos = sin = None
    sh, sc = mod_ref[0, 0:1, :], mod_ref[0, 1:2, :]
    hb = (_rms(x_ref[0], ng_ref[...]) * (1.0 + sc) + sh).astype(BF16)
    qkv = _dot(hb, w_ref[...])
    ones_blk = e_ref[...]
    if n_q:
        q_ref, k_ref, v_ref = refs
        for c in range(n_q):
            z = _head_norm_rope(qkv[:, c * KV_SLAB:(c + 1) * KV_SLAB], ones_blk, qg_ref[...], cos, sin)
            q_ref[0, :, c * KV_SLAB:(c + 1) * KV_SLAB] = (z * (HEAD_DIM ** -0.5)).astype(BF16)
    else:
        k_ref, v_ref = refs
    base = n_q * KV_SLAB
    k_ref[0] = _head_norm_rope(qkv[:, base:base + KV_SLAB], ones_blk, kg_ref[...], cos, sin).astype(BF16)
    v_ref[0] = qkv[:, base + KV_SLAB:base + 2 * KV_SLAB].astype(BF16)


def _qkv_layer(x, mods, norm_g, w, q_gain, k_gain, tables):
    b, seq, d = x.shape
    tm = min(ROW_TILE, seq)
    width = w.shape[1]
    n_q = (width - 2 * KV_SLAB) // KV_SLAB
    rope = tables is not None
    seg = lax.broadcasted_iota(jnp.int32, (KV_SLAB, KV_SLAB), 0) // HEAD_DIM
    ones_blk = (seg == seg.T).astype(BF16)
    in_specs = [
        pl.BlockSpec((1, tm, d), lambda i, t: (i, t, 0)),
        pl.BlockSpec((1, N_MOD, d), lambda i, t: (i, 0, 0)),
        pl.BlockSpec((1, d), lambda i, t: (0, 0)),
        pl.BlockSpec((d, width), lambda i, t: (0, 0)),
        pl.BlockSpec((KV_SLAB, KV_SLAB), lambda i, t: (0, 0)),
        pl.BlockSpec((1, KV_SLAB), lambda i, t: (0, 0)),
        pl.BlockSpec((1, KV_SLAB), lambda i, t: (0, 0)),
    ]
    args = [x, mods, norm_g.reshape(1, d), w, ones_blk,
            jnp.tile(q_gain, N_KV_HEADS).reshape(1, KV_SLAB), jnp.tile(k_gain, N_KV_HEADS).reshape(1, KV_SLAB)]
    if rope:
        in_specs += [pl.BlockSpec((tm, KV_SLAB), lambda i, t: (t, 0))] * 2
        args += list(tables)
    kv_spec = pl.BlockSpec((1, tm, KV_SLAB), lambda i, t: (i, t, 0))
    kv_shape = jax.ShapeDtypeStruct((b, seq, KV_SLAB), BF16)
    out_specs, out_shape = [kv_spec, kv_spec], [kv_shape, kv_shape]
    if n_q:
        out_specs = [pl.BlockSpec((1, tm, n_q * KV_SLAB), lambda i, t: (i, t, 0))] + out_specs
        out_shape = [jax.ShapeDtypeStruct((b, seq, n_q * KV_SLAB), BF16)] + out_shape
    outs = pl.pallas_call(
        functools.partial(_qkv_kernel, n_q=n_q, rope=rope),
        grid=(b, seq // tm),
        in_specs=in_specs,
        out_specs=out_specs,
        out_shape=out_shape,
        compiler_params=_params(("parallel", "parallel"), 48),
        name="qkv_rope" if rope else "kv_ctx",
    )(*args)
    return outs if n_q else [None] + list(outs)


def _rope_tables(seq):
    half = HEAD_DIM // 4
    t = jnp.arange(seq, dtype=jnp.int32)
    pos = jnp.stack([t // GRID_W, t % GRID_W], axis=1).astype(F32)
    freqs = ROPE_THETA ** (-jnp.arange(half, dtype=F32) / half)
    ang = pos[:, :, None] * freqs[None, None, :]
    cos = jnp.cos(ang)
    sin = jnp.sin(ang)
    cos_h = jnp.concatenate([cos, cos], axis=2).reshape(seq, HEAD_DIM)
    sin_h = jnp.concatenate([-sin, sin], axis=2).reshape(seq, HEAD_DIM)
    return jnp.tile(cos_h, (1, N_KV_HEADS)), jnp.tile(sin_h, (1, N_KV_HEADS))


def _attn_kernel(q_ref, kt_ref, v_ref, o_ref, qs_scr, m_scr, acc_scr, *, tq):
    ki = pl.program_id(3)

    @pl.when(ki == 0)
    def _():
        m_scr[...] = jnp.full_like(m_scr, -jnp.inf)
        acc_scr[...] = jnp.zeros_like(acc_scr)
        lane = lax.broadcasted_iota(jnp.int32, (tq, 2 * HEAD_DIM), 1)
        for g in range(GROUP):
            pair = q_ref[0, :, (g // 2) * 2 * HEAD_DIM:(g // 2 + 1) * 2 * HEAD_DIM]
            keep = (lane >= HEAD_DIM) if g % 2 else (lane < HEAD_DIM)
            qs_scr[g * tq:(g + 1) * tq, :] = jnp.where(keep, pair, jnp.zeros_like(pair))

    s = _dot(qs_scr[...], kt_ref[0, 0])
    m_prev = m_scr[:, 0:1]
    m_next = jnp.maximum(m_prev, jnp.max(s, axis=1, keepdims=True))
    alpha = jnp.exp(m_prev - m_next)
    p = jnp.exp(s - m_next)
    acc_scr[...] = alpha * acc_scr[...] + _dot(p.astype(BF16), v_ref[0, 0])
    m_scr[...] = jnp.broadcast_to(m_next, m_scr.shape)

    @pl.when(ki == pl.num_programs(3) - 1)
    def _():
        acc = acc_scr[...]
        o = acc / pltpu.roll(acc, HEAD_DIM, 1)
        lane = lax.broadcasted_iota(jnp.int32, (tq, 2 * HEAD_DIM), 1)
        for j in range(GROUP // 2):
            even = o[(2 * j) * tq:(2 * j + 1) * tq, :]
            odd = pltpu.roll(o[(2 * j + 1) * tq:(2 * j + 2) * tq, :], HEAD_DIM, 1)
            o_ref[0, :, j * 2 * HEAD_DIM:(j + 1) * 2 * HEAD_DIM] = jnp.where(lane < HEAD_DIM, even, odd).astype(BF16)


def _attention(q, kt2, v_ext):
    b, seq, _ = q.shape
    lk = kt2.shape[-1]
    tq, tk = ATT_TQ, ATT_TK
    return pl.pallas_call(
        functools.partial(_attn_kernel, tq=tq),
        grid=(b, N_KV_HEADS, seq // tq, lk // tk),
        in_specs=[
            pl.BlockSpec((1, tq, KV_SLAB), lambda i, h, qi, ki: (i, qi, h)),
            pl.BlockSpec((1, 1, 2 * HEAD_DIM, tk), lambda i, h, qi, ki: (i, h, 0, ki)),
            pl.BlockSpec((1, 1, tk, 2 * HEAD_DIM), lambda i, h, qi, ki: (i, h, ki, 0)),
        ],
        out_specs=pl.BlockSpec((1, tq, KV_SLAB), lambda i, h, qi, ki: (i, qi, h)),
        out_shape=jax.ShapeDtypeStruct(q.shape, BF16),
        scratch_shapes=[
            pltpu.VMEM((GROUP * tq, 2 * HEAD_DIM), BF16),
            pltpu.VMEM((GROUP * tq, 2 * HEAD_DIM), F32),
            pltpu.VMEM((GROUP * tq, 2 * HEAD_DIM), F32),
        ],
        compiler_params=_params(("parallel", "parallel", "parallel", "arbitrary"), 56),
        name="gqa_attention",
    )(q, kt2, v_ext)


def _oproj_router_kernel(a_ref, x_ref, mod_ref, ng_ref, wo_ref, rw_ref, x_out, h_out, e_out, g_out):
    g1 = mod_ref[0, 2:3, :]
    sh, sc = mod_ref[0, 3:4, :], mod_ref[0, 4:5, :]
    x = x_ref[...] + g1 * _dot(a_ref[...], wo_ref[...])
    x_out[...] = x
    h = _rms(x, ng_ref[...]) * (1.0 + sc) + sh
    _store_rows(h_out, h)
    logits = lax.dot_general(rw_ref[...], h, (((1,), (1,)), ((), ())),
                             preferred_element_type=F32, precision=lax.Precision.HIGHEST)
    eid = lax.broadcasted_iota(jnp.int32, logits.shape, 0)
    m1 = jnp.max(logits, axis=0, keepdims=True)
    i1 = jnp.min(jnp.where(logits == m1, eid, N_EXPERTS), axis=0, keepdims=True)
    rest = jnp.where(eid == i1, -jnp.inf, logits)
    m2 = jnp.max(rest, axis=0, keepdims=True)
    i2 = jnp.min(jnp.where(rest == m2, eid, N_EXPERTS), axis=0, keepdims=True)
    e = jnp.exp(m2 - m1)
    e_out[...] = jnp.concatenate([i1, i2], axis=0)
    g_out[...] = jnp.concatenate([1.0 / (1.0 + e), e / (1.0 + e)], axis=0)


def _oproj_router_layer(attn, x, mods, norm_g, w_o, router_w):
    b, seq, d = x.shape
    t = b * seq
    tm = ROW_TILE
    per_batch = seq // tm
    row = lambda i: (i, 0)
    fixed = lambda i: (0, 0)
    return pl.pallas_call(
        _oproj_router_kernel,
        grid=(t // tm,),
        in_specs=[
            pl.BlockSpec((tm, d), row),
            pl.BlockSpec((tm, d), row),
            pl.BlockSpec((1, N_MOD, d), lambda i: (i // per_batch, 0, 0)),
            pl.BlockSpec((1, d), fixed),
            pl.BlockSpec((d, d), fixed),
            pl.BlockSpec((N_EXPERTS, d), fixed),
        ],
        out_specs=[
            pl.BlockSpec((tm, d), row),
            pl.BlockSpec((tm, ROW_SLABS, LANES), lambda i: (i, 0, 0)),
            pl.BlockSpec((TOP_K, tm), lambda i: (0, i)),
            pl.BlockSpec((TOP_K, tm), lambda i: (0, i)),
        ],
        out_shape=[
            jax.ShapeDtypeStruct((t, d), F32),
            jax.ShapeDtypeStruct((t, ROW_SLABS, LANES), F32),
            jax.ShapeDtypeStruct((TOP_K, t), jnp.int32),
            jax.ShapeDtypeStruct((TOP_K, t), F32),
        ],
        compiler_params=_params(("parallel",), 48),
        name="oproj_router",
    )(attn.reshape(t, d), x.reshape(t, d), mods, norm_g.reshape(1, d), w_o, router_w.T)


def _row_copy(src_ref, dst_ref, src_row, dst_row, sem):
    return pltpu.make_async_copy(src_ref.at[pl.ds(src_row, 1)], dst_ref.at[pl.ds(dst_row, 1)], sem)


def _dispatch_kernel(dest_ref, h_ref, init_ref, xs_ref, sem):
    del init_ref
    base = pl.program_id(0) * COPY_ROWS

    def issue(a, carry):
        tok = lax.shift_right_logical(base + a, 1)
        _row_copy(h_ref, xs_ref, tok, dest_ref[0, 0, a], sem).start()
        return carry

    def drain(a, carry):
        _row_copy(h_ref, xs_ref, 0, 0, sem).wait()
        return carry

    lax.fori_loop(0, COPY_ROWS, issue, 0)
    lax.fori_loop(0, COPY_ROWS, drain, 0)


def _combine_kernel(dest_ref, y_ref, out_ref, sem):
    base = pl.program_id(0) * COPY_ROWS

    def issue(a, carry):
        _row_copy(y_ref, out_ref, dest_ref[0, 0, a], base + a, sem).start()
        return carry

    def drain(a, carry):
        _row_copy(y_ref, out_ref, 0, 0, sem).wait()
        return carry

    lax.fori_loop(0, COPY_ROWS, issue, 0)
    lax.fori_loop(0, COPY_ROWS, drain, 0)


def _dest_spec():
    return pl.BlockSpec((1, 1, COPY_ROWS), lambda i: (i, 0, 0), memory_space=pltpu.SMEM)


def _dispatch(h, dest, cap):
    n_assign = dest.shape[0]
    steps = n_assign // COPY_ROWS
    any_spec = pl.BlockSpec(memory_space=pl.ANY)
    return pl.pallas_call(
        _dispatch_kernel,
        grid=(steps,),
        in_specs=[_dest_spec(), any_spec, any_spec],
        out_specs=any_spec,
        out_shape=jax.ShapeDtypeStruct((cap,) + h.shape[1:], h.dtype),
        scratch_shapes=[pltpu.SemaphoreType.DMA(())],
        input_output_aliases={2: 0},
        compiler_params=_params(("arbitrary",), 16),
        name="moe_dispatch",
    )(dest.reshape(steps, 1, COPY_ROWS), h, jnp.zeros((cap,) + h.shape[1:], h.dtype))


def _combine(y, dest):
    n_assign = dest.shape[0]
    steps = n_assign // COPY_ROWS
    any_spec = pl.BlockSpec(memory_space=pl.ANY)
    return pl.pallas_call(
        _combine_kernel,
        grid=(steps,),
        in_specs=[_dest_spec(), any_spec],
        out_specs=any_spec,
        out_shape=jax.ShapeDtypeStruct((n_assign,) + y.shape[1:], y.dtype),
        scratch_shapes=[pltpu.SemaphoreType.DMA(())],
        compiler_params=_params(("arbitrary",), 16),
        name="moe_combine",
    )(dest.reshape(steps, 1, COPY_ROWS), y)


def _ffn_expert_kernel(blk_e_ref, x_ref, w1_ref, w3_ref, w2_ref, o_ref, h_scr, acc_scr):
    del blk_e_ref
    f = pl.program_id(1)

    @pl.when(f == 0)
    def _():
        h_scr[...] = _load_rows(x_ref).astype(BF16)
        acc_scr[...] = jnp.zeros_like(acc_scr)

    h = h_scr[...]
    u = _silu(_dot(h, w1_ref[0])) * _dot(h, w3_ref[0])
    acc_scr[...] += _dot(u.astype(BF16), w2_ref[0])

    @pl.when(f == pl.num_programs(1) - 1)
    def _():
        _store_rows(o_ref, acc_scr[...])


def _ffn_experts(xs, blk_e, w1, w3, w2):
    cap = xs.shape[0]
    d = D_MODEL
    n_blk = cap // MOE_ROWS
    row_spec = pl.BlockSpec((MOE_ROWS, ROW_SLABS, LANES), lambda i, f, e: (i, 0, 0))
    grid_spec = pltpu.PrefetchScalarGridSpec(
        num_scalar_prefetch=1,
        grid=(n_blk, D_FF // FF_TILE),
        in_specs=[
            row_spec,
            pl.BlockSpec((1, d, FF_TILE), lambda i, f, e: (e[i], 0, f)),
            pl.BlockSpec((1, d, FF_TILE), lambda i, f, e: (e[i], 0, f)),
            pl.BlockSpec((1, FF_TILE, d), lambda i, f, e: (e[i], f, 0)),
        ],
        out_specs=row_spec,
        scratch_shapes=[pltpu.VMEM((MOE_ROWS, d), BF16), pltpu.VMEM((MOE_ROWS, d), F32)],
    )
    return pl.pallas_call(
        _ffn_expert_kernel,
        grid_spec=grid_spec,
        out_shape=jax.ShapeDtypeStruct(xs.shape, F32),
        compiler_params=_params(("parallel", "arbitrary"), 56),
        name="ffn_experts",
    )(blk_e, xs, w1, w3, w2)


def _routing_plan(e_idx):
    flat_e = e_idx.T.reshape(-1)
    n_assign = flat_e.shape[0]
    onehot = (flat_e[:, None] == jnp.arange(N_EXPERTS, dtype=jnp.int32)[None, :]).astype(jnp.int32)
    csum = jnp.cumsum(onehot, axis=0)
    sizes = csum[-1]
    padded = (sizes + MOE_ROWS - 1) // MOE_ROWS * MOE_ROWS
    ends = jnp.cumsum(padded)
    pstart = ends - padded
    dest = jnp.sum(onehot * (csum - 1 + pstart[None, :]), axis=1).astype(jnp.int32)
    cap = n_assign + N_EXPERTS * MOE_ROWS
    blk_start = jnp.arange(cap // MOE_ROWS, dtype=jnp.int32) * MOE_ROWS
    blk_e = jnp.sum((ends[None, :] <= blk_start[:, None]).astype(jnp.int32), axis=1)
    blk_e = jnp.minimum(blk_e, N_EXPERTS - 1)
    return dest, blk_e, cap


def _final_kernel(x_ref, y_ref, gt_ref, mod_ref, fg_ref, o_ref):
    gt = gt_ref[...]
    moe = gt[:, 0:1] * _load_rows(y_ref) + gt[:, 1:2] * _load_rows(y_ref, ROW_SLABS)
    x = x_ref[...] + mod_ref[0, 5:6, :] * moe
    o_ref[...] = _rms(x, fg_ref[...])


def _final_layer(x, y_pairs, gates_t, mods, final_g, seq):
    t, d = x.shape
    tm = ROW_TILE
    per_batch = seq // tm
    return pl.pallas_call(
        _final_kernel,
        grid=(t // tm,),
        in_specs=[
            pl.BlockSpec((tm, d), lambda i: (i, 0)),
            pl.BlockSpec((tm, TOP_K * ROW_SLABS, LANES), lambda i: (i, 0, 0)),
            pl.BlockSpec((tm, TOP_K), lambda i: (i, 0)),
            pl.BlockSpec((1, N_MOD, d), lambda i: (i // per_batch, 0, 0)),
            pl.BlockSpec((1, d), lambda i: (0, 0)),
        ],
        out_specs=pl.BlockSpec((tm, d), lambda i: (i, 0)),
        out_shape=jax.ShapeDtypeStruct((t, d), F32),
        compiler_params=_params(("parallel",), 48),
        name="moe_sum_final_norm",
    )(x, y_pairs, gates_t, mods, final_g.reshape(1, d))


def kernel(x, c, ctx, c_ctx, ada_w, ada_b, norm_g, pool_w, pool_scale, ffn_w1, ffn_w3, ffn_w2, w_qkv, w_o,
           q_norm_g, k_norm_g, router_w, moe_w1, moe_w3, moe_w2, final_g):
    b, seq, d = x.shape
    assert ada_w.shape[0] == 2 and b <= 7, "written for the two-layer block (pool+dense, attention+experts)"

    cvecs = jnp.zeros((8, d), F32).at[:b].set(c).at[b].set(c_ctx)
    mods = _adaln(cvecs, ada_w, ada_b).reshape(2, 8, N_MOD, d)
    mods_x = mods[:, :b]
    mods_c = jnp.broadcast_to(mods[:, b:b + 1], (2, b, N_MOD, d))

    pool_wb = pool_w[0].astype(BF16)
    w1b, w3b, w2b = ffn_w1[0].astype(BF16), ffn_w3[0].astype(BF16), ffn_w2[0].astype(BF16)
    x = _pool_layer(x, mods_x[0], norm_g[0, 0], pool_wb, pool_scale[0])
    ctx = _pool_layer(ctx, mods_c[0], norm_g[0, 0], pool_wb, pool_scale[0])
    x = _ffn_dense_layer(x, mods_x[0], norm_g[0, 1], w1b, w3b, w2b)
    ctx = _ffn_dense_layer(ctx, mods_c[0], norm_g[0, 1], w1b, w3b, w2b)

    wqkv = w_qkv[0].astype(BF16)
    q, kx, vx = _qkv_layer(x, mods_x[1], norm_g[1, 0], wqkv, q_norm_g[0], k_norm_g[0], _rope_tables(seq))
    _, kc, vc = _qkv_layer(ctx, mods_c[1], norm_g[1, 0], wqkv[:, Q_DIM:], q_norm_g[0], k_norm_g[0], None)
    k_all = jnp.concatenate([kx, kc], axis=1)
    v_all = jnp.concatenate([vx, vc], axis=1)
    lk = k_all.shape[1]
    kt = k_all.reshape(b, lk, N_KV_HEADS, HEAD_DIM).transpose(0, 2, 3, 1)
    kt2 = jnp.concatenate([kt, kt], axis=2)
    vh = v_all.reshape(b, lk, N_KV_HEADS, HEAD_DIM).transpose(0, 2, 1, 3)
    v_ext = jnp.concatenate([vh, jnp.ones_like(vh)], axis=3)
    attn = _attention(q, kt2, v_ext)

    x3, hf, e_idx, gates = _oproj_router_layer(attn, x, mods_x[1], norm_g[1, 1], w_o[0].astype(BF16), router_w[0])
    dest, blk_e, cap = _routing_plan(e_idx)
    xs = _dispatch(hf, dest, cap)
    ys = _ffn_experts(xs, blk_e, moe_w1[0].astype(BF16), moe_w3[0].astype(BF16), moe_w2[0].astype(BF16))
    y_pairs = _combine(ys, dest).reshape(b * seq, TOP_K * ROW_SLABS, LANES)
    out = _final_layer(x3, y_pairs, gates.T, mods_x[1], final_g, seq)
    return out.reshape(b, seq, d)
```

```python
import functools
import math

import jax
import jax.numpy as jnp
from jax import lax
from jax.experimental import pallas as pl
from jax.experimental.pallas import tpu as pltpu

F32 = jnp.float32
BF16 = jnp.bfloat16

D_MODEL = 1024
GRID_W = 64
POOL_WINDOWS = (2, 4, 8, 16)
POOL_GROUP_DIM = D_MODEL // len(POOL_WINDOWS)
HEAD_DIM = 64
N_KV_HEADS = 4
GROUP = 4
KV_SLAB = N_KV_HEADS * HEAD_DIM
Q_DIM = N_KV_HEADS * GROUP * HEAD_DIM
ROPE_THETA = 10000.0
Q_SCALE = HEAD_DIM ** -0.5 * math.log2(math.e)
D_FF = 2816
N_EXPERTS = 8
TOP_K = 2
NORM_EPS = 1e-6
N_MOD = 6

POOL_HALO = 16
ROW_TILE = 512
FF_TILE = 1408
MOE_ROWS = 512
COPY_ROWS = 512
ATT_TQ = 512
ATT_TK = 1280
MIB = 1024 * 1024
LANES = 128
ROW_SLABS = D_MODEL // LANES


def _dot(a, b):
    return jnp.dot(a, b, preferred_element_type=F32)


def _rms(x, g):
    ms = jnp.mean(x * x, axis=-1, keepdims=True)
    return x * lax.rsqrt(ms + NORM_EPS) * g


def _silu(x):
    return x * (1.0 / (1.0 + jnp.exp(-x)))


def _load_rows(ref, first=0):
    return jnp.concatenate([ref[:, first + s, :] for s in range(ROW_SLABS)], axis=1)


def _store_rows(ref, val):
    for s in range(ROW_SLABS):
        ref[:, s, :] = val[:, s * LANES:(s + 1) * LANES]


def _params(semantics, vmem_mib):
    return pltpu.CompilerParams(dimension_semantics=semantics, vmem_limit_bytes=vmem_mib * MIB)


def _adaln_kernel(c_ref, w_ref, b_ref, o_ref):
    s = _silu(c_ref[...])
    o_ref[0] = jnp.dot(s, w_ref[0], preferred_element_type=F32, precision=lax.Precision.HIGHEST) + b_ref[0]


def _adaln(cvecs, ada_w, ada_b):
    depth, d, n = ada_w.shape
    tn = n // 4
    return pl.pallas_call(
        _adaln_kernel,
        grid=(depth, n // tn),
        in_specs=[
            pl.BlockSpec((8, d), lambda i, j: (0, 0)),
            pl.BlockSpec((1, d, tn), lambda i, j: (i, 0, j)),
            pl.BlockSpec((1, 1, tn), lambda i, j: (i, 0, j)),
        ],
        out_specs=pl.BlockSpec((1, 8, tn), lambda i, j: (i, 0, j)),
        out_shape=jax.ShapeDtypeStruct((depth, 8, n), F32),
        compiler_params=_params(("parallel", "parallel"), 32),
        name="adaln",
    )(cvecs, ada_w, ada_b.reshape(depth, 1, n))


def _pool_kernel(xp_ref, xc_ref, xn_ref, mod_ref, ng_ref, pw_ref, ps_ref, o_ref, *, seq, tm):
    t = pl.program_id(1)
    xc = xc_ref[0]
    xa = jnp.concatenate([xp_ref[0], xc, xn_ref[0]], axis=0)
    sh, sc, gate = mod_ref[0, 0:1, :], mod_ref[0, 1:2, :], mod_ref[0, 2:3, :]
    h = _rms(xa, ng_ref[...]) * (1.0 + sc) + sh
    hb = h.astype(BF16)
    wide = tm + 2 * POOL_HALO
    row = lax.broadcasted_iota(jnp.int32, (tm, wide), 0) + t * tm
    col = lax.broadcasted_iota(jnp.int32, (tm, wide), 1) + (t * tm - POOL_HALO)
    row1 = lax.broadcasted_iota(jnp.int32, (tm, 1), 0) + t * tm
    ys = []
    for g, w in enumerate(POOL_WINDOWS):
        half = w // 2
        lo = jnp.maximum(row - half, 0)
        hi = jnp.minimum(row + half, seq)
        band = jnp.where((col >= lo) & (col < hi), 1.0, 0.0).astype(BF16)
        cnt = (jnp.minimum(row1 + half, seq) - jnp.maximum(row1 - half, 0)).astype(F32)
        lanes = slice(g * POOL_GROUP_DIM, (g + 1) * POOL_GROUP_DIM)
        win_sum = _dot(band, hb[:, lanes])
        d = win_sum / cnt - h[POOL_HALO:POOL_HALO + tm, lanes]
        ys.append(_dot(d.astype(BF16), pw_ref[g]))
    y = jnp.concatenate(ys, axis=1) * ps_ref[...]
    o_ref[0] = xc + gate * y


def _pool_layer(x, mods, norm_g, pool_w, pool_scale):
    b, seq, d = x.shape
    tm = min(ROW_TILE, seq)
    per = tm // POOL_HALO
    last = seq // POOL_HALO - 1
    return pl.pallas_call(
        functools.partial(_pool_kernel, seq=seq, tm=tm),
        grid=(b, seq // tm),
        in_specs=[
            pl.BlockSpec((1, POOL_HALO, d), lambda i, t: (i, jnp.maximum(t * per - 1, 0), 0)),
            pl.BlockSpec((1, tm, d), lambda i, t: (i, t, 0)),
            pl.BlockSpec((1, POOL_HALO, d), lambda i, t: (i, jnp.minimum((t + 1) * per, last), 0)),
            pl.BlockSpec((1, N_MOD, d), lambda i, t: (i, 0, 0)),
            pl.BlockSpec((1, d), lambda i, t: (0, 0)),
            pl.BlockSpec(pool_w.shape, lambda i, t: (0, 0, 0)),
            pl.BlockSpec((1, d), lambda i, t: (0, 0)),
        ],
        out_specs=pl.BlockSpec((1, tm, d), lambda i, t: (i, t, 0)),
        out_shape=jax.ShapeDtypeStruct(x.shape, F32),
        compiler_params=_params(("parallel", "parallel"), 48),
        name="pool_mix",
    )(x, x, x, mods, norm_g.reshape(1, d), pool_w, pool_scale.reshape(1, d))


def _ffn_dense_kernel(x_ref, mod_ref, ng_ref, w1_ref, w3_ref, w2_ref, o_ref, h_scr, acc_scr):
    f = pl.program_id(1)

    @pl.when(f == 0)
    def _():
        sh, sc = mod_ref[0, 3:4, :], mod_ref[0, 4:5, :]
        h_scr[...] = (_rms(x_ref[...], ng_ref[...]) * (1.0 + sc) + sh).astype(BF16)
        acc_scr[...] = jnp.zeros_like(acc_scr)

    h = h_scr[...]
    u = _silu(_dot(h, w1_ref[...])) * _dot(h, w3_ref[...])
    acc_scr[...] += _dot(u.astype(BF16), w2_ref[...])

    @pl.when(f == pl.num_programs(1) - 1)
    def _():
        o_ref[...] = x_ref[...] + mod_ref[0, 5:6, :] * acc_scr[...]


def _ffn_dense_layer(x, mods, norm_g, w1, w3, w2):
    b, seq, d = x.shape
    t = b * seq
    tm = min(ROW_TILE, seq)
    per_batch = seq // tm
    out = pl.pallas_call(
        _ffn_dense_kernel,
        grid=(t // tm, D_FF // FF_TILE),
        in_specs=[
            pl.BlockSpec((tm, d), lambda i, f: (i, 0)),
            pl.BlockSpec((1, N_MOD, d), lambda i, f: (i // per_batch, 0, 0)),
            pl.BlockSpec((1, d), lambda i, f: (0, 0)),
            pl.BlockSpec((d, FF_TILE), lambda i, f: (0, f)),
            pl.BlockSpec((d, FF_TILE), lambda i, f: (0, f)),
            pl.BlockSpec((FF_TILE, d), lambda i, f: (f, 0)),
        ],
        out_specs=pl.BlockSpec((tm, d), lambda i, f: (i, 0)),
        out_shape=jax.ShapeDtypeStruct((t, d), F32),
        scratch_shapes=[pltpu.VMEM((tm, d), BF16), pltpu.VMEM((tm, d), F32)],
        compiler_params=_params(("parallel", "arbitrary"), 56),
        name="ffn_dense",
    )(x.reshape(t, d), mods, norm_g.reshape(1, d), w1, w3, w2)
    return out.reshape(b, seq, d)


def _head_norm_rope(z, ones_blk, gain, cos, sin):
    sq = z * z
    hi = sq.astype(BF16)
    lo = (sq - hi.astype(F32)).astype(BF16)
    ss = _dot(hi, ones_blk) + _dot(lo, ones_blk)
    zn = z * lax.rsqrt(ss * (1.0 / HEAD_DIM) + NORM_EPS) * gain
    if cos is None:
        return zn
    width = z.shape[1]
    fwd = pltpu.roll(zn, width - 16, 1)
    bwd = pltpu.roll(zn, 16, 1)
    lane = lax.broadcasted_iota(jnp.int32, zn.shape, 1)
    partner = jnp.where((lane & 31) < 16, fwd, bwd)
    return zn * cos + partner * sin


def _qkv_kernel(*refs, n_q, rope):
    x_ref, mod_ref, ng_ref, w_ref, e_ref, qg_ref, kg_ref = refs[:7]
    refs = refs[7:]
    if rope:
        cos_ref, sin_ref = refs[:2]
        refs = refs[2:]
        cos, sin = cos_ref[...], sin_ref[...]
    else:
        cos = sin = None
    sh, sc = mod_ref[0, 0:1, :], mod_ref[0, 1:2, :]
    hb = (_rms(x_ref[0], ng_ref[...]) * (1.0 + sc) + sh).astype(BF16)
    qkv = _dot(hb, w_ref[...])
    ones_blk = e_ref[...]
    if n_q:
        q_ref, k_ref, v_ref = refs
        for c in range(n_q):
            z = _head_norm_rope(qkv[:, c * KV_SLAB:(c + 1) * KV_SLAB], ones_blk, qg_ref[...], cos, sin)
            q_ref[0, :, c * KV_SLAB:(c + 1) * KV_SLAB] = (z * Q_SCALE).astype(BF16)
    else:
        k_ref, v_ref = refs
    base = n_q * KV_SLAB
    k_ref[0] = _head_norm_rope(qkv[:, base:base + KV_SLAB], ones_blk, kg_ref[...], cos, sin).astype(BF16)
    v_ref[0] = qkv[:, base + KV_SLAB:base + 2 * KV_SLAB].astype(BF16)


def _qkv_layer(x, mods, norm_g, w, q_gain, k_gain, tables):
    b, seq, d = x.shape
    tm = min(ROW_TILE, seq)
    width = w.shape[1]
    n_q = (width - 2 * KV_SLAB) // KV_SLAB
    rope = tables is not None
    seg = lax.broadcasted_iota(jnp.int32, (KV_SLAB, KV_SLAB), 0) // HEAD_DIM
    ones_blk = (seg == seg.T).astype(BF16)
    in_specs = [
        pl.BlockSpec((1, tm, d), lambda i, t: (i, t, 0)),
        pl.BlockSpec((1, N_MOD, d), lambda i, t: (i, 0, 0)),
        pl.BlockSpec((1, d), lambda i, t: (0, 0)),
        pl.BlockSpec((d, width), lambda i, t: (0, 0)),
        pl.BlockSpec((KV_SLAB, KV_SLAB), lambda i, t: (0, 0)),
        pl.BlockSpec((1, KV_SLAB), lambda i, t: (0, 0)),
        pl.BlockSpec((1, KV_SLAB), lambda i, t: (0, 0)),
    ]
    args = [x, mods, norm_g.reshape(1, d), w, ones_blk,
            jnp.tile(q_gain, N_KV_HEADS).reshape(1, KV_SLAB), jnp.tile(k_gain, N_KV_HEADS).reshape(1, KV_SLAB)]
    if rope:
        in_specs += [pl.BlockSpec((tm, KV_SLAB), lambda i, t: (t, 0))] * 2
        args += list(tables)
    kv_spec = pl.BlockSpec((1, tm, KV_SLAB), lambda i, t: (i, t, 0))
    kv_shape = jax.ShapeDtypeStruct((b, seq, KV_SLAB), BF16)
    out_specs, out_shape = [kv_spec, kv_spec], [kv_shape, kv_shape]
    if n_q:
        out_specs = [pl.BlockSpec((1, tm, n_q * KV_SLAB), lambda i, t: (i, t, 0))] + out_specs
        out_shape = [jax.ShapeDtypeStruct((b, seq, n_q * KV_SLAB), BF16)] + out_shape
    outs = pl.pallas_call(
        functools.partial(_qkv_kernel, n_q=n_q, rope=rope),
        grid=(b, seq // tm),
        in_specs=in_specs,
        out_specs=out_specs,
        out_shape=out_shape,
        compiler_params=_params(("parallel", "parallel"), 48),
        name="qkv_rope" if rope else "kv_ctx",
    )(*args)
    return outs if n_q else [None] + list(outs)


def _rope_tables(seq):
    half = HEAD_DIM // 4
    t = jnp.arange(seq, dtype=jnp.int32)
    pos = jnp.stack([t // GRID_W, t % GRID_W], axis=1).astype(F32)
    freqs = ROPE_THETA ** (-jnp.arange(half, dtype=F32) / half)
    ang = pos[:, :, None] * freqs[None, None, :]
    cos = jnp.cos(ang)
    sin = jnp.sin(ang)
    cos_h = jnp.concatenate([cos, cos], axis=2).reshape(seq, HEAD_DIM)
    sin_h = jnp.concatenate([-sin, sin], axis=2).reshape(seq, HEAD_DIM)
    return jnp.tile(cos_h, (1, N_KV_HEADS)), jnp.tile(sin_h, (1, N_KV_HEADS))


def _attn_kernel(q_ref, kt_ref, v_ref, o_ref, qs_scr, m_scr, acc_scr, *, tq, n_kblk):
    lane = lax.broadcasted_iota(jnp.int32, (tq, 2 * HEAD_DIM), 1)
    for g in range(GROUP):
        pair = q_ref[0, :, (g // 2) * 2 * HEAD_DIM:(g // 2 + 1) * 2 * HEAD_DIM]
        keep = (lane >= HEAD_DIM) if g % 2 else (lane < HEAD_DIM)
        qs_scr[g] = jnp.where(keep, pair, jnp.zeros_like(pair))
    m_scr[...] = jnp.full_like(m_scr, -jnp.inf)
    acc_scr[...] = jnp.zeros_like(acc_scr)

    def key_block(k, carry):
        kt = kt_ref[0, 0, k]
        v = v_ref[0, 0, k]
        for g in range(GROUP):
            s = _dot(qs_scr[g], kt)
            m_prev = m_scr[g, :, 0:1]
            m_next = jnp.maximum(m_prev, jnp.max(s, axis=1, keepdims=True))
            alpha = jnp.exp2(m_prev - m_next)
            p = jnp.exp2(s - m_next)
            acc_scr[g] = alpha * acc_scr[g] + _dot(p.astype(BF16), v)
            m_scr[g] = jnp.broadcast_to(m_next, (tq, 2 * HEAD_DIM))
        return carry

    lax.fori_loop(0, n_kblk, key_block, 0)

    for j in range(GROUP // 2):
        halves = []
        for g in (2 * j, 2 * j + 1):
            acc = acc_scr[g]
            halves.append(acc / pltpu.roll(acc, HEAD_DIM, 1))
        pair = jnp.where(lane < HEAD_DIM, halves[0], pltpu.roll(halves[1], HEAD_DIM, 1))
        o_ref[0, :, j * 2 * HEAD_DIM:(j + 1) * 2 * HEAD_DIM] = pair.astype(BF16)


def _attention(q, kt2, v_ext):
    b, seq, _ = q.shape
    n_kblk, tk = kt2.shape[2], kt2.shape[4]
    tq = ATT_TQ
    return pl.pallas_call(
        functools.partial(_attn_kernel, tq=tq, n_kblk=n_kblk),
        grid=(b, N_KV_HEADS, seq // tq),
        in_specs=[
            pl.BlockSpec((1, tq, KV_SLAB), lambda i, h, qi: (i, qi, h)),
            pl.BlockSpec((1, 1, n_kblk, 2 * HEAD_DIM, tk), lambda i, h, qi: (i, h, 0, 0, 0)),
            pl.BlockSpec((1, 1, n_kblk, tk, 2 * HEAD_DIM), lambda i, h, qi: (i, h, 0, 0, 0)),
        ],
        out_specs=pl.BlockSpec((1, tq, KV_SLAB), lambda i, h, qi: (i, qi, h)),
        out_shape=jax.ShapeDtypeStruct(q.shape, BF16),
        scratch_shapes=[
            pltpu.VMEM((GROUP, tq, 2 * HEAD_DIM), BF16),
            pltpu.VMEM((GROUP, tq, 2 * HEAD_DIM), F32),
            pltpu.VMEM((GROUP, tq, 2 * HEAD_DIM), F32),
        ],
        compiler_params=_params(("parallel", "parallel", "parallel"), 56),
        name="gqa_attention",
    )(q, kt2, v_ext)


def _oproj_router_kernel(a_ref, x_ref, mod_ref, ng_ref, wo_ref, rw_ref, x_out, h_out, e_out, g_out):
    g1 = mod_ref[0, 2:3, :]
    sh, sc = mod_ref[0, 3:4, :], mod_ref[0, 4:5, :]
    x = x_ref[...] + g1 * _dot(a_ref[...], wo_ref[...])
    x_out[...] = x
    h = _rms(x, ng_ref[...]) * (1.0 + sc) + sh
    _store_rows(h_out, h)
    logits = lax.dot_general(rw_ref[...], h, (((1,), (1,)), ((), ())),
                             preferred_element_type=F32, precision=lax.Precision.HIGHEST)
    eid = lax.broadcasted_iota(jnp.int32, logits.shape, 0)
    m1 = jnp.max(logits, axis=0, keepdims=True)
    i1 = jnp.min(jnp.where(logits == m1, eid, N_EXPERTS), axis=0, keepdims=True)
    rest = jnp.where(eid == i1, -jnp.inf, logits)
    m2 = jnp.max(rest, axis=0, keepdims=True)
    i2 = jnp.min(jnp.where(rest == m2, eid, N_EXPERTS), axis=0, keepdims=True)
    e = jnp.exp(m2 - m1)
    e_out[...] = jnp.concatenate([i1, i2], axis=0)
    g_out[...] = jnp.concatenate([1.0 / (1.0 + e), e / (1.0 + e)], axis=0)


def _oproj_router_layer(attn, x, mods, norm_g, w_o, router_w):
    b, seq, d = x.shape
    t = b * seq
    tm = ROW_TILE
    per_batch = seq // tm
    row = lambda i: (i, 0)
    fixed = lambda i: (0, 0)
    return pl.pallas_call(
        _oproj_router_kernel,
        grid=(t // tm,),
        in_specs=[
            pl.BlockSpec((tm, d), row),
            pl.BlockSpec((tm, d), row),
            pl.BlockSpec((1, N_MOD, d), lambda i: (i // per_batch, 0, 0)),
            pl.BlockSpec((1, d), fixed),
            pl.BlockSpec((d, d), fixed),
            pl.BlockSpec((N_EXPERTS, d), fixed),
        ],
        out_specs=[
            pl.BlockSpec((tm, d), row),
            pl.BlockSpec((tm, ROW_SLABS, LANES), lambda i: (i, 0, 0)),
            pl.BlockSpec((TOP_K, tm), lambda i: (0, i)),
            pl.BlockSpec((TOP_K, tm), lambda i: (0, i)),
        ],
        out_shape=[
            jax.ShapeDtypeStruct((t, d), F32),
            jax.ShapeDtypeStruct((t, ROW_SLABS, LANES), F32),
            jax.ShapeDtypeStruct((TOP_K, t), jnp.int32),
            jax.ShapeDtypeStruct((TOP_K, t), F32),
        ],
        compiler_params=_params(("parallel",), 48),
        name="oproj_router",
    )(attn.reshape(t, d), x.reshape(t, d), mods, norm_g.reshape(1, d), w_o, router_w.T)


def _row_copy(src_ref, dst_ref, src_row, dst_row, sem):
    return pltpu.make_async_copy(src_ref.at[pl.ds(src_row, 1)], dst_ref.at[pl.ds(dst_row, 1)], sem)


def _rows_copy(src_ref, dst_ref, rows, sem):
    return pltpu.make_async_copy(src_ref.at[pl.ds(0, rows)], dst_ref.at[pl.ds(0, rows)], sem)


def _dest_spec(tm):
    return pl.BlockSpec((1, 1, TOP_K * tm), lambda i: (i, 0, 0), memory_space=pltpu.SMEM)


def _dispatch_kernel(dest_ref, h_ref, init_ref, xs_ref, sem):
    del init_ref
    tm = h_ref.shape[0]

    def issue(r, carry):
        for k in range(TOP_K):
            _row_copy(h_ref, xs_ref, r, dest_ref[0, 0, TOP_K * r + k], sem).start()
        return carry

    lax.fori_loop(0, tm, issue, 0)
    for _ in range(TOP_K):
        _rows_copy(h_ref, xs_ref, tm, sem).wait()


def _dispatch(h, dest, cap):
    t = h.shape[0]
    tm = ROW_TILE
    any_spec = pl.BlockSpec(memory_space=pl.ANY)
    return pl.pallas_call(
        _dispatch_kernel,
        grid=(t // tm,),
        in_specs=[_dest_spec(tm), pl.BlockSpec((tm, ROW_SLABS, LANES), lambda i: (i, 0, 0)), any_spec],
        out_specs=any_spec,
        out_shape=jax.ShapeDtypeStruct((cap,) + h.shape[1:], h.dtype),
        scratch_shapes=[pltpu.SemaphoreType.DMA(())],
        input_output_aliases={2: 0},
        compiler_params=_params(("arbitrary",), 32),
        name="moe_dispatch",
    )(dest.reshape(t // tm, 1, TOP_K * tm), h, jnp.zeros((cap,) + h.shape[1:], h.dtype))


def _ffn_expert_kernel(blk_e_ref, x_ref, w1_ref, w3_ref, w2_ref, o_ref, h_scr, acc_scr):
    del blk_e_ref
    f = pl.program_id(1)

    @pl.when(f == 0)
    def _():
        h_scr[...] = _load_rows(x_ref).astype(BF16)
        acc_scr[...] = jnp.zeros_like(acc_scr)

    h = h_scr[...]
    u = _silu(_dot(h, w1_ref[0])) * _dot(h, w3_ref[0])
    acc_scr[...] += _dot(u.astype(BF16), w2_ref[0])

    @pl.when(f == pl.num_programs(1) - 1)
    def _():
        _store_rows(o_ref, acc_scr[...])


def _ffn_experts(xs, blk_e, w1, w3, w2):
    cap = xs.shape[0]
    d = D_MODEL
    n_blk = cap // MOE_ROWS
    row_spec = pl.BlockSpec((MOE_ROWS, ROW_SLABS, LANES), lambda i, f, e: (i, 0, 0))
    grid_spec = pltpu.PrefetchScalarGridSpec(
        num_scalar_prefetch=1,
        grid=(n_blk, D_FF // FF_TILE),
        in_specs=[
            row_spec,
            pl.BlockSpec((1, d, FF_TILE), lambda i, f, e: (e[i], 0, f)),
            pl.BlockSpec((1, d, FF_TILE), lambda i, f, e: (e[i], 0, f)),
            pl.BlockSpec((1, FF_TILE, d), lambda i, f, e: (e[i], f, 0)),
        ],
        out_specs=row_spec,
        scratch_shapes=[pltpu.VMEM((MOE_ROWS, d), BF16), pltpu.VMEM((MOE_ROWS, d), F32)],
    )
    return pl.pallas_call(
        _ffn_expert_kernel,
        grid_spec=grid_spec,
        out_shape=jax.ShapeDtypeStruct(xs.shape, F32),
        compiler_params=_params(("parallel", "arbitrary"), 56),
        name="ffn_experts",
    )(blk_e, xs, w1, w3, w2)


def _routing_plan(e_idx):
    flat_e = e_idx.T.reshape(-1)
    n_assign = flat_e.shape[0]
    onehot = (flat_e[:, None] == jnp.arange(N_EXPERTS, dtype=jnp.int32)[None, :]).astype(jnp.int32)
    csum = jnp.cumsum(onehot, axis=0)
    sizes = csum[-1]
    padded = (sizes + MOE_ROWS - 1) // MOE_ROWS * MOE_ROWS
    ends = jnp.cumsum(padded)
    pstart = ends - padded
    dest = jnp.sum(onehot * (csum - 1 + pstart[None, :]), axis=1).astype(jnp.int32)
    cap = n_assign + N_EXPERTS * MOE_ROWS
    blk_start = jnp.arange(cap // MOE_ROWS, dtype=jnp.int32) * MOE_ROWS
    blk_e = jnp.sum((ends[None, :] <= blk_start[:, None]).astype(jnp.int32), axis=1)
    blk_e = jnp.minimum(blk_e, N_EXPERTS - 1)
    return dest, blk_e, cap


def _final_kernel(dest_ref, x_ref, gt_ref, mod_ref, fg_ref, y_ref, o_ref, y_buf, sem):
    tm = x_ref.shape[0]

    def issue(r, carry):
        for k in range(TOP_K):
            _row_copy(y_ref, y_buf, dest_ref[0, 0, TOP_K * r + k], k * tm + r, sem).start()
        return carry

    lax.fori_loop(0, tm, issue, 0)
    _rows_copy(y_ref, y_buf, TOP_K * tm, sem).wait()
    gt = gt_ref[...]
    moe = gt[:, 0:1] * _load_rows(y_buf.at[pl.ds(0, tm)]) + gt[:, 1:2] * _load_rows(y_buf.at[pl.ds(tm, tm)])
    x = x_ref[...] + mod_ref[0, 5:6, :] * moe
    o_ref[...] = _rms(x, fg_ref[...])


def _final_layer(x, ys, dest, gates_t, mods, final_g, seq):
    t, d = x.shape
    tm = ROW_TILE
    per_batch = seq // tm
    return pl.pallas_call(
        _final_kernel,
        grid=(t // tm,),
        in_specs=[
            _dest_spec(tm),
            pl.BlockSpec((tm, d), lambda i: (i, 0)),
            pl.BlockSpec((tm, TOP_K), lambda i: (i, 0)),
            pl.BlockSpec((1, N_MOD, d), lambda i: (i // per_batch, 0, 0)),
            pl.BlockSpec((1, d), lambda i: (0, 0)),
            pl.BlockSpec(memory_space=pl.ANY),
        ],
        out_specs=pl.BlockSpec((tm, d), lambda i: (i, 0)),
        out_shape=jax.ShapeDtypeStruct((t, d), F32),
        scratch_shapes=[pltpu.VMEM((TOP_K * tm, ROW_SLABS, LANES), F32), pltpu.SemaphoreType.DMA(())],
        compiler_params=_params(("arbitrary",), 48),
        name="moe_sum_final_norm",
    )(dest.reshape(t // tm, 1, TOP_K * tm), x, gates_t, mods, final_g.reshape(1, d), ys)


def kernel(x, c, ctx, c_ctx, ada_w, ada_b, norm_g, pool_w, pool_scale, ffn_w1, ffn_w3, ffn_w2, w_qkv, w_o,
           q_norm_g, k_norm_g, router_w, moe_w1, moe_w3, moe_w2, final_g):
    b, seq, d = x.shape
    assert ada_w.shape[0] == 2 and b <= 7, "written for the two-layer block (pool+dense, attention+experts)"

    cvecs = jnp.zeros((8, d), F32).at[:b].set(c).at[b].set(c_ctx)
    mods = _adaln(cvecs, ada_w, ada_b).reshape(2, 8, N_MOD, d)
    mods_x = mods[:, :b]
    mods_c = jnp.broadcast_to(mods[:, b:b + 1], (2, b, N_MOD, d))

    pool_wb = pool_w[0].astype(BF16)
    w1b, w3b, w2b = ffn_w1[0].astype(BF16), ffn_w3[0].astype(BF16), ffn_w2[0].astype(BF16)
    x = _pool_layer(x, mods_x[0], norm_g[0, 0], pool_wb, pool_scale[0])
    ctx = _pool_layer(ctx, mods_c[0], norm_g[0, 0], pool_wb, pool_scale[0])
    x = _ffn_dense_layer(x, mods_x[0], norm_g[0, 1], w1b, w3b, w2b)
    ctx = _ffn_dense_layer(ctx, mods_c[0], norm_g[0, 1], w1b, w3b, w2b)

    wqkv = w_qkv[0].astype(BF16)
    q, kx, vx = _qkv_layer(x, mods_x[1], norm_g[1, 0], wqkv, q_norm_g[0], k_norm_g[0], _rope_tables(seq))
    _, kc, vc = _qkv_layer(ctx, mods_c[1], norm_g[1, 0], wqkv[:, Q_DIM:], q_norm_g[0], k_norm_g[0], None)
    k_all = jnp.concatenate([kx, kc], axis=1)
    v_all = jnp.concatenate([vx, vc], axis=1)
    lk = k_all.shape[1]
    n_kblk = lk // ATT_TK
    kt = k_all.reshape(b, n_kblk, ATT_TK, N_KV_HEADS, HEAD_DIM).transpose(0, 3, 1, 4, 2)
    kt2 = jnp.concatenate([kt, kt], axis=3)
    vh = v_all.reshape(b, n_kblk, ATT_TK, N_KV_HEADS, HEAD_DIM).transpose(0, 3, 1, 2, 4)
    v_ext = jnp.concatenate([vh, jnp.ones_like(vh)], axis=4)
    attn = _attention(q, kt2, v_ext)

    x3, hf, e_idx, gates = _oproj_router_layer(attn, x, mods_x[1], norm_g[1, 1], w_o[0].astype(BF16), router_w[0])
    dest, blk_e, cap = _routing_plan(e_idx)
    xs = _dispatch(hf, dest, cap)
    ys = _ffn_experts(xs, blk_e, moe_w1[0].astype(BF16), moe_w3[0].astype(BF16), moe_w2[0].astype(BF16))
    out = _final_layer(x3, ys, dest, gates.T, mods_x[1], final_g, seq)
    return out.reshape(b, seq, d)
```

```python
import functools
import math

import jax
import jax.numpy as jnp
from jax import lax
from jax.experimental import pallas as pl
from jax.experimental.pallas import tpu as pltpu

F32 = jnp.float32
BF16 = jnp.bfloat16

D_MODEL = 1024
GRID_W = 64
POOL_WINDOWS = (2, 4, 8, 16)
POOL_GROUP_DIM = D_MODEL // len(POOL_WINDOWS)
HEAD_DIM = 64
N_KV_HEADS = 4
GROUP = 4
KV_SLAB = N_KV_HEADS * HEAD_DIM
Q_DIM = N_KV_HEADS * GROUP * HEAD_DIM
ROPE_THETA = 10000.0
Q_SCALE = HEAD_DIM ** -0.5 * math.log2(math.e)
D_FF = 2816
N_EXPERTS = 8
TOP_K = 2
NORM_EPS = 1e-6
N_MOD = 6

POOL_HALO = 16
ROW_TILE = 512
FF_TILE = 1408
MOE_ROWS = 512
COPY_ROWS = 512
ATT_TQ = 512
ATT_TK = 1280
MIB = 1024 * 1024
LANES = 128
ROW_SLABS = D_MODEL // LANES


def _dot(a, b):
    return jnp.dot(a, b, preferred_element_type=F32)


def _rms(x, g):
    ms = jnp.mean(x * x, axis=-1, keepdims=True)
    return x * lax.rsqrt(ms + NORM_EPS) * g


def _silu(x):
    return x * (1.0 / (1.0 + jnp.exp(-x)))


def _load_rows(ref, first=0):
    return jnp.concatenate([ref[:, first + s, :] for s in range(ROW_SLABS)], axis=1)


def _store_rows(ref, val):
    for s in range(ROW_SLABS):
        ref[:, s, :] = val[:, s * LANES:(s + 1) * LANES]


def _params(semantics, vmem_mib):
    return pltpu.CompilerParams(dimension_semantics=semantics, vmem_limit_bytes=vmem_mib * MIB)


def _adaln_kernel(c_ref, w_ref, b_ref, o_ref):
    s = _silu(c_ref[...])
    o_ref[0] = jnp.dot(s, w_ref[0], preferred_element_type=F32, precision=lax.Precision.HIGHEST) + b_ref[0]


def _adaln(cvecs, ada_w, ada_b):
    depth, d, n = ada_w.shape
    tn = n // 4
    return pl.pallas_call(
        _adaln_kernel,
        grid=(depth, n // tn),
        in_specs=[
            pl.BlockSpec((8, d), lambda i, j: (0, 0)),
            pl.BlockSpec((1, d, tn), lambda i, j: (i, 0, j)),
            pl.BlockSpec((1, 1, tn), lambda i, j: (i, 0, j)),
        ],
        out_specs=pl.BlockSpec((1, 8, tn), lambda i, j: (i, 0, j)),
        out_shape=jax.ShapeDtypeStruct((depth, 8, n), F32),
        compiler_params=_params(("parallel", "parallel"), 32),
        name="adaln",
    )(cvecs, ada_w, ada_b.reshape(depth, 1, n))


def _pool_kernel(xp_ref, xc_ref, xn_ref, mod_ref, ng_ref, pw_ref, ps_ref, o_ref, *, seq, tm):
    t = pl.program_id(1)
    xc = xc_ref[0]
    xa = jnp.concatenate([xp_ref[0], xc, xn_ref[0]], axis=0)
    sh, sc, gate = mod_ref[0, 0:1, :], mod_ref[0, 1:2, :], mod_ref[0, 2:3, :]
    h = _rms(xa, ng_ref[...]) * (1.0 + sc) + sh
    hb = h.astype(BF16)
    wide = tm + 2 * POOL_HALO
    row = lax.broadcasted_iota(jnp.int32, (tm, wide), 0) + t * tm
    col = lax.broadcasted_iota(jnp.int32, (tm, wide), 1) + (t * tm - POOL_HALO)
    row1 = lax.broadcasted_iota(jnp.int32, (tm, 1), 0) + t * tm
    ys = []
    for g, w in enumerate(POOL_WINDOWS):
        half = w // 2
        lo = jnp.maximum(row - half, 0)
        hi = jnp.minimum(row + half, seq)
        band = jnp.where((col >= lo) & (col < hi), 1.0, 0.0).astype(BF16)
        cnt = (jnp.minimum(row1 + half, seq) - jnp.maximum(row1 - half, 0)).astype(F32)
        lanes = slice(g * POOL_GROUP_DIM, (g + 1) * POOL_GROUP_DIM)
        win_sum = _dot(band, hb[:, lanes])
        d = win_sum / cnt - h[POOL_HALO:POOL_HALO + tm, lanes]
        ys.append(_dot(d.astype(BF16), pw_ref[g]))
    y = jnp.concatenate(ys, axis=1) * ps_ref[...]
    o_ref[0] = xc + gate * y


def _pool_layer(x, mods, norm_g, pool_w, pool_scale):
    b, seq, d = x.shape
    tm = min(ROW_TILE, seq)
    per = tm // POOL_HALO
    last = seq // POOL_HALO - 1
    return pl.pallas_call(
        functools.partial(_pool_kernel, seq=seq, tm=tm),
        grid=(b, seq // tm),
        in_specs=[
            pl.BlockSpec((1, POOL_HALO, d), lambda i, t: (i, jnp.maximum(t * per - 1, 0), 0)),
            pl.BlockSpec((1, tm, d), lambda i, t: (i, t, 0)),
            pl.BlockSpec((1, POOL_HALO, d), lambda i, t: (i, jnp.minimum((t + 1) * per, last), 0)),
            pl.BlockSpec((1, N_MOD, d), lambda i, t: (i, 0, 0)),
            pl.BlockSpec((1, d), lambda i, t: (0, 0)),
            pl.BlockSpec(pool_w.shape, lambda i, t: (0, 0, 0)),
            pl.BlockSpec((1, d), lambda i, t: (0, 0)),
        ],
        out_specs=pl.BlockSpec((1, tm, d), lambda i, t: (i, t, 0)),
        out_shape=jax.ShapeDtypeStruct(x.shape, F32),
        compiler_params=_params(("parallel", "parallel"), 48),
        name="pool_mix",
    )(x, x, x, mods, norm_g.reshape(1, d), pool_w, pool_scale.reshape(1, d))


def _ffn_dense_kernel(x_ref, mod_ref, ng_ref, w1_ref, w3_ref, w2_ref, o_ref, h_scr, acc_scr):
    f = pl.program_id(1)

    @pl.when(f == 0)
    def _():
        sh, sc = mod_ref[0, 3:4, :], mod_ref[0, 4:5, :]
        h_scr[...] = (_rms(x_ref[...], ng_ref[...]) * (1.0 + sc) + sh).astype(BF16)
        acc_scr[...] = jnp.zeros_like(acc_scr)

    h = h_scr[...]
    u = _silu(_dot(h, w1_ref[...])) * _dot(h, w3_ref[...])
    acc_scr[...] += _dot(u.astype(BF16), w2_ref[...])

    @pl.when(f == pl.num_programs(1) - 1)
    def _():
        o_ref[...] = x_ref[...] + mod_ref[0, 5:6, :] * acc_scr[...]


def _ffn_dense_layer(x, mods, norm_g, w1, w3, w2):
    b, seq, d = x.shape
    t = b * seq
    tm = min(ROW_TILE, seq)
    per_batch = seq // tm
    out = pl.pallas_call(
        _ffn_dense_kernel,
        grid=(t // tm, D_FF // FF_TILE),
        in_specs=[
            pl.BlockSpec((tm, d), lambda i, f: (i, 0)),
            pl.BlockSpec((1, N_MOD, d), lambda i, f: (i // per_batch, 0, 0)),
            pl.BlockSpec((1, d), lambda i, f: (0, 0)),
            pl.BlockSpec((d, FF_TILE), lambda i, f: (0, f)),
            pl.BlockSpec((d, FF_TILE), lambda i, f: (0, f)),
            pl.BlockSpec((FF_TILE, d), lambda i, f: (f, 0)),
        ],
        out_specs=pl.BlockSpec((tm, d), lambda i, f: (i, 0)),
        out_shape=jax.ShapeDtypeStruct((t, d), F32),
        scratch_shapes=[pltpu.VMEM((tm, d), BF16), pltpu.VMEM((tm, d), F32)],
        compiler_params=_params(("parallel", "arbitrary"), 56),
        name="ffn_dense",
    )(x.reshape(t, d), mods, norm_g.reshape(1, d), w1, w3, w2)
    return out.reshape(b, seq, d)


def _head_norm_rope(z, ones_blk, gain, cos, sin):
    sq = z * z
    hi = sq.astype(BF16)
    lo = (sq - hi.astype(F32)).astype(BF16)
    ss = _dot(hi, ones_blk) + _dot(lo, ones_blk)
    zn = z * lax.rsqrt(ss * (1.0 / HEAD_DIM) + NORM_EPS) * gain
    if cos is None:
        return zn
    width = z.shape[1]
    fwd = pltpu.roll(zn, width - 16, 1)
    bwd = pltpu.roll(zn, 16, 1)
    lane = lax.broadcasted_iota(jnp.int32, zn.shape, 1)
    partner = jnp.where((lane & 31) < 16, fwd, bwd)
    return zn * cos + partner * sin


def _qkv_kernel(*refs, n_q, rope):
    x_ref, mod_ref, ng_ref, w_ref, e_ref, qg_ref, kg_ref = refs[:7]
    refs = refs[7:]
    if rope:
        cos_ref, sin_ref = refs[:2]
        refs = refs[2:]
        cos, sin = cos_ref[...], sin_ref[...]
    else:
        cos = sin = None
    sh, sc = mod_ref[0, 0:1, :], mod_ref[0, 1:2, :]
    hb = (_rms(x_ref[0], ng_ref[...]) * (1.0 + sc) + sh).astype(BF16)
    qkv = _dot(hb, w_ref[...])
    ones_blk = e_ref[...]
    if n_q:
        q_ref, k_ref, v_ref = refs
        for c in range(n_q):
            z = _head_norm_rope(qkv[:, c * KV_SLAB:(c + 1) * KV_SLAB], ones_blk, qg_ref[...], cos, sin)
            q_ref[0, :, c * KV_SLAB:(c + 1) * KV_SLAB] = (z * Q_SCALE).astype(BF16)
    else:
        k_ref, v_ref = refs
    base = n_q * KV_SLAB
    k_ref[0] = _head_norm_rope(qkv[:, base:base + KV_SLAB], ones_blk, kg_ref[...], cos, sin).astype(BF16)
    v_ref[0] = qkv[:, base + KV_SLAB:base + 2 * KV_SLAB].astype(BF16)


def _qkv_layer(x, mods, norm_g, w, q_gain, k_gain, tables):
    b, seq, d = x.shape
    tm = min(ROW_TILE, seq)
    width = w.shape[1]
    n_q = (width - 2 * KV_SLAB) // KV_SLAB
    rope = tables is not None
    seg = lax.broadcasted_iota(jnp.int32, (KV_SLAB, KV_SLAB), 0) // HEAD_DIM
    ones_blk = (seg == seg.T).astype(BF16)
    in_specs = [
        pl.BlockSpec((1, tm, d), lambda i, t: (i, t, 0)),
        pl.BlockSpec((1, N_MOD, d), lambda i, t: (i, 0, 0)),
        pl.BlockSpec((1, d), lambda i, t: (0, 0)),
        pl.BlockSpec((d, width), lambda i, t: (0, 0)),
        pl.BlockSpec((KV_SLAB, KV_SLAB), lambda i, t: (0, 0)),
        pl.BlockSpec((1, KV_SLAB), lambda i, t: (0, 0)),
        pl.BlockSpec((1, KV_SLAB), lambda i, t: (0, 0)),
    ]
    args = [x, mods, norm_g.reshape(1, d), w, ones_blk,
            jnp.tile(q_gain, N_KV_HEADS).reshape(1, KV_SLAB), jnp.tile(k_gain, N_KV_HEADS).reshape(1, KV_SLAB)]
    if rope:
        in_specs += [pl.BlockSpec((tm, KV_SLAB), lambda i, t: (t, 0))] * 2
        args += list(tables)
    kv_spec = pl.BlockSpec((1, tm, KV_SLAB), lambda i, t: (i, t, 0))
    kv_shape = jax.ShapeDtypeStruct((b, seq, KV_SLAB), BF16)
    out_specs, out_shape = [kv_spec, kv_spec], [kv_shape, kv_shape]
    if n_q:
        out_specs = [pl.BlockSpec((1, tm, n_q * KV_SLAB), lambda i, t: (i, t, 0))] + out_specs
        out_shape = [jax.ShapeDtypeStruct((b, seq, n_q * KV_SLAB), BF16)] + out_shape
    outs = pl.pallas_call(
        functools.partial(_qkv_kernel, n_q=n_q, rope=rope),
        grid=(b, seq // tm),
        in_specs=in_specs,
        out_specs=out_specs,
        out_shape=out_shape,
        compiler_params=_params(("parallel", "parallel"), 48),
        name="qkv_rope" if rope else "kv_ctx",
    )(*args)
    return outs if n_q else [None] + list(outs)


def _rope_tables(seq):
    half = HEAD_DIM // 4
    t = jnp.arange(seq, dtype=jnp.int32)
    pos = jnp.stack([t // GRID_W, t % GRID_W], axis=1).astype(F32)
    freqs = ROPE_THETA ** (-jnp.arange(half, dtype=F32) / half)
    ang = pos[:, :, None] * freqs[None, None, :]
    cos = jnp.cos(ang)
    sin = jnp.sin(ang)
    cos_h = jnp.concatenate([cos, cos], axis=2).reshape(seq, HEAD_DIM)
    sin_h = jnp.concatenate([-sin, sin], axis=2).reshape(seq, HEAD_DIM)
    return jnp.tile(cos_h, (1, N_KV_HEADS)), jnp.tile(sin_h, (1, N_KV_HEADS))


def _attn_kernel(q_ref, kt_ref, v_ref, o_ref, qs_scr, m_scr, acc_scr, sa_scr, sb_scr, smaxa_scr, smaxb_scr, *,
                 tq, n_kblk):
    lane = lax.broadcasted_iota(jnp.int32, (tq, 2 * HEAD_DIM), 1)
    for g in range(GROUP):
        pair = q_ref[0, :, (g // 2) * 2 * HEAD_DIM:(g // 2 + 1) * 2 * HEAD_DIM]
        keep = (lane >= HEAD_DIM) if g % 2 else (lane < HEAD_DIM)
        qs_scr[g] = jnp.where(keep, pair, jnp.zeros_like(pair))
    m_scr[...] = jnp.full_like(m_scr, -jnp.inf)
    acc_scr[...] = jnp.zeros_like(acc_scr)

    def scores(k, buf):
        s_buf, smax_buf = buf
        kt = kt_ref[0, 0, k]
        for g in range(GROUP):
            s = _dot(qs_scr[g], kt)
            s_buf[g] = s
            smax_buf[g] = jnp.broadcast_to(jnp.max(s, axis=1, keepdims=True), (tq, 2 * HEAD_DIM))

    def accumulate(k, buf):
        s_buf, smax_buf = buf
        v = v_ref[0, 0, k]
        for g in range(GROUP):
            m_prev = m_scr[g]
            m_next = jnp.maximum(m_prev, smax_buf[g])
            alpha = jnp.exp2(m_prev - m_next)
            p = jnp.exp2(s_buf[g] - m_next[:, 0:1])
            acc_scr[g] = alpha * acc_scr[g] + _dot(p.astype(BF16), v)
            m_scr[g] = m_next

    buf_a, buf_b = (sa_scr, smaxa_scr), (sb_scr, smaxb_scr)
    scores(0, buf_a)

    def key_block_pair(j, carry):
        k = 2 * j
        scores(k + 1, buf_b)
        accumulate(k, buf_a)
        scores(k + 2, buf_a)
        accumulate(k + 1, buf_b)
        return carry

    n_pairs = (n_kblk - 1) // 2
    lax.fori_loop(0, n_pairs, key_block_pair, 0)
    if n_kblk % 2:
        accumulate(n_kblk - 1, buf_a)
    else:
        scores(n_kblk - 1, buf_b)
        accumulate(n_kblk - 2, buf_a)
        accumulate(n_kblk - 1, buf_b)

    for j in range(GROUP // 2):
        halves = []
        for g in (2 * j, 2 * j + 1):
            acc = acc_scr[g]
            halves.append(acc / pltpu.roll(acc, HEAD_DIM, 1))
        pair = jnp.where(lane < HEAD_DIM, halves[0], pltpu.roll(halves[1], HEAD_DIM, 1))
        o_ref[0, :, j * 2 * HEAD_DIM:(j + 1) * 2 * HEAD_DIM] = pair.astype(BF16)


def _attention(q, kt2, v_ext):
    b, seq, _ = q.shape
    n_kblk, tk = kt2.shape[2], kt2.shape[4]
    tq = ATT_TQ
    return pl.pallas_call(
        functools.partial(_attn_kernel, tq=tq, n_kblk=n_kblk),
        grid=(b, N_KV_HEADS, seq // tq),
        in_specs=[
            pl.BlockSpec((1, tq, KV_SLAB), lambda i, h, qi: (i, qi, h)),
            pl.BlockSpec((1, 1, n_kblk, 2 * HEAD_DIM, tk), lambda i, h, qi: (i, h, 0, 0, 0)),
            pl.BlockSpec((1, 1, n_kblk, tk, 2 * HEAD_DIM), lambda i, h, qi: (i, h, 0, 0, 0)),
        ],
        out_specs=pl.BlockSpec((1, tq, KV_SLAB), lambda i, h, qi: (i, qi, h)),
        out_shape=jax.ShapeDtypeStruct(q.shape, BF16),
        scratch_shapes=[
            pltpu.VMEM((GROUP, tq, 2 * HEAD_DIM), BF16),
            pltpu.VMEM((GROUP, tq, 2 * HEAD_DIM), F32),
            pltpu.VMEM((GROUP, tq, 2 * HEAD_DIM), F32),
            pltpu.VMEM((GROUP, tq, tk), F32),
            pltpu.VMEM((GROUP, tq, tk), F32),
            pltpu.VMEM((GROUP, tq, 2 * HEAD_DIM), F32),
            pltpu.VMEM((GROUP, tq, 2 * HEAD_DIM), F32),
        ],
        compiler_params=_params(("parallel", "parallel", "parallel"), 60),
        name="gqa_attention",
    )(q, kt2, v_ext)


def _oproj_router_kernel(a_ref, x_ref, mod_ref, ng_ref, wo_ref, rw_ref, x_out, h_out, e_out, g_out):
    g1 = mod_ref[0, 2:3, :]
    sh, sc = mod_ref[0, 3:4, :], mod_ref[0, 4:5, :]
    x = x_ref[...] + g1 * _dot(a_ref[...], wo_ref[...])
    x_out[...] = x
    h = _rms(x, ng_ref[...]) * (1.0 + sc) + sh
    _store_rows(h_out, h)
    logits = lax.dot_general(rw_ref[...], h, (((1,), (1,)), ((), ())),
                             preferred_element_type=F32, precision=lax.Precision.HIGHEST)
    eid = lax.broadcasted_iota(jnp.int32, logits.shape, 0)
    m1 = jnp.max(logits, axis=0, keepdims=True)
    i1 = jnp.min(jnp.where(logits == m1, eid, N_EXPERTS), axis=0, keepdims=True)
    rest = jnp.where(eid == i1, -jnp.inf, logits)
    m2 = jnp.max(rest, axis=0, keepdims=True)
    i2 = jnp.min(jnp.where(rest == m2, eid, N_EXPERTS), axis=0, keepdims=True)
    e = jnp.exp(m2 - m1)
    e_out[...] = jnp.concatenate([i1, i2], axis=0)
    g_out[...] = jnp.concatenate([1.0 / (1.0 + e), e / (1.0 + e)], axis=0)


def _oproj_router_layer(attn, x, mods, norm_g, w_o, router_w):
    b, seq, d = x.shape
    t = b * seq
    tm = ROW_TILE
    per_batch = seq // tm
    row = lambda i: (i, 0)
    fixed = lambda i: (0, 0)
    return pl.pallas_call(
        _oproj_router_kernel,
        grid=(t // tm,),
        in_specs=[
            pl.BlockSpec((tm, d), row),
            pl.BlockSpec((tm, d), row),
            pl.BlockSpec((1, N_MOD, d), lambda i: (i // per_batch, 0, 0)),
            pl.BlockSpec((1, d), fixed),
            pl.BlockSpec((d, d), fixed),
            pl.BlockSpec((N_EXPERTS, d), fixed),
        ],
        out_specs=[
            pl.BlockSpec((tm, d), row),
            pl.BlockSpec((tm, ROW_SLABS, LANES), lambda i: (i, 0, 0)),
            pl.BlockSpec((TOP_K, tm), lambda i: (0, i)),
            pl.BlockSpec((TOP_K, tm), lambda i: (0, i)),
        ],
        out_shape=[
            jax.ShapeDtypeStruct((t, d), F32),
            jax.ShapeDtypeStruct((t, ROW_SLABS, LANES), F32),
            jax.ShapeDtypeStruct((TOP_K, t), jnp.int32),
            jax.ShapeDtypeStruct((TOP_K, t), F32),
        ],
        compiler_params=_params(("parallel",), 48),
        name="oproj_router",
    )(attn.reshape(t, d), x.reshape(t, d), mods, norm_g.reshape(1, d), w_o, router_w.T)


def _row_copy(src_ref, dst_ref, src_row, dst_row, sem):
    return pltpu.make_async_copy(src_ref.at[pl.ds(src_row, 1)], dst_ref.at[pl.ds(dst_row, 1)], sem)


def _rows_copy(src_ref, dst_ref, rows, sem):
    return pltpu.make_async_copy(src_ref.at[pl.ds(0, rows)], dst_ref.at[pl.ds(0, rows)], sem)


def _dest_spec(tm):
    return pl.BlockSpec((1, 1, TOP_K * tm), lambda i: (i, 0, 0), memory_space=pltpu.SMEM)


def _dispatch_kernel(dest_ref, h_ref, init_ref, xs_ref, sem):
    del init_ref
    tm = h_ref.shape[0]

    def issue(r, carry):
        for k in range(TOP_K):
            _row_copy(h_ref, xs_ref, r, dest_ref[0, 0, TOP_K * r + k], sem).start()
        return carry

    lax.fori_loop(0, tm, issue, 0)
    for _ in range(TOP_K):
        _rows_copy(h_ref, xs_ref, tm, sem).wait()


def _dispatch(h, dest, cap):
    t = h.shape[0]
    tm = ROW_TILE
    any_spec = pl.BlockSpec(memory_space=pl.ANY)
    return pl.pallas_call(
        _dispatch_kernel,
        grid=(t // tm,),
        in_specs=[_dest_spec(tm), pl.BlockSpec((tm, ROW_SLABS, LANES), lambda i: (i, 0, 0)), any_spec],
        out_specs=any_spec,
        out_shape=jax.ShapeDtypeStruct((cap,) + h.shape[1:], h.dtype),
        scratch_shapes=[pltpu.SemaphoreType.DMA(())],
        input_output_aliases={2: 0},
        compiler_params=_params(("arbitrary",), 32),
        name="moe_dispatch",
    )(dest.reshape(t // tm, 1, TOP_K * tm), h, jnp.zeros((cap,) + h.shape[1:], h.dtype))


def _ffn_expert_kernel(blk_e_ref, x_ref, w1_ref, w3_ref, w2_ref, o_ref, h_scr, acc_scr):
    del blk_e_ref
    f = pl.program_id(1)

    @pl.when(f == 0)
    def _():
        h_scr[...] = _load_rows(x_ref).astype(BF16)
        acc_scr[...] = jnp.zeros_like(acc_scr)

    h = h_scr[...]
    u = _silu(_dot(h, w1_ref[0])) * _dot(h, w3_ref[0])
    acc_scr[...] += _dot(u.astype(BF16), w2_ref[0])

    @pl.when(f == pl.num_programs(1) - 1)
    def _():
        _store_rows(o_ref, acc_scr[...])


def _ffn_experts(xs, blk_e, w1, w3, w2):
    cap = xs.shape[0]
    d = D_MODEL
    n_blk = cap // MOE_ROWS
    row_spec = pl.BlockSpec((MOE_ROWS, ROW_SLABS, LANES), lambda i, f, e: (i, 0, 0))
    grid_spec = pltpu.PrefetchScalarGridSpec(
        num_scalar_prefetch=1,
        grid=(n_blk, D_FF // FF_TILE),
        in_specs=[
            row_spec,
            pl.BlockSpec((1, d, FF_TILE), lambda i, f, e: (e[i], 0, f)),
            pl.BlockSpec((1, d, FF_TILE), lambda i, f, e: (e[i], 0, f)),
            pl.BlockSpec((1, FF_TILE, d), lambda i, f, e: (e[i], f, 0)),
        ],
        out_specs=row_spec,
        scratch_shapes=[pltpu.VMEM((MOE_ROWS, d), BF16), pltpu.VMEM((MOE_ROWS, d), F32)],
    )
    return pl.pallas_call(
        _ffn_expert_kernel,
        grid_spec=grid_spec,
        out_shape=jax.ShapeDtypeStruct(xs.shape, F32),
        compiler_params=_params(("parallel", "arbitrary"), 56),
        name="ffn_experts",
    )(blk_e, xs, w1, w3, w2)


def _routing_plan(e_idx):
    flat_e = e_idx.T.reshape(-1)
    n_assign = flat_e.shape[0]
    onehot = (flat_e[:, None] == jnp.arange(N_EXPERTS, dtype=jnp.int32)[None, :]).astype(jnp.int32)
    csum = jnp.cumsum(onehot, axis=0)
    sizes = csum[-1]
    padded = (sizes + MOE_ROWS - 1) // MOE_ROWS * MOE_ROWS
    ends = jnp.cumsum(padded)
    pstart = ends - padded
    dest = jnp.sum(onehot * (csum - 1 + pstart[None, :]), axis=1).astype(jnp.int32)
    cap = n_assign + N_EXPERTS * MOE_ROWS
    blk_start = jnp.arange(cap // MOE_ROWS, dtype=jnp.int32) * MOE_ROWS
    blk_e = jnp.sum((ends[None, :] <= blk_start[:, None]).astype(jnp.int32), axis=1)
    blk_e = jnp.minimum(blk_e, N_EXPERTS - 1)
    return dest, blk_e, cap


def _final_kernel(dest_ref, x_ref, gt_ref, mod_ref, fg_ref, y_ref, o_ref, y_buf, sem):
    tm = x_ref.shape[0]

    def issue(r, carry):
        for k in range(TOP_K):
            _row_copy(y_ref, y_buf, dest_ref[0, 0, TOP_K * r + k], k * tm + r, sem).start()
        return carry

    lax.fori_loop(0, tm, issue, 0)
    _rows_copy(y_ref, y_buf, TOP_K * tm, sem).wait()
    gt = gt_ref[...]
    moe = gt[:, 0:1] * _load_rows(y_buf.at[pl.ds(0, tm)]) + gt[:, 1:2] * _load_rows(y_buf.at[pl.ds(tm, tm)])
    x = x_ref[...] + mod_ref[0, 5:6, :] * moe
    o_ref[...] = _rms(x, fg_ref[...])


def _final_layer(x, ys, dest, gates_t, mods, final_g, seq):
    t, d = x.shape
    tm = ROW_TILE
    per_batch = seq // tm
    return pl.pallas_call(
        _final_kernel,
        grid=(t // tm,),
        in_specs=[
            _dest_spec(tm),
            pl.BlockSpec((tm, d), lambda i: (i, 0)),
            pl.BlockSpec((tm, TOP_K), lambda i: (i, 0)),
            pl.BlockSpec((1, N_MOD, d), lambda i: (i // per_batch, 0, 0)),
            pl.BlockSpec((1, d), lambda i: (0, 0)),
            pl.BlockSpec(memory_space=pl.ANY),
        ],
        out_specs=pl.BlockSpec((tm, d), lambda i: (i, 0)),
        out_shape=jax.ShapeDtypeStruct((t, d), F32),
        scratch_shapes=[pltpu.VMEM((TOP_K * tm, ROW_SLABS, LANES), F32), pltpu.SemaphoreType.DMA(())],
        compiler_params=_params(("arbitrary",), 48),
        name="moe_sum_final_norm",
    )(dest.reshape(t // tm, 1, TOP_K * tm), x, gates_t, mods, final_g.reshape(1, d), ys)


def kernel(x, c, ctx, c_ctx, ada_w, ada_b, norm_g, pool_w, pool_scale, ffn_w1, ffn_w3, ffn_w2, w_qkv, w_o,
           q_norm_g, k_norm_g, router_w, moe_w1, moe_w3, moe_w2, final_g):
    b, seq, d = x.shape
    assert ada_w.shape[0] == 2 and b <= 7, "written for the two-layer block (pool+dense, attention+experts)"

    cvecs = jnp.zeros((8, d), F32).at[:b].set(c).at[b].set(c_ctx)
    mods = _adaln(cvecs, ada_w, ada_b).reshape(2, 8, N_MOD, d)
    mods_x = mods[:, :b]
    mods_c = jnp.broadcast_to(mods[:, b:b + 1], (2, b, N_MOD, d))

    pool_wb = pool_w[0].astype(BF16)
    w1b, w3b, w2b = ffn_w1[0].astype(BF16), ffn_w3[0].astype(BF16), ffn_w2[0].astype(BF16)
    x = _pool_layer(x, mods_x[0], norm_g[0, 0], pool_wb, pool_scale[0])
    ctx = _pool_layer(ctx, mods_c[0], norm_g[0, 0], pool_wb, pool_scale[0])
    x = _ffn_dense_layer(x, mods_x[0], norm_g[0, 1], w1b, w3b, w2b)
    ctx = _ffn_dense_layer(ctx, mods_c[0], norm_g[0, 1], w1b, w3b, w2b)

    wqkv = w_qkv[0].astype(BF16)
    q, kx, vx = _qkv_layer(x, mods_x[1], norm_g[1, 0], wqkv, q_norm_g[0], k_norm_g[0], _rope_tables(seq))
    _, kc, vc = _qkv_layer(ctx, mods_c[1], norm_g[1, 0], wqkv[:, Q_DIM:], q_norm_g[0], k_norm_g[0], None)
    k_all = jnp.concatenate([kx, kc], axis=1)
    v_all = jnp.concatenate([vx, vc], axis=1)
    lk = k_all.shape[1]
    n_kblk = lk // ATT_TK
    kt = k_all.reshape(b, n_kblk, ATT_TK, N_KV_HEADS, HEAD_DIM).transpose(0, 3, 1, 4, 2)
    kt2 = jnp.concatenate([kt, kt], axis=3)
    vh = v_all.reshape(b, n_kblk, ATT_TK, N_KV_HEADS, HEAD_DIM).transpose(0, 3, 1, 2, 4)
    v_ext = jnp.concatenate([vh, jnp.ones_like(vh)], axis=4)
    attn = _attention(q, kt2, v_ext)

    x3, hf, e_idx, gates = _oproj_router_layer(attn, x, mods_x[1], norm_g[1, 1], w_o[0].astype(BF16), router_w[0])
    dest, blk_e, cap = _routing_plan(e_idx)
    xs = _dispatch(hf, dest, cap)
    ys = _ffn_experts(xs, blk_e, moe_w1[0].astype(BF16), moe_w3[0].astype(BF16), moe_w2[0].astype(BF16))
    out = _final_layer(x3, ys, dest, gates.T, mods_x[1], final_g, seq)
    return out.reshape(b, seq, d)
```

```python
import functools
import math

import jax
import jax.numpy as jnp
from jax import lax
from jax.experimental import pallas as pl
from jax.experimental.pallas import tpu as pltpu

F32 = jnp.float32
BF16 = jnp.bfloat16

D_MODEL = 1024
GRID_W = 64
POOL_WINDOWS = (2, 4, 8, 16)
POOL_GROUP_DIM = D_MODEL // len(POOL_WINDOWS)
HEAD_DIM = 64
N_KV_HEADS = 4
GROUP = 4
KV_SLAB = N_KV_HEADS * HEAD_DIM
Q_DIM = N_KV_HEADS * GROUP * HEAD_DIM
ROPE_THETA = 10000.0
Q_SCALE = HEAD_DIM ** -0.5 * math.log2(math.e)
D_FF = 2816
N_EXPERTS = 8
TOP_K = 2
NORM_EPS = 1e-6
N_MOD = 6

POOL_HALO = 16
ROW_TILE = 512
FF_TILE = 1408
MOE_ROWS = 512
DISPATCH_TILE = 2048
ATT_TQ = 512
ATT_TK = 1280
ATT_TILES = 8
MIB = 1024 * 1024
LANES = 128
ROW_SLABS = D_MODEL // LANES


def _dot(a, b):
    return jnp.dot(a, b, preferred_element_type=F32)


def _rms(x, g):
    ms = jnp.mean(x * x, axis=-1, keepdims=True)
    return x * lax.rsqrt(ms + NORM_EPS) * g


def _silu(x):
    return x * (1.0 / (1.0 + jnp.exp(-x)))


def _load_rows(ref, n_rows, first=0):
    return jnp.concatenate(
        [ref[pl.ds(ROW_SLABS * first + s, n_rows, stride=ROW_SLABS), :] for s in range(ROW_SLABS)], axis=1)


def _store_rows(ref, val):
    for s in range(ROW_SLABS):
        ref[pl.ds(s, val.shape[0], stride=ROW_SLABS), :] = val[:, s * LANES:(s + 1) * LANES]


def _tile_of_row(ref, row):
    return ref.at[pl.ds(pl.multiple_of(row * ROW_SLABS, ROW_SLABS), ROW_SLABS)]


def _params(semantics, vmem_mib):
    return pltpu.CompilerParams(dimension_semantics=semantics, vmem_limit_bytes=vmem_mib * MIB)


def _adaln_kernel(c_ref, w_ref, b_ref, o_ref):
    s = _silu(c_ref[...])
    o_ref[0] = jnp.dot(s, w_ref[0], preferred_element_type=F32, precision=lax.Precision.HIGHEST) + b_ref[0]


def _adaln(cvecs, ada_w, ada_b):
    depth, d, n = ada_w.shape
    tn = n // 4
    return pl.pallas_call(
        _adaln_kernel,
        grid=(depth, n // tn),
        in_specs=[
            pl.BlockSpec((8, d), lambda i, j: (0, 0)),
            pl.BlockSpec((1, d, tn), lambda i, j: (i, 0, j)),
            pl.BlockSpec((1, 1, tn), lambda i, j: (i, 0, j)),
        ],
        out_specs=pl.BlockSpec((1, 8, tn), lambda i, j: (i, 0, j)),
        out_shape=jax.ShapeDtypeStruct((depth, 8, n), F32),
        compiler_params=_params(("parallel", "parallel"), 32),
        name="adaln",
    )(cvecs, ada_w, ada_b.reshape(depth, 1, n))


def _pool_kernel(xp_ref, xc_ref, xn_ref, mod_ref, ng_ref, pw_ref, ps_ref, o_ref, *, seq, tm):
    t = pl.program_id(1)
    xc = xc_ref[0]
    xa = jnp.concatenate([xp_ref[0], xc, xn_ref[0]], axis=0)
    sh, sc, gate = mod_ref[0, 0:1, :], mod_ref[0, 1:2, :], mod_ref[0, 2:3, :]
    h = _rms(xa, ng_ref[...]) * (1.0 + sc) + sh
    hb = h.astype(BF16)
    wide = tm + 2 * POOL_HALO
    row = lax.broadcasted_iota(jnp.int32, (tm, wide), 0) + t * tm
    col = lax.broadcasted_iota(jnp.int32, (tm, wide), 1) + (t * tm - POOL_HALO)
    row1 = lax.broadcasted_iota(jnp.int32, (tm, 1), 0) + t * tm
    ys = []
    for g, w in enumerate(POOL_WINDOWS):
        half = w // 2
        lo = jnp.maximum(row - half, 0)
        hi = jnp.minimum(row + half, seq)
        band = jnp.where((col >= lo) & (col < hi), 1.0, 0.0).astype(BF16)
        cnt = (jnp.minimum(row1 + half, seq) - jnp.maximum(row1 - half, 0)).astype(F32)
        lanes = slice(g * POOL_GROUP_DIM, (g + 1) * POOL_GROUP_DIM)
        win_sum = _dot(band, hb[:, lanes])
        d = win_sum / cnt - h[POOL_HALO:POOL_HALO + tm, lanes]
        ys.append(_dot(d.astype(BF16), pw_ref[g]))
    y = jnp.concatenate(ys, axis=1) * ps_ref[...]
    o_ref[0] = xc + gate * y


def _pool_layer(x, mods, norm_g, pool_w, pool_scale):
    b, seq, d = x.shape
    tm = min(ROW_TILE, seq)
    per = tm // POOL_HALO
    last = seq // POOL_HALO - 1
    return pl.pallas_call(
        functools.partial(_pool_kernel, seq=seq, tm=tm),
        grid=(b, seq // tm),
        in_specs=[
            pl.BlockSpec((1, POOL_HALO, d), lambda i, t: (i, jnp.maximum(t * per - 1, 0), 0)),
            pl.BlockSpec((1, tm, d), lambda i, t: (i, t, 0)),
            pl.BlockSpec((1, POOL_HALO, d), lambda i, t: (i, jnp.minimum((t + 1) * per, last), 0)),
            pl.BlockSpec((1, N_MOD, d), lambda i, t: (i, 0, 0)),
            pl.BlockSpec((1, d), lambda i, t: (0, 0)),
            pl.BlockSpec(pool_w.shape, lambda i, t: (0, 0, 0)),
            pl.BlockSpec((1, d), lambda i, t: (0, 0)),
        ],
        out_specs=pl.BlockSpec((1, tm, d), lambda i, t: (i, t, 0)),
        out_shape=jax.ShapeDtypeStruct(x.shape, F32),
        compiler_params=_params(("parallel", "parallel"), 48),
        name="pool_mix",
    )(x, x, x, mods, norm_g.reshape(1, d), pool_w, pool_scale.reshape(1, d))


def _swiglu(h, w1_ref, w3_ref, w2_ref):
    acc = None
    for f in range(D_FF // FF_TILE):
        cols = slice(f * FF_TILE, (f + 1) * FF_TILE)
        u = _silu(_dot(h, w1_ref[:, cols])) * _dot(h, w3_ref[:, cols])
        part = _dot(u.astype(BF16), w2_ref[cols, :])
        acc = part if acc is None else acc + part
    return acc


def _ffn_dense_kernel(x_ref, mod_ref, ng_ref, w1_ref, w3_ref, w2_ref, o_ref):
    sh, sc, gate = mod_ref[0, 3:4, :], mod_ref[0, 4:5, :], mod_ref[0, 5:6, :]
    x = x_ref[...]
    h = (_rms(x, ng_ref[...]) * (1.0 + sc) + sh).astype(BF16)
    o_ref[...] = x + gate * _swiglu(h, w1_ref, w3_ref, w2_ref)


def _ffn_dense_layer(x, mods, norm_g, w1, w3, w2):
    b, seq, d = x.shape
    t = b * seq
    tm = min(ROW_TILE, seq)
    per_batch = seq // tm
    fixed = lambda i: (0, 0)
    out = pl.pallas_call(
        _ffn_dense_kernel,
        grid=(t // tm,),
        in_specs=[
            pl.BlockSpec((tm, d), lambda i: (i, 0)),
            pl.BlockSpec((1, N_MOD, d), lambda i: (i // per_batch, 0, 0)),
            pl.BlockSpec((1, d), fixed),
            pl.BlockSpec((d, D_FF), fixed),
            pl.BlockSpec((d, D_FF), fixed),
            pl.BlockSpec((D_FF, d), fixed),
        ],
        out_specs=pl.BlockSpec((tm, d), lambda i: (i, 0)),
        out_shape=jax.ShapeDtypeStruct((t, d), F32),
        compiler_params=_params(("parallel",), 60),
        name="ffn_dense",
    )(x.reshape(t, d), mods, norm_g.reshape(1, d), w1, w3, w2)
    return out.reshape(b, seq, d)


def _head_norm_rope(z, ones_blk, gain, cos, sin):
    sq = z * z
    hi = sq.astype(BF16)
    lo = (sq - hi.astype(F32)).astype(BF16)
    ss = _dot(hi, ones_blk) + _dot(lo, ones_blk)
    zn = z * lax.rsqrt(ss * (1.0 / HEAD_DIM) + NORM_EPS) * gain
    if cos is None:
        return zn
    width = z.shape[1]
    fwd = pltpu.roll(zn, width - 16, 1)
    bwd = pltpu.roll(zn, 16, 1)
    lane = lax.broadcasted_iota(jnp.int32, zn.shape, 1)
    partner = jnp.where((lane & 31) < 16, fwd, bwd)
    return zn * cos + partner * sin


def _qkv_kernel(*refs, n_q, rope):
    x_ref, mod_ref, ng_ref, w_ref, e_ref, qg_ref, kg_ref = refs[:7]
    refs = refs[7:]
    if rope:
        cos_ref, sin_ref = refs[:2]
        refs = refs[2:]
        cos, sin = cos_ref[...], sin_ref[...]
    else:
        cos = sin = None
    sh, sc = mod_ref[0, 0:1, :], mod_ref[0, 1:2, :]
    hb = (_rms(x_ref[0], ng_ref[...]) * (1.0 + sc) + sh).astype(BF16)
    qkv = _dot(hb, w_ref[...])
    ones_blk = e_ref[...]
    if n_q:
        q_ref, k_ref, v_ref = refs
        for c in range(n_q):
            z = _head_norm_rope(qkv[:, c * KV_SLAB:(c + 1) * KV_SLAB], ones_blk, qg_ref[...], cos, sin)
            q_ref[0, :, c * KV_SLAB:(c + 1) * KV_SLAB] = (z * Q_SCALE).astype(BF16)
    else:
        k_ref, v_ref = refs
    base = n_q * KV_SLAB
    k_ref[0] = _head_norm_rope(qkv[:, base:base + KV_SLAB], ones_blk, kg_ref[...], cos, sin).astype(BF16)
    v_ref[0] = qkv[:, base + KV_SLAB:base + 2 * KV_SLAB].astype(BF16)


def _qkv_layer(x, mods, norm_g, w, q_gain, k_gain, tables):
    b, seq, d = x.shape
    tm = min(ROW_TILE, seq)
    width = w.shape[1]
    n_q = (width - 2 * KV_SLAB) // KV_SLAB
    rope = tables is not None
    seg = lax.broadcasted_iota(jnp.int32, (KV_SLAB, KV_SLAB), 0) // HEAD_DIM
    ones_blk = (seg == seg.T).astype(BF16)
    in_specs = [
        pl.BlockSpec((1, tm, d), lambda i, t: (i, t, 0)),
        pl.BlockSpec((1, N_MOD, d), lambda i, t: (i, 0, 0)),
        pl.BlockSpec((1, d), lambda i, t: (0, 0)),
        pl.BlockSpec((d, width), lambda i, t: (0, 0)),
        pl.BlockSpec((KV_SLAB, KV_SLAB), lambda i, t: (0, 0)),
        pl.BlockSpec((1, KV_SLAB), lambda i, t: (0, 0)),
        pl.BlockSpec((1, KV_SLAB), lambda i, t: (0, 0)),
    ]
    args = [x, mods, norm_g.reshape(1, d), w, ones_blk,
            jnp.tile(q_gain, N_KV_HEADS).reshape(1, KV_SLAB), jnp.tile(k_gain, N_KV_HEADS).reshape(1, KV_SLAB)]
    if rope:
        in_specs += [pl.BlockSpec((tm, KV_SLAB), lambda i, t: (t, 0))] * 2
        args += list(tables)
    kv_spec = pl.BlockSpec((1, tm, KV_SLAB), lambda i, t: (i, t, 0))
    kv_shape = jax.ShapeDtypeStruct((b, seq, KV_SLAB), BF16)
    out_specs, out_shape = [kv_spec, kv_spec], [kv_shape, kv_shape]
    if n_q:
        out_specs = [pl.BlockSpec((1, tm, n_q * KV_SLAB), lambda i, t: (i, t, 0))] + out_specs
        out_shape = [jax.ShapeDtypeStruct((b, seq, n_q * KV_SLAB), BF16)] + out_shape
    outs = pl.pallas_call(
        functools.partial(_qkv_kernel, n_q=n_q, rope=rope),
        grid=(b, seq // tm),
        in_specs=in_specs,
        out_specs=out_specs,
        out_shape=out_shape,
        compiler_params=_params(("parallel", "parallel"), 48),
        name="qkv_rope" if rope else "kv_ctx",
    )(*args)
    return outs if n_q else [None] + list(outs)


def _rope_tables(seq):
    half = HEAD_DIM // 4
    t = jnp.arange(seq, dtype=jnp.int32)
    pos = jnp.stack([t // GRID_W, t % GRID_W], axis=1).astype(F32)
    freqs = ROPE_THETA ** (-jnp.arange(half, dtype=F32) / half)
    ang = pos[:, :, None] * freqs[None, None, :]
    cos = jnp.cos(ang)
    sin = jnp.sin(ang)
    cos_h = jnp.concatenate([cos, cos], axis=2).reshape(seq, HEAD_DIM)
    sin_h = jnp.concatenate([-sin, sin], axis=2).reshape(seq, HEAD_DIM)
    return jnp.tile(cos_h, (1, N_KV_HEADS)), jnp.tile(sin_h, (1, N_KV_HEADS))


def _attn_kernel(q_ref, kt_ref, v_ref, o_ref, qs_scr, m_scr, acc_scr, sa_scr, sb_scr, smaxa_scr, smaxb_scr, *,
                 tq, n_kblk, n_tiles):
    lane = lax.broadcasted_iota(jnp.int32, (tq, 2 * HEAD_DIM), 1)

    def tile_rows(t):
        return pl.ds(pl.multiple_of(t * tq, tq), tq)

    def load_queries(t):
        for g in range(GROUP):
            pair = q_ref[0, tile_rows(t), (g // 2) * 2 * HEAD_DIM:(g // 2 + 1) * 2 * HEAD_DIM]
            keep = (lane >= HEAD_DIM) if g % 2 else (lane < HEAD_DIM)
            qs_scr[g] = jnp.where(keep, pair, jnp.zeros_like(pair))

    def reset_state():
        m_scr[...] = jnp.full_like(m_scr, -jnp.inf)
        acc_scr[...] = jnp.zeros_like(acc_scr)

    def write_output(t):
        for j in range(GROUP // 2):
            halves = []
            for g in (2 * j, 2 * j + 1):
                acc = acc_scr[g]
                halves.append(acc / pltpu.roll(acc, HEAD_DIM, 1))
            pair = jnp.where(lane < HEAD_DIM, halves[0], pltpu.roll(halves[1], HEAD_DIM, 1))
            o_ref[0, tile_rows(t), j * 2 * HEAD_DIM:(j + 1) * 2 * HEAD_DIM] = pair.astype(BF16)

    def scores(k, buf):
        s_buf, smax_buf = buf
        kt = kt_ref[0, 0, k]
        for g in range(GROUP):
            s = _dot(qs_scr[g], kt)
            s_buf[g] = s
            smax_buf[g] = jnp.broadcast_to(jnp.max(s, axis=1, keepdims=True), (tq, 2 * HEAD_DIM))

    def accumulate(k, buf):
        s_buf, smax_buf = buf
        v = v_ref[0, 0, k]
        for g in range(GROUP):
            m_prev = m_scr[g]
            m_next = jnp.maximum(m_prev, smax_buf[g])
            alpha = jnp.exp2(m_prev - m_next)
            p = jnp.exp2(s_buf[g] - m_next[:, 0:1])
            acc_scr[g] = alpha * acc_scr[g] + _dot(p.astype(BF16), v)
            m_scr[g] = m_next

    def query_tile(t, first, second):
        def key_block_pair(j, carry):
            k = 2 * j
            scores(k + 1, second)
            accumulate(k, first)
            scores(k + 2, first)
            accumulate(k + 1, second)
            return carry

        lax.fori_loop(0, (n_kblk - 1) // 2, key_block_pair, 0)
        load_queries(jnp.minimum(t + 1, n_tiles - 1))
        scores(0, second)
        accumulate(n_kblk - 1, first)
        write_output(t)
        reset_state()

    buf_a, buf_b = (sa_scr, smaxa_scr), (sb_scr, smaxb_scr)
    load_queries(0)
    reset_state()
    scores(0, buf_a)

    def query_tile_pair(i, carry):
        query_tile(2 * i, buf_a, buf_b)
        query_tile(2 * i + 1, buf_b, buf_a)
        return carry

    lax.fori_loop(0, n_tiles // 2, query_tile_pair, 0)


def _attention(q, kt2, v_ext):
    b, seq, _ = q.shape
    n_kblk, tk = kt2.shape[2], kt2.shape[4]
    tq, n_tiles = ATT_TQ, ATT_TILES
    assert n_kblk % 2 == 1 and n_tiles % 2 == 0 and seq % (tq * n_tiles) == 0
    return pl.pallas_call(
        functools.partial(_attn_kernel, tq=tq, n_kblk=n_kblk, n_tiles=n_tiles),
        grid=(b, N_KV_HEADS, seq // (tq * n_tiles)),
        in_specs=[
            pl.BlockSpec((1, tq * n_tiles, KV_SLAB), lambda i, h, qi: (i, qi, h)),
            pl.BlockSpec((1, 1, n_kblk, 2 * HEAD_DIM, tk), lambda i, h, qi: (i, h, 0, 0, 0)),
            pl.BlockSpec((1, 1, n_kblk, tk, 2 * HEAD_DIM), lambda i, h, qi: (i, h, 0, 0, 0)),
        ],
        out_specs=pl.BlockSpec((1, tq * n_tiles, KV_SLAB), lambda i, h, qi: (i, qi, h)),
        out_shape=jax.ShapeDtypeStruct(q.shape, BF16),
        scratch_shapes=[
            pltpu.VMEM((GROUP, tq, 2 * HEAD_DIM), BF16),
            pltpu.VMEM((GROUP, tq, 2 * HEAD_DIM), F32),
            pltpu.VMEM((GROUP, tq, 2 * HEAD_DIM), F32),
            pltpu.VMEM((GROUP, tq, tk), F32),
            pltpu.VMEM((GROUP, tq, tk), F32),
            pltpu.VMEM((GROUP, tq, 2 * HEAD_DIM), F32),
            pltpu.VMEM((GROUP, tq, 2 * HEAD_DIM), F32),
        ],
        compiler_params=_params(("parallel", "parallel", "parallel"), 60),
        name="gqa_attention",
    )(q, kt2, v_ext)


def _oproj_router_kernel(a_ref, x_ref, mod_ref, ng_ref, wo_ref, rw_ref, x_out, h_out, e_out, g_out):
    g1 = mod_ref[0, 2:3, :]
    sh, sc = mod_ref[0, 3:4, :], mod_ref[0, 4:5, :]
    x = x_ref[...] + g1 * _dot(a_ref[...], wo_ref[...])
    x_out[...] = x
    h = _rms(x, ng_ref[...]) * (1.0 + sc) + sh
    _store_rows(h_out, h)
    logits = lax.dot_general(rw_ref[...], h, (((1,), (1,)), ((), ())),
                             preferred_element_type=F32, precision=lax.Precision.HIGHEST)
    eid = lax.broadcasted_iota(jnp.int32, logits.shape, 0)
    m1 = jnp.max(logits, axis=0, keepdims=True)
    i1 = jnp.min(jnp.where(logits == m1, eid, N_EXPERTS), axis=0, keepdims=True)
    rest = jnp.where(eid == i1, -jnp.inf, logits)
    m2 = jnp.max(rest, axis=0, keepdims=True)
    i2 = jnp.min(jnp.where(rest == m2, eid, N_EXPERTS), axis=0, keepdims=True)
    e = jnp.exp(m2 - m1)
    e_out[...] = jnp.concatenate([i1, i2], axis=0)
    g_out[...] = jnp.concatenate([1.0 / (1.0 + e), e / (1.0 + e)], axis=0)


def _oproj_router_layer(attn, x, mods, norm_g, w_o, router_w):
    b, seq, d = x.shape
    t = b * seq
    tm = ROW_TILE
    per_batch = seq // tm
    row = lambda i: (i, 0)
    fixed = lambda i: (0, 0)
    return pl.pallas_call(
        _oproj_router_kernel,
        grid=(t // tm,),
        in_specs=[
            pl.BlockSpec((tm, d), row),
            pl.BlockSpec((tm, d), row),
            pl.BlockSpec((1, N_MOD, d), lambda i: (i // per_batch, 0, 0)),
            pl.BlockSpec((1, d), fixed),
            pl.BlockSpec((d, d), fixed),
            pl.BlockSpec((N_EXPERTS, d), fixed),
        ],
        out_specs=[
            pl.BlockSpec((tm, d), row),
            pl.BlockSpec((tm * ROW_SLABS, LANES), row),
            pl.BlockSpec((TOP_K, tm), lambda i: (0, i)),
            pl.BlockSpec((TOP_K, tm), lambda i: (0, i)),
        ],
        out_shape=[
            jax.ShapeDtypeStruct((t, d), F32),
            jax.ShapeDtypeStruct((t * ROW_SLABS, LANES), F32),
            jax.ShapeDtypeStruct((TOP_K, t), jnp.int32),
            jax.ShapeDtypeStruct((TOP_K, t), F32),
        ],
        compiler_params=_params(("parallel",), 48),
        name="oproj_router",
    )(attn.reshape(t, d), x.reshape(t, d), mods, norm_g.reshape(1, d), w_o, router_w.T)


def _row_copy(src_ref, dst_ref, src_row, dst_row, sem):
    return pltpu.make_async_copy(_tile_of_row(src_ref, src_row), _tile_of_row(dst_ref, dst_row), sem)


def _rows_copy(src_ref, dst_ref, rows, sem):
    return pltpu.make_async_copy(src_ref.at[pl.ds(0, rows * ROW_SLABS)], dst_ref.at[pl.ds(0, rows * ROW_SLABS)], sem)


def _dest_spec(tm):
    return pl.BlockSpec((1, 1, TOP_K * tm), lambda i: (i, 0, 0), memory_space=pltpu.SMEM)


def _dispatch_kernel(dest_ref, h_ref, init_ref, xs_ref, sem):
    del init_ref
    tm = h_ref.shape[0] // ROW_SLABS

    def issue(r, carry):
        for k in range(TOP_K):
            _row_copy(h_ref, xs_ref, r, dest_ref[0, 0, TOP_K * r + k], sem).start()
        return carry

    lax.fori_loop(0, tm, issue, 0)
    for _ in range(TOP_K):
        _rows_copy(h_ref, xs_ref, tm, sem).wait()


def _dispatch(h, dest, cap):
    t = h.shape[0] // ROW_SLABS
    tm = DISPATCH_TILE
    any_spec = pl.BlockSpec(memory_space=pl.ANY)
    return pl.pallas_call(
        _dispatch_kernel,
        grid=(t // tm,),
        in_specs=[_dest_spec(tm), pl.BlockSpec((tm * ROW_SLABS, LANES), lambda i: (i, 0)), any_spec],
        out_specs=any_spec,
        out_shape=jax.ShapeDtypeStruct((cap * ROW_SLABS, LANES), h.dtype),
        scratch_shapes=[pltpu.SemaphoreType.DMA(())],
        input_output_aliases={2: 0},
        compiler_params=_params(("arbitrary",), 32),
        name="moe_dispatch",
    )(dest.reshape(t // tm, 1, TOP_K * tm), h, jnp.zeros((cap * ROW_SLABS, LANES), h.dtype))


def _ffn_expert_kernel(blk_e_ref, x_ref, w1_ref, w3_ref, w2_ref, o_ref):
    del blk_e_ref
    h = _load_rows(x_ref, MOE_ROWS).astype(BF16)
    _store_rows(o_ref, _swiglu(h, w1_ref.at[0], w3_ref.at[0], w2_ref.at[0]))


def _ffn_experts(xs, blk_e, w1, w3, w2):
    cap = xs.shape[0] // ROW_SLABS
    d = D_MODEL
    n_blk = cap // MOE_ROWS
    row_spec = pl.BlockSpec((MOE_ROWS * ROW_SLABS, LANES), lambda i, e: (i, 0))
    grid_spec = pltpu.PrefetchScalarGridSpec(
        num_scalar_prefetch=1,
        grid=(n_blk,),
        in_specs=[
            row_spec,
            pl.BlockSpec((1, d, D_FF), lambda i, e: (e[i], 0, 0)),
            pl.BlockSpec((1, d, D_FF), lambda i, e: (e[i], 0, 0)),
            pl.BlockSpec((1, D_FF, d), lambda i, e: (e[i], 0, 0)),
        ],
        out_specs=row_spec,
    )
    return pl.pallas_call(
        _ffn_expert_kernel,
        grid_spec=grid_spec,
        out_shape=jax.ShapeDtypeStruct(xs.shape, F32),
        compiler_params=_params(("arbitrary",), 60),
        name="ffn_experts",
    )(blk_e, xs, w1, w3, w2)


def _routing_plan(e_idx):
    flat_e = e_idx.T.reshape(-1)
    n_assign = flat_e.shape[0]
    onehot = (flat_e[:, None] == jnp.arange(N_EXPERTS, dtype=jnp.int32)[None, :]).astype(jnp.int32)
    csum = jnp.cumsum(onehot, axis=0)
    sizes = csum[-1]
    padded = (sizes + MOE_ROWS - 1) // MOE_ROWS * MOE_ROWS
    ends = jnp.cumsum(padded)
    pstart = ends - padded
    dest = jnp.sum(onehot * (csum - 1 + pstart[None, :]), axis=1).astype(jnp.int32)
    cap = n_assign + N_EXPERTS * MOE_ROWS
    blk_start = jnp.arange(cap // MOE_ROWS, dtype=jnp.int32) * MOE_ROWS
    blk_e = jnp.sum((ends[None, :] <= blk_start[:, None]).astype(jnp.int32), axis=1)
    blk_e = jnp.minimum(blk_e, N_EXPERTS - 1)
    return dest, blk_e, cap


def _final_kernel(dest_ref, dest_next_ref, x_ref, gt_ref, mod_ref, fg_ref, y_ref, o_ref, y_buf, sems):
    tm = x_ref.shape[0]
    i = pl.program_id(0)
    slot = lax.rem(i, 2)
    half = TOP_K * tm

    def fetch(idx_ref, into):
        def issue(r, carry):
            for k in range(TOP_K):
                _row_copy(y_ref, y_buf, idx_ref[0, 0, TOP_K * r + k], into * half + k * tm + r, sems.at[into]).start()
            return carry

        lax.fori_loop(0, tm, issue, 0, unroll=8)

    @pl.when(i == 0)
    def _():
        fetch(dest_ref, slot)

    @pl.when(i + 1 < pl.num_programs(0))
    def _():
        fetch(dest_next_ref, 1 - slot)

    _rows_copy(y_ref, y_buf, half, sems.at[slot]).wait()
    gt = gt_ref[...]
    moe = (gt[:, 0:1] * _load_rows(y_buf, tm, slot * half) + gt[:, 1:2] * _load_rows(y_buf, tm, slot * half + tm))
    x = x_ref[...] + mod_ref[0, 5:6, :] * moe
    o_ref[...] = _rms(x, fg_ref[...])


def _final_layer(x, ys, dest, gates_t, mods, final_g, seq):
    t, d = x.shape
    tm = ROW_TILE
    per_batch = seq // tm
    n_steps = t // tm
    dest_tiles = dest.reshape(n_steps, 1, TOP_K * tm)
    return pl.pallas_call(
        _final_kernel,
        grid=(n_steps,),
        in_specs=[
            _dest_spec(tm),
            pl.BlockSpec((1, 1, TOP_K * tm), lambda i: (jnp.minimum(i + 1, n_steps - 1), 0, 0),
                         memory_space=pltpu.SMEM),
            pl.BlockSpec((tm, d), lambda i: (i, 0)),
            pl.BlockSpec((tm, TOP_K), lambda i: (i, 0)),
            pl.BlockSpec((1, N_MOD, d), lambda i: (i // per_batch, 0, 0)),
            pl.BlockSpec((1, d), lambda i: (0, 0)),
            pl.BlockSpec(memory_space=pl.ANY),
        ],
        out_specs=pl.BlockSpec((tm, d), lambda i: (i, 0)),
        out_shape=jax.ShapeDtypeStruct((t, d), F32),
        scratch_shapes=[pltpu.VMEM((2 * TOP_K * tm * ROW_SLABS, LANES), F32), pltpu.SemaphoreType.DMA((2,))],
        compiler_params=_params(("arbitrary",), 48),
        name="moe_sum_final_norm",
    )(dest_tiles, dest_tiles, x, gates_t, mods, final_g.reshape(1, d), ys)


def kernel(x, c, ctx, c_ctx, ada_w, ada_b, norm_g, pool_w, pool_scale, ffn_w1, ffn_w3, ffn_w2, w_qkv, w_o,
           q_norm_g, k_norm_g, router_w, moe_w1, moe_w3, moe_w2, final_g):
    b, seq, d = x.shape
    assert ada_w.shape[0] == 2 and b <= 7, "written for the two-layer block (pool+dense, attention+experts)"

    cvecs = jnp.zeros((8, d), F32).at[:b].set(c).at[b].set(c_ctx)
    mods = _adaln(cvecs, ada_w, ada_b).reshape(2, 8, N_MOD, d)
    mods_x = mods[:, :b]
    mods_c = jnp.broadcast_to(mods[:, b:b + 1], (2, b, N_MOD, d))

    pool_wb = pool_w[0].astype(BF16)
    w1b, w3b, w2b = ffn_w1[0].astype(BF16), ffn_w3[0].astype(BF16), ffn_w2[0].astype(BF16)
    x = _pool_layer(x, mods_x[0], norm_g[0, 0], pool_wb, pool_scale[0])
    ctx = _pool_layer(ctx, mods_c[0], norm_g[0, 0], pool_wb, pool_scale[0])
    x = _ffn_dense_layer(x, mods_x[0], norm_g[0, 1], w1b, w3b, w2b)
    ctx = _ffn_dense_layer(ctx, mods_c[0], norm_g[0, 1], w1b, w3b, w2b)

    wqkv = w_qkv[0].astype(BF16)
    q, kx, vx = _qkv_layer(x, mods_x[1], norm_g[1, 0], wqkv, q_norm_g[0], k_norm_g[0], _rope_tables(seq))
    _, kc, vc = _qkv_layer(ctx, mods_c[1], norm_g[1, 0], wqkv[:, Q_DIM:], q_norm_g[0], k_norm_g[0], None)
    k_all = jnp.concatenate([kx, kc], axis=1)
    v_all = jnp.concatenate([vx, vc], axis=1)
    lk = k_all.shape[1]
    n_kblk = lk // ATT_TK
    kt = k_all.reshape(b, n_kblk, ATT_TK, N_KV_HEADS, HEAD_DIM).transpose(0, 3, 1, 4, 2)
    kt2 = jnp.concatenate([kt, kt], axis=3)
    vh = v_all.reshape(b, n_kblk, ATT_TK, N_KV_HEADS, HEAD_DIM).transpose(0, 3, 1, 2, 4)
    v_ext = jnp.concatenate([vh, jnp.ones_like(vh)], axis=4)
    attn = _attention(q, kt2, v_ext)

    x3, hf, e_idx, gates = _oproj_router_layer(attn, x, mods_x[1], norm_g[1, 1], w_o[0].astype(BF16), router_w[0])
    dest, blk_e, cap = _routing_plan(e_idx)
    xs = _dispatch(hf, dest, cap)
    ys = _ffn_experts(xs, blk_e, moe_w1[0].astype(BF16), moe_w3[0].astype(BF16), moe_w2[0].astype(BF16))
    out = _final_layer(x3, ys, dest, gates.T, mods_x[1], final_g, seq)
    return out.reshape(b, seq, d)
```

```python
import functools
import math

import jax
import jax.numpy as jnp
from jax import lax
from jax.experimental import pallas as pl
from jax.experimental.pallas import tpu as pltpu

F32 = jnp.float32
BF16 = jnp.bfloat16

D_MODEL = 1024
GRID_W = 64
POOL_WINDOWS = (2, 4, 8, 16)
POOL_GROUP_DIM = D_MODEL // len(POOL_WINDOWS)
HEAD_DIM = 64
N_KV_HEADS = 4
GROUP = 4
KV_SLAB = N_KV_HEADS * HEAD_DIM
Q_DIM = N_KV_HEADS * GROUP * HEAD_DIM
ROPE_THETA = 10000.0
Q_SCALE = HEAD_DIM ** -0.5 * math.log2(math.e)
D_FF = 2816
N_EXPERTS = 8
TOP_K = 2
NORM_EPS = 1e-6
N_MOD = 6

POOL_HALO = 16
ROW_TILE = 512
FF_TILE = 1408
MOE_ROWS = 512
DISPATCH_TILE = 2048
ATT_TQ = 512
ATT_TK = 1280
ATT_TILES = 8
MIB = 1024 * 1024
LANES = 128
ROW_SLABS = D_MODEL // LANES


def _dot(a, b):
    return jnp.dot(a, b, preferred_element_type=F32)


def _rms(x, g):
    ms = jnp.mean(x * x, axis=-1, keepdims=True)
    return x * lax.rsqrt(ms + NORM_EPS) * g


def _silu(x):
    return x * (1.0 / (1.0 + jnp.exp(-x)))


def _load_rows(ref, n_rows, first=0):
    return jnp.concatenate(
        [ref[pl.ds(ROW_SLABS * first + s, n_rows, stride=ROW_SLABS), :] for s in range(ROW_SLABS)], axis=1)


def _store_rows(ref, val):
    for s in range(ROW_SLABS):
        ref[pl.ds(s, val.shape[0], stride=ROW_SLABS), :] = val[:, s * LANES:(s + 1) * LANES]


def _tile_of_row(ref, row):
    return ref.at[pl.ds(pl.multiple_of(row * ROW_SLABS, ROW_SLABS), ROW_SLABS)]


def _params(semantics, vmem_mib):
    return pltpu.CompilerParams(dimension_semantics=semantics, vmem_limit_bytes=vmem_mib * MIB)


def _adaln_kernel(c_ref, w_ref, b_ref, o_ref):
    s = _silu(c_ref[...])
    o_ref[0] = jnp.dot(s, w_ref[0], preferred_element_type=F32, precision=lax.Precision.HIGHEST) + b_ref[0]


def _adaln(cvecs, ada_w, ada_b):
    depth, d, n = ada_w.shape
    tn = n // 4
    return pl.pallas_call(
        _adaln_kernel,
        grid=(depth, n // tn),
        in_specs=[
            pl.BlockSpec((8, d), lambda i, j: (0, 0)),
            pl.BlockSpec((1, d, tn), lambda i, j: (i, 0, j)),
            pl.BlockSpec((1, 1, tn), lambda i, j: (i, 0, j)),
        ],
        out_specs=pl.BlockSpec((1, 8, tn), lambda i, j: (i, 0, j)),
        out_shape=jax.ShapeDtypeStruct((depth, 8, n), F32),
        compiler_params=_params(("parallel", "parallel"), 32),
        name="adaln",
    )(cvecs, ada_w, ada_b.reshape(depth, 1, n))


def _pool_kernel(xp_ref, xc_ref, xn_ref, mod_ref, ng_ref, pw_ref, ps_ref, o_ref, *, seq, tm):
    t = pl.program_id(1)
    xc = xc_ref[0]
    xa = jnp.concatenate([xp_ref[0], xc, xn_ref[0]], axis=0)
    sh, sc, gate = mod_ref[0, 0:1, :], mod_ref[0, 1:2, :], mod_ref[0, 2:3, :]
    h = _rms(xa, ng_ref[...]) * (1.0 + sc) + sh
    hb = h.astype(BF16)
    wide = tm + 2 * POOL_HALO
    row = lax.broadcasted_iota(jnp.int32, (tm, wide), 0) + t * tm
    col = lax.broadcasted_iota(jnp.int32, (tm, wide), 1) + (t * tm - POOL_HALO)
    row1 = lax.broadcasted_iota(jnp.int32, (tm, 1), 0) + t * tm
    ys = []
    for g, w in enumerate(POOL_WINDOWS):
        half = w // 2
        lo = jnp.maximum(row - half, 0)
        hi = jnp.minimum(row + half, seq)
        band = jnp.where((col >= lo) & (col < hi), 1.0, 0.0).astype(BF16)
        cnt = (jnp.minimum(row1 + half, seq) - jnp.maximum(row1 - half, 0)).astype(F32)
        lanes = slice(g * POOL_GROUP_DIM, (g + 1) * POOL_GROUP_DIM)
        win_sum = _dot(band, hb[:, lanes])
        d = win_sum / cnt - h[POOL_HALO:POOL_HALO + tm, lanes]
        ys.append(_dot(d.astype(BF16), pw_ref[g]))
    y = jnp.concatenate(ys, axis=1) * ps_ref[...]
    o_ref[0] = xc + gate * y


def _pool_layer(x, mods, norm_g, pool_w, pool_scale):
    b, seq, d = x.shape
    tm = min(ROW_TILE, seq)
    per = tm // POOL_HALO
    last = seq // POOL_HALO - 1
    return pl.pallas_call(
        functools.partial(_pool_kernel, seq=seq, tm=tm),
        grid=(b, seq // tm),
        in_specs=[
            pl.BlockSpec((1, POOL_HALO, d), lambda i, t: (i, jnp.maximum(t * per - 1, 0), 0)),
            pl.BlockSpec((1, tm, d), lambda i, t: (i, t, 0)),
            pl.BlockSpec((1, POOL_HALO, d), lambda i, t: (i, jnp.minimum((t + 1) * per, last), 0)),
            pl.BlockSpec((1, N_MOD, d), lambda i, t: (i, 0, 0)),
            pl.BlockSpec((1, d), lambda i, t: (0, 0)),
            pl.BlockSpec(pool_w.shape, lambda i, t: (0, 0, 0)),
            pl.BlockSpec((1, d), lambda i, t: (0, 0)),
        ],
        out_specs=pl.BlockSpec((1, tm, d), lambda i, t: (i, t, 0)),
        out_shape=jax.ShapeDtypeStruct(x.shape, F32),
        compiler_params=_params(("parallel", "parallel"), 48),
        name="pool_mix",
    )(x, x, x, mods, norm_g.reshape(1, d), pool_w, pool_scale.reshape(1, d))


def _swiglu(h, w1_ref, w3_ref, w2_ref):
    acc = None
    for f in range(D_FF // FF_TILE):
        cols = slice(f * FF_TILE, (f + 1) * FF_TILE)
        u = _silu(_dot(h, w1_ref[:, cols])) * _dot(h, w3_ref[:, cols])
        part = _dot(u.astype(BF16), w2_ref[cols, :])
        acc = part if acc is None else acc + part
    return acc


def _ffn_dense_kernel(x_ref, mod_ref, ng_ref, w1_ref, w3_ref, w2_ref, o_ref):
    sh, sc, gate = mod_ref[0, 3:4, :], mod_ref[0, 4:5, :], mod_ref[0, 5:6, :]
    x = x_ref[...]
    h = (_rms(x, ng_ref[...]) * (1.0 + sc) + sh).astype(BF16)
    o_ref[...] = x + gate * _swiglu(h, w1_ref, w3_ref, w2_ref)


def _ffn_dense_layer(x, mods, norm_g, w1, w3, w2):
    b, seq, d = x.shape
    t = b * seq
    tm = min(ROW_TILE, seq)
    per_batch = seq // tm
    fixed = lambda i: (0, 0)
    out = pl.pallas_call(
        _ffn_dense_kernel,
        grid=(t // tm,),
        in_specs=[
            pl.BlockSpec((tm, d), lambda i: (i, 0)),
            pl.BlockSpec((1, N_MOD, d), lambda i: (i // per_batch, 0, 0)),
            pl.BlockSpec((1, d), fixed),
            pl.BlockSpec((d, D_FF), fixed),
            pl.BlockSpec((d, D_FF), fixed),
            pl.BlockSpec((D_FF, d), fixed),
        ],
        out_specs=pl.BlockSpec((tm, d), lambda i: (i, 0)),
        out_shape=jax.ShapeDtypeStruct((t, d), F32),
        compiler_params=_params(("parallel",), 60),
        name="ffn_dense",
    )(x.reshape(t, d), mods, norm_g.reshape(1, d), w1, w3, w2)
    return out.reshape(b, seq, d)


def _head_norm_rope(z, ones_blk, gain, cos, sin):
    sq = z * z
    hi = sq.astype(BF16)
    lo = (sq - hi.astype(F32)).astype(BF16)
    ss = _dot(hi, ones_blk) + _dot(lo, ones_blk)
    zn = z * lax.rsqrt(ss * (1.0 / HEAD_DIM) + NORM_EPS) * gain
    if cos is None:
        return zn
    width = z.shape[1]
    fwd = pltpu.roll(zn, width - 16, 1)
    bwd = pltpu.roll(zn, 16, 1)
    lane = lax.broadcasted_iota(jnp.int32, zn.shape, 1)
    partner = jnp.where((lane & 31) < 16, fwd, bwd)
    return zn * cos + partner * sin


def _qkv_kernel(*refs, n_q, rope):
    x_ref, mod_ref, ng_ref, w_ref, e_ref, qg_ref, kg_ref = refs[:7]
    refs = refs[7:]
    if rope:
        cos_ref, sin_ref = refs[:2]
        refs = refs[2:]
        cos, sin = cos_ref[...], sin_ref[...]
    else:
        cos = sin = None
    sh, sc = mod_ref[0, 0:1, :], mod_ref[0, 1:2, :]
    hb = (_rms(x_ref[0], ng_ref[...]) * (1.0 + sc) + sh).astype(BF16)
    qkv = _dot(hb, w_ref[...])
    ones_blk = e_ref[...]
    if n_q:
        q_ref, k_ref, v_ref = refs
        for c in range(n_q):
            z = _head_norm_rope(qkv[:, c * KV_SLAB:(c + 1) * KV_SLAB], ones_blk, qg_ref[...], cos, sin)
            q_ref[0, :, c * KV_SLAB:(c + 1) * KV_SLAB] = (z * Q_SCALE).astype(BF16)
    else:
        k_ref, v_ref = refs
    base = n_q * KV_SLAB
    k_ref[0] = _head_norm_rope(qkv[:, base:base + KV_SLAB], ones_blk, kg_ref[...], cos, sin).astype(BF16)
    v_ref[0] = qkv[:, base + KV_SLAB:base + 2 * KV_SLAB].astype(BF16)


def _qkv_layer(x, mods, norm_g, w, q_gain, k_gain, tables):
    b, seq, d = x.shape
    tm = min(ROW_TILE, seq)
    width = w.shape[1]
    n_q = (width - 2 * KV_SLAB) // KV_SLAB
    rope = tables is not None
    seg = lax.broadcasted_iota(jnp.int32, (KV_SLAB, KV_SLAB), 0) // HEAD_DIM
    ones_blk = (seg == seg.T).astype(BF16)
    in_specs = [
        pl.BlockSpec((1, tm, d), lambda i, t: (i, t, 0)),
        pl.BlockSpec((1, N_MOD, d), lambda i, t: (i, 0, 0)),
        pl.BlockSpec((1, d), lambda i, t: (0, 0)),
        pl.BlockSpec((d, width), lambda i, t: (0, 0)),
        pl.BlockSpec((KV_SLAB, KV_SLAB), lambda i, t: (0, 0)),
        pl.BlockSpec((1, KV_SLAB), lambda i, t: (0, 0)),
        pl.BlockSpec((1, KV_SLAB), lambda i, t: (0, 0)),
    ]
    args = [x, mods, norm_g.reshape(1, d), w, ones_blk,
            jnp.tile(q_gain, N_KV_HEADS).reshape(1, KV_SLAB), jnp.tile(k_gain, N_KV_HEADS).reshape(1, KV_SLAB)]
    if rope:
        in_specs += [pl.BlockSpec((tm, KV_SLAB), lambda i, t: (t, 0))] * 2
        args += list(tables)
    kv_spec = pl.BlockSpec((1, tm, KV_SLAB), lambda i, t: (i, t, 0))
    kv_shape = jax.ShapeDtypeStruct((b, seq, KV_SLAB), BF16)
    out_specs, out_shape = [kv_spec, kv_spec], [kv_shape, kv_shape]
    if n_q:
        out_specs = [pl.BlockSpec((1, tm, n_q * KV_SLAB), lambda i, t: (i, t, 0))] + out_specs
        out_shape = [jax.ShapeDtypeStruct((b, seq, n_q * KV_SLAB), BF16)] + out_shape
    outs = pl.pallas_call(
        functools.partial(_qkv_kernel, n_q=n_q, rope=rope),
        grid=(b, seq // tm),
        in_specs=in_specs,
        out_specs=out_specs,
        out_shape=out_shape,
        compiler_params=_params(("parallel", "parallel"), 48),
        name="qkv_rope" if rope else "kv_ctx",
    )(*args)
    return outs if n_q else [None] + list(outs)


def _rope_tables(seq):
    half = HEAD_DIM // 4
    t = jnp.arange(seq, dtype=jnp.int32)
    pos = jnp.stack([t // GRID_W, t % GRID_W], axis=1).astype(F32)
    freqs = ROPE_THETA ** (-jnp.arange(half, dtype=F32) / half)
    ang = pos[:, :, None] * freqs[None, None, :]
    cos = jnp.cos(ang)
    sin = jnp.sin(ang)
    cos_h = jnp.concatenate([cos, cos], axis=2).reshape(seq, HEAD_DIM)
    sin_h = jnp.concatenate([-sin, sin], axis=2).reshape(seq, HEAD_DIM)
    return jnp.tile(cos_h, (1, N_KV_HEADS)), jnp.tile(sin_h, (1, N_KV_HEADS))


def _attn_kernel(q_ref, k_ref, vt_ref, o_ref, qt_scr, m_scr, acc_scr, sa_scr, sb_scr, smaxa_scr, smaxb_scr, *,
                 tq, tk, n_kblk, n_tiles):
    odd_head = lax.rem(pl.program_id(1), 2) == 1

    def load_queries(t):
        rows = pl.ds(pl.multiple_of(t * tq, tq), tq)
        for j in range(GROUP // 2):
            pair_t = q_ref[0, rows, j * 2 * HEAD_DIM:(j + 1) * 2 * HEAD_DIM].astype(F32).T
            for h in range(2):
                qt = pair_t[h * HEAD_DIM:(h + 1) * HEAD_DIM, :]
                zero = jnp.zeros_like(qt)
                low, high = jnp.concatenate([qt, zero], axis=0), jnp.concatenate([zero, qt], axis=0)
                qt_scr[2 * j + h] = jnp.where(odd_head, high, low).astype(BF16)

    def reset_state():
        m_scr[...] = jnp.full_like(m_scr, -jnp.inf)
        acc_scr[...] = jnp.zeros_like(acc_scr)

    def write_output(t):
        for g in range(GROUP):
            acc = acc_scr[g]
            o_ref[0, t, g * HEAD_DIM:(g + 1) * HEAD_DIM, :] = (acc[:HEAD_DIM] / acc[HEAD_DIM:]).astype(BF16)

    def scores_head(g, k, buf):
        s_buf, smax_buf = buf
        keys = k_ref[0, pl.ds(pl.multiple_of(k * tk, tk), tk), :]
        s = _dot(keys, qt_scr[g])
        s_buf[g] = s
        smax_buf[g] = jnp.broadcast_to(jnp.max(s, axis=0, keepdims=True), (8, tq))

    def accumulate_head(g, k, buf):
        s_buf, smax_buf = buf
        m_prev = m_scr[g]
        m_next = jnp.maximum(m_prev, smax_buf[g])
        alpha = jnp.exp2(m_prev - m_next)
        p = jnp.exp2(s_buf[g] - m_next[0:1, :])
        acc_scr[g] = alpha[0:1, :] * acc_scr[g] + _dot(vt_ref[0, 0, k], p.astype(BF16))
        m_scr[g] = m_next

    def scores(k, buf):
        for g in range(GROUP):
            scores_head(g, k, buf)

    def scores_and_accumulate(k_new, buf_new, k_old, buf_old):
        for g in range(GROUP):
            scores_head(g, k_new, buf_new)
            accumulate_head(g, k_old, buf_old)

    def query_tile(t, first, second):
        def key_block_pair(j, carry):
            k = 2 * j
            scores_and_accumulate(k + 1, second, k, first)
            scores_and_accumulate(k + 2, first, k + 1, second)
            return carry

        lax.fori_loop(0, (n_kblk - 1) // 2, key_block_pair, 0)
        load_queries(jnp.minimum(t + 1, n_tiles - 1))
        scores_and_accumulate(0, second, n_kblk - 1, first)
        write_output(t)
        reset_state()

    buf_a, buf_b = (sa_scr, smaxa_scr), (sb_scr, smaxb_scr)
    load_queries(0)
    reset_state()
    scores(0, buf_a)

    def query_tile_pair(i, carry):
        query_tile(2 * i, buf_a, buf_b)
        query_tile(2 * i + 1, buf_b, buf_a)
        return carry

    lax.fori_loop(0, n_tiles // 2, query_tile_pair, 0)


def _attention(q, k, vt_ext):
    b, seq, _ = q.shape
    n_kblk, tk = vt_ext.shape[2], vt_ext.shape[4]
    lk = k.shape[1]
    tq, n_tiles = ATT_TQ, ATT_TILES
    assert n_kblk % 2 == 1 and n_tiles % 2 == 0 and seq % (tq * n_tiles) == 0 and lk == n_kblk * tk
    stat_rows = 8
    return pl.pallas_call(
        functools.partial(_attn_kernel, tq=tq, tk=tk, n_kblk=n_kblk, n_tiles=n_tiles),
        grid=(b, N_KV_HEADS, seq // (tq * n_tiles)),
        in_specs=[
            pl.BlockSpec((1, tq * n_tiles, KV_SLAB), lambda i, h, qi: (i, qi, h)),
            pl.BlockSpec((1, lk, 2 * HEAD_DIM), lambda i, h, qi: (i, 0, h // 2)),
            pl.BlockSpec((1, 1, n_kblk, 2 * HEAD_DIM, tk), lambda i, h, qi: (i, h, 0, 0, 0)),
        ],
        out_specs=pl.BlockSpec((1, n_tiles, KV_SLAB, tq), lambda i, h, qi: (i, qi, h, 0)),
        out_shape=jax.ShapeDtypeStruct((b, seq // tq, Q_DIM, tq), BF16),
        scratch_shapes=[
            pltpu.VMEM((GROUP, 2 * HEAD_DIM, tq), BF16),
            pltpu.VMEM((GROUP, stat_rows, tq), F32),
            pltpu.VMEM((GROUP, 2 * HEAD_DIM, tq), F32),
            pltpu.VMEM((GROUP, tk, tq), F32),
            pltpu.VMEM((GROUP, tk, tq), F32),
            pltpu.VMEM((GROUP, stat_rows, tq), F32),
            pltpu.VMEM((GROUP, stat_rows, tq), F32),
        ],
        compiler_params=_params(("parallel", "parallel", "parallel"), 60),
        name="gqa_attention",
    )(q, k, vt_ext)


def _oproj_router_kernel(a_ref, x_ref, mod_ref, ng_ref, wo_ref, rw_ref, x_out, h_out, e_out, g_out):
    g1 = mod_ref[0, 2:3, :]
    sh, sc = mod_ref[0, 3:4, :], mod_ref[0, 4:5, :]
    x = x_ref[...] + g1 * _dot(a_ref[...], wo_ref[...])
    x_out[...] = x
    h = _rms(x, ng_ref[...]) * (1.0 + sc) + sh
    _store_rows(h_out, h)
    logits = lax.dot_general(rw_ref[...], h, (((1,), (1,)), ((), ())),
                             preferred_element_type=F32, precision=lax.Precision.HIGHEST)
    eid = lax.broadcasted_iota(jnp.int32, logits.shape, 0)
    m1 = jnp.max(logits, axis=0, keepdims=True)
    i1 = jnp.min(jnp.where(logits == m1, eid, N_EXPERTS), axis=0, keepdims=True)
    rest = jnp.where(eid == i1, -jnp.inf, logits)
    m2 = jnp.max(rest, axis=0, keepdims=True)
    i2 = jnp.min(jnp.where(rest == m2, eid, N_EXPERTS), axis=0, keepdims=True)
    e = jnp.exp(m2 - m1)
    e_out[...] = jnp.concatenate([i1, i2], axis=0)
    g_out[...] = jnp.concatenate([1.0 / (1.0 + e), e / (1.0 + e)], axis=0)


def _oproj_router_layer(attn, x, mods, norm_g, w_o, router_w):
    b, seq, d = x.shape
    t = b * seq
    tm = ROW_TILE
    per_batch = seq // tm
    row = lambda i: (i, 0)
    fixed = lambda i: (0, 0)
    return pl.pallas_call(
        _oproj_router_kernel,
        grid=(t // tm,),
        in_specs=[
            pl.BlockSpec((tm, d), row),
            pl.BlockSpec((tm, d), row),
            pl.BlockSpec((1, N_MOD, d), lambda i: (i // per_batch, 0, 0)),
            pl.BlockSpec((1, d), fixed),
            pl.BlockSpec((d, d), fixed),
            pl.BlockSpec((N_EXPERTS, d), fixed),
        ],
        out_specs=[
            pl.BlockSpec((tm, d), row),
            pl.BlockSpec((tm * ROW_SLABS, LANES), row),
            pl.BlockSpec((TOP_K, tm), lambda i: (0, i)),
            pl.BlockSpec((TOP_K, tm), lambda i: (0, i)),
        ],
        out_shape=[
            jax.ShapeDtypeStruct((t, d), F32),
            jax.ShapeDtypeStruct((t * ROW_SLABS, LANES), F32),
            jax.ShapeDtypeStruct((TOP_K, t), jnp.int32),
            jax.ShapeDtypeStruct((TOP_K, t), F32),
        ],
        compiler_params=_params(("parallel",), 48),
        name="oproj_router",
    )(attn.reshape(t, d), x.reshape(t, d), mods, norm_g.reshape(1, d), w_o, router_w.T)


def _row_copy(src_ref, dst_ref, src_row, dst_row, sem):
    return pltpu.make_async_copy(_tile_of_row(src_ref, src_row), _tile_of_row(dst_ref, dst_row), sem)


def _rows_copy(src_ref, dst_ref, rows, sem):
    return pltpu.make_async_copy(src_ref.at[pl.ds(0, rows * ROW_SLABS)], dst_ref.at[pl.ds(0, rows * ROW_SLABS)], sem)


def _dest_spec(tm):
    return pl.BlockSpec((1, 1, TOP_K * tm), lambda i: (i, 0, 0), memory_space=pltpu.SMEM)


def _dispatch_kernel(dest_ref, h_ref, init_ref, xs_ref, sem):
    del init_ref
    tm = h_ref.shape[0] // ROW_SLABS

    def issue(r, carry):
        for k in range(TOP_K):
            _row_copy(h_ref, xs_ref, r, dest_ref[0, 0, TOP_K * r + k], sem).start()
        return carry

    lax.fori_loop(0, tm, issue, 0)
    for _ in range(TOP_K):
        _rows_copy(h_ref, xs_ref, tm, sem).wait()


def _dispatch(h, dest, cap):
    t = h.shape[0] // ROW_SLABS
    tm = DISPATCH_TILE
    any_spec = pl.BlockSpec(memory_space=pl.ANY)
    return pl.pallas_call(
        _dispatch_kernel,
        grid=(t // tm,),
        in_specs=[_dest_spec(tm), pl.BlockSpec((tm * ROW_SLABS, LANES), lambda i: (i, 0)), any_spec],
        out_specs=any_spec,
        out_shape=jax.ShapeDtypeStruct((cap * ROW_SLABS, LANES), h.dtype),
        scratch_shapes=[pltpu.SemaphoreType.DMA(())],
        input_output_aliases={2: 0},
        compiler_params=_params(("arbitrary",), 32),
        name="moe_dispatch",
    )(dest.reshape(t // tm, 1, TOP_K * tm), h, jnp.zeros((cap * ROW_SLABS, LANES), h.dtype))


def _ffn_expert_kernel(blk_e_ref, x_ref, w1_ref, w3_ref, w2_ref, o_ref):
    del blk_e_ref
    h = _load_rows(x_ref, MOE_ROWS).astype(BF16)
    _store_rows(o_ref, _swiglu(h, w1_ref.at[0], w3_ref.at[0], w2_ref.at[0]))


def _ffn_experts(xs, blk_e, w1, w3, w2):
    cap = xs.shape[0] // ROW_SLABS
    d = D_MODEL
    n_blk = cap // MOE_ROWS
    row_spec = pl.BlockSpec((MOE_ROWS * ROW_SLABS, LANES), lambda i, e: (i, 0))
    grid_spec = pltpu.PrefetchScalarGridSpec(
        num_scalar_prefetch=1,
        grid=(n_blk,),
        in_specs=[
            row_spec,
            pl.BlockSpec((1, d, D_FF), lambda i, e: (e[i], 0, 0)),
            pl.BlockSpec((1, d, D_FF), lambda i, e: (e[i], 0, 0)),
            pl.BlockSpec((1, D_FF, d), lambda i, e: (e[i], 0, 0)),
        ],
        out_specs=row_spec,
    )
    return pl.pallas_call(
        _ffn_expert_kernel,
        grid_spec=grid_spec,
        out_shape=jax.ShapeDtypeStruct(xs.shape, F32),
        compiler_params=_params(("arbitrary",), 60),
        name="ffn_experts",
    )(blk_e, xs, w1, w3, w2)


def _routing_plan(e_idx):
    flat_e = e_idx.T.reshape(-1)
    n_assign = flat_e.shape[0]
    onehot = (flat_e[:, None] == jnp.arange(N_EXPERTS, dtype=jnp.int32)[None, :]).astype(jnp.int32)
    csum = jnp.cumsum(onehot, axis=0)
    sizes = csum[-1]
    padded = (sizes + MOE_ROWS - 1) // MOE_ROWS * MOE_ROWS
    ends = jnp.cumsum(padded)
    pstart = ends - padded
    dest = jnp.sum(onehot * (csum - 1 + pstart[None, :]), axis=1).astype(jnp.int32)
    cap = n_assign + N_EXPERTS * MOE_ROWS
    blk_start = jnp.arange(cap // MOE_ROWS, dtype=jnp.int32) * MOE_ROWS
    blk_e = jnp.sum((ends[None, :] <= blk_start[:, None]).astype(jnp.int32), axis=1)
    blk_e = jnp.minimum(blk_e, N_EXPERTS - 1)
    return dest, blk_e, cap


def _final_kernel(dest_ref, dest_next_ref, x_ref, gt_ref, mod_ref, fg_ref, y_ref, o_ref, y_buf, sems):
    tm = x_ref.shape[0]
    i = pl.program_id(0)
    slot = lax.rem(i, 2)
    half = TOP_K * tm

    def fetch(idx_ref, into):
        def issue(r, carry):
            for k in range(TOP_K):
                _row_copy(y_ref, y_buf, idx_ref[0, 0, TOP_K * r + k], into * half + k * tm + r, sems.at[into]).start()
            return carry

        lax.fori_loop(0, tm, issue, 0, unroll=8)

    @pl.when(i == 0)
    def _():
        fetch(dest_ref, slot)

    @pl.when(i + 1 < pl.num_programs(0))
    def _():
        fetch(dest_next_ref, 1 - slot)

    _rows_copy(y_ref, y_buf, half, sems.at[slot]).wait()
    gt = gt_ref[...]
    moe = (gt[:, 0:1] * _load_rows(y_buf, tm, slot * half) + gt[:, 1:2] * _load_rows(y_buf, tm, slot * half + tm))
    x = x_ref[...] + mod_ref[0, 5:6, :] * moe
    o_ref[...] = _rms(x, fg_ref[...])


def _final_layer(x, ys, dest, gates_t, mods, final_g, seq):
    t, d = x.shape
    tm = ROW_TILE
    per_batch = seq // tm
    n_steps = t // tm
    dest_tiles = dest.reshape(n_steps, 1, TOP_K * tm)
    return pl.pallas_call(
        _final_kernel,
        grid=(n_steps,),
        in_specs=[
            _dest_spec(tm),
            pl.BlockSpec((1, 1, TOP_K * tm), lambda i: (jnp.minimum(i + 1, n_steps - 1), 0, 0),
                         memory_space=pltpu.SMEM),
            pl.BlockSpec((tm, d), lambda i: (i, 0)),
            pl.BlockSpec((tm, TOP_K), lambda i: (i, 0)),
            pl.BlockSpec((1, N_MOD, d), lambda i: (i // per_batch, 0, 0)),
            pl.BlockSpec((1, d), lambda i: (0, 0)),
            pl.BlockSpec(memory_space=pl.ANY),
        ],
        out_specs=pl.BlockSpec((tm, d), lambda i: (i, 0)),
        out_shape=jax.ShapeDtypeStruct((t, d), F32),
        scratch_shapes=[pltpu.VMEM((2 * TOP_K * tm * ROW_SLABS, LANES), F32), pltpu.SemaphoreType.DMA((2,))],
        compiler_params=_params(("arbitrary",), 48),
        name="moe_sum_final_norm",
    )(dest_tiles, dest_tiles, x, gates_t, mods, final_g.reshape(1, d), ys)


def kernel(x, c, ctx, c_ctx, ada_w, ada_b, norm_g, pool_w, pool_scale, ffn_w1, ffn_w3, ffn_w2, w_qkv, w_o,
           q_norm_g, k_norm_g, router_w, moe_w1, moe_w3, moe_w2, final_g):
    b, seq, d = x.shape
    assert ada_w.shape[0] == 2 and b <= 7, "written for the two-layer block (pool+dense, attention+experts)"

    cvecs = jnp.zeros((8, d), F32).at[:b].set(c).at[b].set(c_ctx)
    mods = _adaln(cvecs, ada_w, ada_b).reshape(2, 8, N_MOD, d)
    mods_x = mods[:, :b]
    mods_c = jnp.broadcast_to(mods[:, b:b + 1], (2, b, N_MOD, d))

    pool_wb = pool_w[0].astype(BF16)
    w1b, w3b, w2b = ffn_w1[0].astype(BF16), ffn_w3[0].astype(BF16), ffn_w2[0].astype(BF16)
    x = _pool_layer(x, mods_x[0], norm_g[0, 0], pool_wb, pool_scale[0])
    ctx = _pool_layer(ctx, mods_c[0], norm_g[0, 0], pool_wb, pool_scale[0])
    x = _ffn_dense_layer(x, mods_x[0], norm_g[0, 1], w1b, w3b, w2b)
    ctx = _ffn_dense_layer(ctx, mods_c[0], norm_g[0, 1], w1b, w3b, w2b)

    wqkv = w_qkv[0].astype(BF16)
    q, kx, vx = _qkv_layer(x, mods_x[1], norm_g[1, 0], wqkv, q_norm_g[0], k_norm_g[0], _rope_tables(seq))
    _, kc, vc = _qkv_layer(ctx, mods_c[1], norm_g[1, 0], wqkv[:, Q_DIM:], q_norm_g[0], k_norm_g[0], None)
    k_all = jnp.concatenate([kx, kc], axis=1)
    v_all = jnp.concatenate([vx, vc], axis=1)
    lk = k_all.shape[1]
    n_kblk = lk // ATT_TK
    vt = v_all.reshape(b, n_kblk, ATT_TK, N_KV_HEADS, HEAD_DIM).transpose(0, 3, 1, 4, 2)
    vt_ext = jnp.concatenate([vt, jnp.ones_like(vt)], axis=3)
    attn_t = _attention(q, k_all, vt_ext)
    attn = attn_t.transpose(0, 1, 3, 2).reshape(b, seq, Q_DIM)

    x3, hf, e_idx, gates = _oproj_router_layer(attn, x, mods_x[1], norm_g[1, 1], w_o[0].astype(BF16), router_w[0])
    dest, blk_e, cap = _routing_plan(e_idx)
    xs = _dispatch(hf, dest, cap)
    ys = _ffn_experts(xs, blk_e, moe_w1[0].astype(BF16), moe_w3[0].astype(BF16), moe_w2[0].astype(BF16))
    out = _final_layer(x3, ys, dest, gates.T, mods_x[1], final_g, seq)
    return out.reshape(b, seq, d)
```

```python
import functools
import math

import jax
import jax.numpy as jnp
from jax import lax
from jax.experimental import pallas as pl
from jax.experimental.pallas import tpu as pltpu

F32 = jnp.float32
BF16 = jnp.bfloat16

D_MODEL = 1024
GRID_W = 64
POOL_WINDOWS = (2, 4, 8, 16)
POOL_GROUP_DIM = D_MODEL // len(POOL_WINDOWS)
HEAD_DIM = 64
N_KV_HEADS = 4
GROUP = 4
KV_SLAB = N_KV_HEADS * HEAD_DIM
Q_DIM = N_KV_HEADS * GROUP * HEAD_DIM
ROPE_THETA = 10000.0
Q_SCALE = HEAD_DIM ** -0.5 * math.log2(math.e)
D_FF = 2816
N_EXPERTS = 8
TOP_K = 2
NORM_EPS = 1e-6
N_MOD = 6

POOL_HALO = 16
ROW_TILE = 512
FF_TILE = 1408
MOE_ROWS = 512
DISPATCH_TILE = 2048
ATT_TQ = 512
ATT_TK = 1280
KEY_CHUNK = 256
ONES_ROWS = 16
ATT_TILES = 8
MIB = 1024 * 1024
LANES = 128
ROW_SLABS = D_MODEL // LANES


def _dot(a, b):
    return jnp.dot(a, b, preferred_element_type=F32)


def _rms(x, g):
    ms = jnp.mean(x * x, axis=-1, keepdims=True)
    return x * lax.rsqrt(ms + NORM_EPS) * g


def _silu(x):
    return x * (1.0 / (1.0 + jnp.exp(-x)))


def _load_rows(ref, n_rows, first=0):
    return jnp.concatenate(
        [ref[pl.ds(ROW_SLABS * first + s, n_rows, stride=ROW_SLABS), :] for s in range(ROW_SLABS)], axis=1)


def _store_rows(ref, val):
    for s in range(ROW_SLABS):
        ref[pl.ds(s, val.shape[0], stride=ROW_SLABS), :] = val[:, s * LANES:(s + 1) * LANES]


def _tile_of_row(ref, row):
    return ref.at[pl.ds(pl.multiple_of(row * ROW_SLABS, ROW_SLABS), ROW_SLABS)]


def _params(semantics, vmem_mib):
    return pltpu.CompilerParams(dimension_semantics=semantics, vmem_limit_bytes=vmem_mib * MIB)


def _adaln_kernel(c_ref, w_ref, b_ref, o_ref):
    s = _silu(c_ref[...])
    o_ref[0] = jnp.dot(s, w_ref[0], preferred_element_type=F32, precision=lax.Precision.HIGHEST) + b_ref[0]


def _adaln(cvecs, ada_w, ada_b):
    depth, d, n = ada_w.shape
    tn = n // 4
    return pl.pallas_call(
        _adaln_kernel,
        grid=(depth, n // tn),
        in_specs=[
            pl.BlockSpec((8, d), lambda i, j: (0, 0)),
            pl.BlockSpec((1, d, tn), lambda i, j: (i, 0, j)),
            pl.BlockSpec((1, 1, tn), lambda i, j: (i, 0, j)),
        ],
        out_specs=pl.BlockSpec((1, 8, tn), lambda i, j: (i, 0, j)),
        out_shape=jax.ShapeDtypeStruct((depth, 8, n), F32),
        compiler_params=_params(("parallel", "parallel"), 32),
        name="adaln",
    )(cvecs, ada_w, ada_b.reshape(depth, 1, n))


def _pool_kernel(xp_ref, xc_ref, xn_ref, mod_ref, ng_ref, pw_ref, ps_ref, o_ref, *, seq, tm):
    t = pl.program_id(1)
    xc = xc_ref[0]
    xa = jnp.concatenate([xp_ref[0], xc, xn_ref[0]], axis=0)
    sh, sc, gate = mod_ref[0, 0:1, :], mod_ref[0, 1:2, :], mod_ref[0, 2:3, :]
    h = _rms(xa, ng_ref[...]) * (1.0 + sc) + sh
    hb = h.astype(BF16)
    wide = tm + 2 * POOL_HALO
    row = lax.broadcasted_iota(jnp.int32, (tm, wide), 0) + t * tm
    col = lax.broadcasted_iota(jnp.int32, (tm, wide), 1) + (t * tm - POOL_HALO)
    row1 = lax.broadcasted_iota(jnp.int32, (tm, 1), 0) + t * tm
    ys = []
    for g, w in enumerate(POOL_WINDOWS):
        half = w // 2
        lo = jnp.maximum(row - half, 0)
        hi = jnp.minimum(row + half, seq)
        band = jnp.where((col >= lo) & (col < hi), 1.0, 0.0).astype(BF16)
        cnt = (jnp.minimum(row1 + half, seq) - jnp.maximum(row1 - half, 0)).astype(F32)
        lanes = slice(g * POOL_GROUP_DIM, (g + 1) * POOL_GROUP_DIM)
        win_sum = _dot(band, hb[:, lanes])
        d = win_sum / cnt - h[POOL_HALO:POOL_HALO + tm, lanes]
        ys.append(_dot(d.astype(BF16), pw_ref[g]))
    y = jnp.concatenate(ys, axis=1) * ps_ref[...]
    o_ref[0] = xc + gate * y


def _pool_layer(x, mods, norm_g, pool_w, pool_scale):
    b, seq, d = x.shape
    tm = min(ROW_TILE, seq)
    per = tm // POOL_HALO
    last = seq // POOL_HALO - 1
    return pl.pallas_call(
        functools.partial(_pool_kernel, seq=seq, tm=tm),
        grid=(b, seq // tm),
        in_specs=[
            pl.BlockSpec((1, POOL_HALO, d), lambda i, t: (i, jnp.maximum(t * per - 1, 0), 0)),
            pl.BlockSpec((1, tm, d), lambda i, t: (i, t, 0)),
            pl.BlockSpec((1, POOL_HALO, d), lambda i, t: (i, jnp.minimum((t + 1) * per, last), 0)),
            pl.BlockSpec((1, N_MOD, d), lambda i, t: (i, 0, 0)),
            pl.BlockSpec((1, d), lambda i, t: (0, 0)),
            pl.BlockSpec(pool_w.shape, lambda i, t: (0, 0, 0)),
            pl.BlockSpec((1, d), lambda i, t: (0, 0)),
        ],
        out_specs=pl.BlockSpec((1, tm, d), lambda i, t: (i, t, 0)),
        out_shape=jax.ShapeDtypeStruct(x.shape, F32),
        compiler_params=_params(("parallel", "parallel"), 48),
        name="pool_mix",
    )(x, x, x, mods, norm_g.reshape(1, d), pool_w, pool_scale.reshape(1, d))


def _swiglu(h, w1_ref, w3_ref, w2_ref):
    acc = None
    for f in range(D_FF // FF_TILE):
        cols = slice(f * FF_TILE, (f + 1) * FF_TILE)
        u = _silu(_dot(h, w1_ref[:, cols])) * _dot(h, w3_ref[:, cols])
        part = _dot(u.astype(BF16), w2_ref[cols, :])
        acc = part if acc is None else acc + part
    return acc


def _ffn_dense_kernel(x_ref, mod_ref, ng_ref, w1_ref, w3_ref, w2_ref, o_ref):
    sh, sc, gate = mod_ref[0, 3:4, :], mod_ref[0, 4:5, :], mod_ref[0, 5:6, :]
    x = x_ref[...]
    h = (_rms(x, ng_ref[...]) * (1.0 + sc) + sh).astype(BF16)
    o_ref[...] = x + gate * _swiglu(h, w1_ref, w3_ref, w2_ref)


def _ffn_dense_layer(x, mods, norm_g, w1, w3, w2):
    b, seq, d = x.shape
    t = b * seq
    tm = min(ROW_TILE, seq)
    per_batch = seq // tm
    fixed = lambda i: (0, 0)
    out = pl.pallas_call(
        _ffn_dense_kernel,
        grid=(t // tm,),
        in_specs=[
            pl.BlockSpec((tm, d), lambda i: (i, 0)),
            pl.BlockSpec((1, N_MOD, d), lambda i: (i // per_batch, 0, 0)),
            pl.BlockSpec((1, d), fixed),
            pl.BlockSpec((d, D_FF), fixed),
            pl.BlockSpec((d, D_FF), fixed),
            pl.BlockSpec((D_FF, d), fixed),
        ],
        out_specs=pl.BlockSpec((tm, d), lambda i: (i, 0)),
        out_shape=jax.ShapeDtypeStruct((t, d), F32),
        compiler_params=_params(("parallel",), 60),
        name="ffn_dense",
    )(x.reshape(t, d), mods, norm_g.reshape(1, d), w1, w3, w2)
    return out.reshape(b, seq, d)


def _head_norm_rope(z, ones_blk, gain, cos, sin):
    sq = z * z
    hi = sq.astype(BF16)
    lo = (sq - hi.astype(F32)).astype(BF16)
    ss = _dot(hi, ones_blk) + _dot(lo, ones_blk)
    zn = z * lax.rsqrt(ss * (1.0 / HEAD_DIM) + NORM_EPS) * gain
    if cos is None:
        return zn
    width = z.shape[1]
    fwd = pltpu.roll(zn, width - 16, 1)
    bwd = pltpu.roll(zn, 16, 1)
    lane = lax.broadcasted_iota(jnp.int32, zn.shape, 1)
    partner = jnp.where((lane & 31) < 16, fwd, bwd)
    return zn * cos + partner * sin


def _qkv_kernel(*refs, n_q, rope):
    x_ref, mod_ref, ng_ref, w_ref, e_ref, qg_ref, kg_ref = refs[:7]
    refs = refs[7:]
    if rope:
        cos_ref, sin_ref = refs[:2]
        refs = refs[2:]
        cos, sin = cos_ref[...], sin_ref[...]
    else:
        cos = sin = None
    sh, sc = mod_ref[0, 0:1, :], mod_ref[0, 1:2, :]
    hb = (_rms(x_ref[0], ng_ref[...]) * (1.0 + sc) + sh).astype(BF16)
    qkv = _dot(hb, w_ref[...])
    ones_blk = e_ref[...]
    if n_q:
        q_ref, k_ref, v_ref = refs
        for c in range(n_q):
            z = _head_norm_rope(qkv[:, c * KV_SLAB:(c + 1) * KV_SLAB], ones_blk, qg_ref[...], cos, sin)
            q_ref[0, :, c * KV_SLAB:(c + 1) * KV_SLAB] = (z * Q_SCALE).astype(BF16)
    else:
        k_ref, v_ref = refs
    base = n_q * KV_SLAB
    k_ref[0] = _head_norm_rope(qkv[:, base:base + KV_SLAB], ones_blk, kg_ref[...], cos, sin).astype(BF16)
    v_ref[0] = qkv[:, base + KV_SLAB:base + 2 * KV_SLAB].astype(BF16)


def _qkv_layer(x, mods, norm_g, w, q_gain, k_gain, tables):
    b, seq, d = x.shape
    tm = min(ROW_TILE, seq)
    width = w.shape[1]
    n_q = (width - 2 * KV_SLAB) // KV_SLAB
    rope = tables is not None
    seg = lax.broadcasted_iota(jnp.int32, (KV_SLAB, KV_SLAB), 0) // HEAD_DIM
    ones_blk = (seg == seg.T).astype(BF16)
    in_specs = [
        pl.BlockSpec((1, tm, d), lambda i, t: (i, t, 0)),
        pl.BlockSpec((1, N_MOD, d), lambda i, t: (i, 0, 0)),
        pl.BlockSpec((1, d), lambda i, t: (0, 0)),
        pl.BlockSpec((d, width), lambda i, t: (0, 0)),
        pl.BlockSpec((KV_SLAB, KV_SLAB), lambda i, t: (0, 0)),
        pl.BlockSpec((1, KV_SLAB), lambda i, t: (0, 0)),
        pl.BlockSpec((1, KV_SLAB), lambda i, t: (0, 0)),
    ]
    args = [x, mods, norm_g.reshape(1, d), w, ones_blk,
            jnp.tile(q_gain, N_KV_HEADS).reshape(1, KV_SLAB), jnp.tile(k_gain, N_KV_HEADS).reshape(1, KV_SLAB)]
    if rope:
        in_specs += [pl.BlockSpec((tm, KV_SLAB), lambda i, t: (t, 0))] * 2
        args += list(tables)
    kv_spec = pl.BlockSpec((1, tm, KV_SLAB), lambda i, t: (i, t, 0))
    kv_shape = jax.ShapeDtypeStruct((b, seq, KV_SLAB), BF16)
    out_specs, out_shape = [kv_spec, kv_spec], [kv_shape, kv_shape]
    if n_q:
        out_specs = [pl.BlockSpec((1, tm, n_q * KV_SLAB), lambda i, t: (i, t, 0))] + out_specs
        out_shape = [jax.ShapeDtypeStruct((b, seq, n_q * KV_SLAB), BF16)] + out_shape
    outs = pl.pallas_call(
        functools.partial(_qkv_kernel, n_q=n_q, rope=rope),
        grid=(b, seq // tm),
        in_specs=in_specs,
        out_specs=out_specs,
        out_shape=out_shape,
        compiler_params=_params(("parallel", "parallel"), 48),
        name="qkv_rope" if rope else "kv_ctx",
    )(*args)
    return outs if n_q else [None] + list(outs)


def _rope_tables(seq):
    half = HEAD_DIM // 4
    t = jnp.arange(seq, dtype=jnp.int32)
    pos = jnp.stack([t // GRID_W, t % GRID_W], axis=1).astype(F32)
    freqs = ROPE_THETA ** (-jnp.arange(half, dtype=F32) / half)
    ang = pos[:, :, None] * freqs[None, None, :]
    cos = jnp.cos(ang)
    sin = jnp.sin(ang)
    cos_h = jnp.concatenate([cos, cos], axis=2).reshape(seq, HEAD_DIM)
    sin_h = jnp.concatenate([-sin, sin], axis=2).reshape(seq, HEAD_DIM)
    return jnp.tile(cos_h, (1, N_KV_HEADS)), jnp.tile(sin_h, (1, N_KV_HEADS))


def _attn_kernel(q_ref, k_ref, vt_ref, o_ref, qt_scr, m_scr, acc_scr, sa_scr, sb_scr, smaxa_scr, smaxb_scr, *,
                 tq, tk, n_kblk, n_tiles):
    odd_head = lax.rem(pl.program_id(1), 2) == 1

    def load_queries(t):
        rows = pl.ds(pl.multiple_of(t * tq, tq), tq)
        for j in range(GROUP // 2):
            pair_t = q_ref[0, rows, j * 2 * HEAD_DIM:(j + 1) * 2 * HEAD_DIM].astype(F32).T
            for h in range(2):
                qt = pair_t[h * HEAD_DIM:(h + 1) * HEAD_DIM, :]
                zero = jnp.zeros_like(qt)
                low, high = jnp.concatenate([qt, zero], axis=0), jnp.concatenate([zero, qt], axis=0)
                qt_scr[2 * j + h] = jnp.where(odd_head, high, low).astype(BF16)

    def reset_state():
        m_scr[...] = jnp.full_like(m_scr, -jnp.inf)
        acc_scr[...] = jnp.zeros_like(acc_scr)

    def write_output(t):
        for g in range(GROUP):
            acc = acc_scr[g]
            out = acc[:HEAD_DIM] / acc[HEAD_DIM:HEAD_DIM + 1]
            o_ref[0, t, g * HEAD_DIM:(g + 1) * HEAD_DIM, :] = out.astype(BF16)

    n_chunks = tk // KEY_CHUNK

    def scores_chunk(g, k, c, buf):
        keys = k_ref[0, pl.ds(pl.multiple_of(k * tk + c * KEY_CHUNK, KEY_CHUNK), KEY_CHUNK), :]
        s = _dot(keys, qt_scr[g])
        buf[0][g, c * KEY_CHUNK:(c + 1) * KEY_CHUNK, :] = s
        return jnp.max(s, axis=0, keepdims=True)

    def values_chunk(g, k, c, buf, m_next):
        p = jnp.exp2(buf[0][g, c * KEY_CHUNK:(c + 1) * KEY_CHUNK, :] - m_next[0:1, :])
        return _dot(vt_ref[0, 0, k, :, c * KEY_CHUNK:(c + 1) * KEY_CHUNK], p.astype(BF16))

    def scores(k, buf):
        for g in range(GROUP):
            smax = None
            for c in range(n_chunks):
                cmax = scores_chunk(g, k, c, buf)
                smax = cmax if smax is None else jnp.maximum(smax, cmax)
            buf[1][g] = jnp.broadcast_to(smax, (8, tq))

    def scores_and_accumulate(k_new, buf_new, k_old, buf_old):
        for g in range(GROUP):
            m_prev = m_scr[g]
            m_next = jnp.maximum(m_prev, buf_old[1][g])
            alpha = jnp.exp2(m_prev - m_next)
            smax = pv = None
            for c in range(n_chunks):
                cmax = scores_chunk(g, k_new, c, buf_new)
                part = values_chunk(g, k_old, c, buf_old, m_next)
                smax = cmax if smax is None else jnp.maximum(smax, cmax)
                pv = part if pv is None else pv + part
            buf_new[1][g] = jnp.broadcast_to(smax, (8, tq))
            acc_scr[g] = alpha[0:1, :] * acc_scr[g] + pv
            m_scr[g] = m_next

    def query_tile(t, first, second):
        def key_block_pair(j, carry):
            k = 2 * j
            scores_and_accumulate(k + 1, second, k, first)
            scores_and_accumulate(k + 2, first, k + 1, second)
            return carry

        lax.fori_loop(0, (n_kblk - 1) // 2, key_block_pair, 0)
        load_queries(jnp.minimum(t + 1, n_tiles - 1))
        scores_and_accumulate(0, second, n_kblk - 1, first)
        write_output(t)
        reset_state()

    buf_a, buf_b = (sa_scr, smaxa_scr), (sb_scr, smaxb_scr)
    load_queries(0)
    reset_state()
    scores(0, buf_a)

    def query_tile_pair(i, carry):
        query_tile(2 * i, buf_a, buf_b)
        query_tile(2 * i + 1, buf_b, buf_a)
        return carry

    lax.fori_loop(0, n_tiles // 2, query_tile_pair, 0)


def _attention(q, k, vt_ext):
    b, seq, _ = q.shape
    n_kblk, v_rows, tk = vt_ext.shape[2:]
    lk = k.shape[1]
    tq, n_tiles = ATT_TQ, ATT_TILES
    assert n_kblk % 2 == 1 and n_tiles % 2 == 0 and seq % (tq * n_tiles) == 0 and lk == n_kblk * tk
    stat_rows = 8
    return pl.pallas_call(
        functools.partial(_attn_kernel, tq=tq, tk=tk, n_kblk=n_kblk, n_tiles=n_tiles),
        grid=(b, N_KV_HEADS, seq // (tq * n_tiles)),
        in_specs=[
            pl.BlockSpec((1, tq * n_tiles, KV_SLAB), lambda i, h, qi: (i, qi, h)),
            pl.BlockSpec((1, lk, 2 * HEAD_DIM), lambda i, h, qi: (i, 0, h // 2)),
            pl.BlockSpec((1, 1, n_kblk, v_rows, tk), lambda i, h, qi: (i, h, 0, 0, 0)),
        ],
        out_specs=pl.BlockSpec((1, n_tiles, KV_SLAB, tq), lambda i, h, qi: (i, qi, h, 0)),
        out_shape=jax.ShapeDtypeStruct((b, seq // tq, Q_DIM, tq), BF16),
        scratch_shapes=[
            pltpu.VMEM((GROUP, 2 * HEAD_DIM, tq), BF16),
            pltpu.VMEM((GROUP, stat_rows, tq), F32),
            pltpu.VMEM((GROUP, v_rows, tq), F32),
            pltpu.VMEM((GROUP, tk, tq), F32),
            pltpu.VMEM((GROUP, tk, tq), F32),
            pltpu.VMEM((GROUP, stat_rows, tq), F32),
            pltpu.VMEM((GROUP, stat_rows, tq), F32),
        ],
        compiler_params=_params(("parallel", "parallel", "parallel"), 60),
        name="gqa_attention",
    )(q, k, vt_ext)


def _oproj_router_kernel(a_ref, x_ref, mod_ref, ng_ref, wo_ref, rw_ref, x_out, h_out, e_out, g_out):
    g1 = mod_ref[0, 2:3, :]
    sh, sc = mod_ref[0, 3:4, :], mod_ref[0, 4:5, :]
    x = x_ref[...] + g1 * _dot(a_ref[...], wo_ref[...])
    x_out[...] = x
    h = _rms(x, ng_ref[...]) * (1.0 + sc) + sh
    _store_rows(h_out, h)
    logits = lax.dot_general(rw_ref[...], h, (((1,), (1,)), ((), ())),
                             preferred_element_type=F32, precision=lax.Precision.HIGHEST)
    eid = lax.broadcasted_iota(jnp.int32, logits.shape, 0)
    m1 = jnp.max(logits, axis=0, keepdims=True)
    i1 = jnp.min(jnp.where(logits == m1, eid, N_EXPERTS), axis=0, keepdims=True)
    rest = jnp.where(eid == i1, -jnp.inf, logits)
    m2 = jnp.max(rest, axis=0, keepdims=True)
    i2 = jnp.min(jnp.where(rest == m2, eid, N_EXPERTS), axis=0, keepdims=True)
    e = jnp.exp(m2 - m1)
    e_out[...] = jnp.concatenate([i1, i2], axis=0)
    g_out[...] = jnp.concatenate([1.0 / (1.0 + e), e / (1.0 + e)], axis=0)


def _oproj_router_layer(attn, x, mods, norm_g, w_o, router_w):
    b, seq, d = x.shape
    t = b * seq
    tm = ROW_TILE
    per_batch = seq // tm
    row = lambda i: (i, 0)
    fixed = lambda i: (0, 0)
    return pl.pallas_call(
        _oproj_router_kernel,
        grid=(t // tm,),
        in_specs=[
            pl.BlockSpec((tm, d), row),
            pl.BlockSpec((tm, d), row),
            pl.BlockSpec((1, N_MOD, d), lambda i: (i // per_batch, 0, 0)),
            pl.BlockSpec((1, d), fixed),
            pl.BlockSpec((d, d), fixed),
            pl.BlockSpec((N_EXPERTS, d), fixed),
        ],
        out_specs=[
            pl.BlockSpec((tm, d), row),
            pl.BlockSpec((tm * ROW_SLABS, LANES), row),
            pl.BlockSpec((TOP_K, tm), lambda i: (0, i)),
            pl.BlockSpec((TOP_K, tm), lambda i: (0, i)),
        ],
        out_shape=[
            jax.ShapeDtypeStruct((t, d), F32),
            jax.ShapeDtypeStruct((t * ROW_SLABS, LANES), F32),
            jax.ShapeDtypeStruct((TOP_K, t), jnp.int32),
            jax.ShapeDtypeStruct((TOP_K, t), F32),
        ],
        compiler_params=_params(("parallel",), 48),
        name="oproj_router",
    )(attn.reshape(t, d), x.reshape(t, d), mods, norm_g.reshape(1, d), w_o, router_w.T)


def _row_copy(src_ref, dst_ref, src_row, dst_row, sem):
    return pltpu.make_async_copy(_tile_of_row(src_ref, src_row), _tile_of_row(dst_ref, dst_row), sem)


def _rows_copy(src_ref, dst_ref, rows, sem):
    return pltpu.make_async_copy(src_ref.at[pl.ds(0, rows * ROW_SLABS)], dst_ref.at[pl.ds(0, rows * ROW_SLABS)], sem)


def _dest_spec(tm):
    return pl.BlockSpec((1, 1, TOP_K * tm), lambda i: (i, 0, 0), memory_space=pltpu.SMEM)


def _dispatch_kernel(dest_ref, h_ref, init_ref, xs_ref, sem):
    del init_ref
    tm = h_ref.shape[0] // ROW_SLABS

    def issue(r, carry):
        for k in range(TOP_K):
            _row_copy(h_ref, xs_ref, r, dest_ref[0, 0, TOP_K * r + k], sem).start()
        return carry

    lax.fori_loop(0, tm, issue, 0)
    for _ in range(TOP_K):
        _rows_copy(h_ref, xs_ref, tm, sem).wait()


def _dispatch(h, dest, cap):
    t = h.shape[0] // ROW_SLABS
    tm = DISPATCH_TILE
    any_spec = pl.BlockSpec(memory_space=pl.ANY)
    return pl.pallas_call(
        _dispatch_kernel,
        grid=(t // tm,),
        in_specs=[_dest_spec(tm), pl.BlockSpec((tm * ROW_SLABS, LANES), lambda i: (i, 0)), any_spec],
        out_specs=any_spec,
        out_shape=jax.ShapeDtypeStruct((cap * ROW_SLABS, LANES), h.dtype),
        scratch_shapes=[pltpu.SemaphoreType.DMA(())],
        input_output_aliases={2: 0},
        compiler_params=_params(("arbitrary",), 32),
        name="moe_dispatch",
    )(dest.reshape(t // tm, 1, TOP_K * tm), h, jnp.zeros((cap * ROW_SLABS, LANES), h.dtype))


def _ffn_expert_kernel(blk_e_ref, x_ref, w1_ref, w3_ref, w2_ref, o_ref):
    del blk_e_ref
    h = _load_rows(x_ref, MOE_ROWS).astype(BF16)
    _store_rows(o_ref, _swiglu(h, w1_ref.at[0], w3_ref.at[0], w2_ref.at[0]))


def _ffn_experts(xs, blk_e, w1, w3, w2):
    cap = xs.shape[0] // ROW_SLABS
    d = D_MODEL
    n_blk = cap // MOE_ROWS
    row_spec = pl.BlockSpec((MOE_ROWS * ROW_SLABS, LANES), lambda i, e: (i, 0))
    grid_spec = pltpu.PrefetchScalarGridSpec(
        num_scalar_prefetch=1,
        grid=(n_blk,),
        in_specs=[
            row_spec,
            pl.BlockSpec((1, d, D_FF), lambda i, e: (e[i], 0, 0)),
            pl.BlockSpec((1, d, D_FF), lambda i, e: (e[i], 0, 0)),
            pl.BlockSpec((1, D_FF, d), lambda i, e: (e[i], 0, 0)),
        ],
        out_specs=row_spec,
    )
    return pl.pallas_call(
        _ffn_expert_kernel,
        grid_spec=grid_spec,
        out_shape=jax.ShapeDtypeStruct(xs.shape, F32),
        compiler_params=_params(("arbitrary",), 60),
        name="ffn_experts",
    )(blk_e, xs, w1, w3, w2)


def _routing_plan(e_idx):
    flat_e = e_idx.T.reshape(-1)
    n_assign = flat_e.shape[0]
    onehot = (flat_e[:, None] == jnp.arange(N_EXPERTS, dtype=jnp.int32)[None, :]).astype(jnp.int32)
    csum = jnp.cumsum(onehot, axis=0)
    sizes = csum[-1]
    padded = (sizes + MOE_ROWS - 1) // MOE_ROWS * MOE_ROWS
    ends = jnp.cumsum(padded)
    pstart = ends - padded
    dest = jnp.sum(onehot * (csum - 1 + pstart[None, :]), axis=1).astype(jnp.int32)
    cap = n_assign + N_EXPERTS * MOE_ROWS
    blk_start = jnp.arange(cap // MOE_ROWS, dtype=jnp.int32) * MOE_ROWS
    blk_e = jnp.sum((ends[None, :] <= blk_start[:, None]).astype(jnp.int32), axis=1)
    blk_e = jnp.minimum(blk_e, N_EXPERTS - 1)
    return dest, blk_e, cap


def _final_kernel(dest_ref, dest_next_ref, x_ref, gt_ref, mod_ref, fg_ref, y_ref, o_ref, y_buf, sems):
    tm = x_ref.shape[0]
    i = pl.program_id(0)
    slot = lax.rem(i, 2)
    half = TOP_K * tm

    def fetch(idx_ref, into):
        def issue(r, carry):
            for k in range(TOP_K):
                _row_copy(y_ref, y_buf, idx_ref[0, 0, TOP_K * r + k], into * half + k * tm + r, sems.at[into]).start()
            return carry

        lax.fori_loop(0, tm, issue, 0, unroll=8)

    @pl.when(i == 0)
    def _():
        fetch(dest_ref, slot)

    @pl.when(i + 1 < pl.num_programs(0))
    def _():
        fetch(dest_next_ref, 1 - slot)

    _rows_copy(y_ref, y_buf, half, sems.at[slot]).wait()
    gt = gt_ref[...]
    moe = (gt[:, 0:1] * _load_rows(y_buf, tm, slot * half) + gt[:, 1:2] * _load_rows(y_buf, tm, slot * half + tm))
    x = x_ref[...] + mod_ref[0, 5:6, :] * moe
    o_ref[...] = _rms(x, fg_ref[...])


def _final_layer(x, ys, dest, gates_t, mods, final_g, seq):
    t, d = x.shape
    tm = ROW_TILE
    per_batch = seq // tm
    n_steps = t // tm
    dest_tiles = dest.reshape(n_steps, 1, TOP_K * tm)
    return pl.pallas_call(
        _final_kernel,
        grid=(n_steps,),
        in_specs=[
            _dest_spec(tm),
            pl.BlockSpec((1, 1, TOP_K * tm), lambda i: (jnp.minimum(i + 1, n_steps - 1), 0, 0),
                         memory_space=pltpu.SMEM),
            pl.BlockSpec((tm, d), lambda i: (i, 0)),
            pl.BlockSpec((tm, TOP_K), lambda i: (i, 0)),
            pl.BlockSpec((1, N_MOD, d), lambda i: (i // per_batch, 0, 0)),
            pl.BlockSpec((1, d), lambda i: (0, 0)),
            pl.BlockSpec(memory_space=pl.ANY),
        ],
        out_specs=pl.BlockSpec((tm, d), lambda i: (i, 0)),
        out_shape=jax.ShapeDtypeStruct((t, d), F32),
        scratch_shapes=[pltpu.VMEM((2 * TOP_K * tm * ROW_SLABS, LANES), F32), pltpu.SemaphoreType.DMA((2,))],
        compiler_params=_params(("arbitrary",), 48),
        name="moe_sum_final_norm",
    )(dest_tiles, dest_tiles, x, gates_t, mods, final_g.reshape(1, d), ys)


def kernel(x, c, ctx, c_ctx, ada_w, ada_b, norm_g, pool_w, pool_scale, ffn_w1, ffn_w3, ffn_w2, w_qkv, w_o,
           q_norm_g, k_norm_g, router_w, moe_w1, moe_w3, moe_w2, final_g):
    b, seq, d = x.shape
    assert ada_w.shape[0] == 2 and b <= 7, "written for the two-layer block (pool+dense, attention+experts)"

    cvecs = jnp.zeros((8, d), F32).at[:b].set(c).at[b].set(c_ctx)
    mods = _adaln(cvecs, ada_w, ada_b).reshape(2, 8, N_MOD, d)
    mods_x = mods[:, :b]
    mods_c = jnp.broadcast_to(mods[:, b:b + 1], (2, b, N_MOD, d))

    pool_wb = pool_w[0].astype(BF16)
    w1b, w3b, w2b = ffn_w1[0].astype(BF16), ffn_w3[0].astype(BF16), ffn_w2[0].astype(BF16)
    x = _pool_layer(x, mods_x[0], norm_g[0, 0], pool_wb, pool_scale[0])
    ctx = _pool_layer(ctx, mods_c[0], norm_g[0, 0], pool_wb, pool_scale[0])
    x = _ffn_dense_layer(x, mods_x[0], norm_g[0, 1], w1b, w3b, w2b)
    ctx = _ffn_dense_layer(ctx, mods_c[0], norm_g[0, 1], w1b, w3b, w2b)

    wqkv = w_qkv[0].astype(BF16)
    q, kx, vx = _qkv_layer(x, mods_x[1], norm_g[1, 0], wqkv, q_norm_g[0], k_norm_g[0], _rope_tables(seq))
    _, kc, vc = _qkv_layer(ctx, mods_c[1], norm_g[1, 0], wqkv[:, Q_DIM:], q_norm_g[0], k_norm_g[0], None)
    k_all = jnp.concatenate([kx, kc], axis=1)
    v_all = jnp.concatenate([vx, vc], axis=1)
    lk = k_all.shape[1]
    n_kblk = lk // ATT_TK
    vt = v_all.reshape(b, n_kblk, ATT_TK, N_KV_HEADS, HEAD_DIM).transpose(0, 3, 1, 4, 2)
    vt_ext = jnp.concatenate([vt, jnp.ones_like(vt[:, :, :, :ONES_ROWS])], axis=3)
    attn_t = _attention(q, k_all, vt_ext)
    attn = attn_t.transpose(0, 1, 3, 2).reshape(b, seq, Q_DIM)

    x3, hf, e_idx, gates = _oproj_router_layer(attn, x, mods_x[1], norm_g[1, 1], w_o[0].astype(BF16), router_w[0])
    dest, blk_e, cap = _routing_plan(e_idx)
    xs = _dispatch(hf, dest, cap)
    ys = _ffn_experts(xs, blk_e, moe_w1[0].astype(BF16), moe_w3[0].astype(BF16), moe_w2[0].astype(BF16))
    out = _final_layer(x3, ys, dest, gates.T, mods_x[1], final_g, seq)
    return out.reshape(b, seq, d)
```

```python
import functools
import math

import jax
import jax.numpy as jnp
from jax import lax
from jax.experimental import pallas as pl
from jax.experimental.pallas import tpu as pltpu

F32 = jnp.float32
BF16 = jnp.bfloat16

D_MODEL = 1024
GRID_W = 64
POOL_WINDOWS = (2, 4, 8, 16)
POOL_GROUP_DIM = D_MODEL // len(POOL_WINDOWS)
HEAD_DIM = 64
N_KV_HEADS = 4
GROUP = 4
KV_SLAB = N_KV_HEADS * HEAD_DIM
Q_DIM = N_KV_HEADS * GROUP * HEAD_DIM
ROPE_THETA = 10000.0
Q_SCALE = HEAD_DIM ** -0.5 * math.log2(math.e)
D_FF = 2816
N_EXPERTS = 8
TOP_K = 2
NORM_EPS = 1e-6
N_MOD = 6

POOL_HALO = 16
ROW_TILE = 512
FF_TILE = 1408
MOE_ROWS = 512
DISPATCH_TILE = 2048
ATT_TQ = 512
ATT_TK = 1280
KEY_CHUNK = 256
DENOM_ROWS = 8
ATT_TILES = 8
MIB = 1024 * 1024
LANES = 128
ROW_SLABS = D_MODEL // LANES


def _dot(a, b):
    return jnp.dot(a, b, preferred_element_type=F32)


def _rms(x, g):
    ms = jnp.mean(x * x, axis=-1, keepdims=True)
    return x * lax.rsqrt(ms + NORM_EPS) * g


def _silu(x):
    return x * (1.0 / (1.0 + jnp.exp(-x)))


def _load_rows(ref, n_rows, first=0):
    return jnp.concatenate(
        [ref[pl.ds(ROW_SLABS * first + s, n_rows, stride=ROW_SLABS), :] for s in range(ROW_SLABS)], axis=1)


def _store_rows(ref, val):
    for s in range(ROW_SLABS):
        ref[pl.ds(s, val.shape[0], stride=ROW_SLABS), :] = val[:, s * LANES:(s + 1) * LANES]


def _tile_of_row(ref, row):
    return ref.at[pl.ds(pl.multiple_of(row * ROW_SLABS, ROW_SLABS), ROW_SLABS)]


def _params(semantics, vmem_mib):
    return pltpu.CompilerParams(dimension_semantics=semantics, vmem_limit_bytes=vmem_mib * MIB)


def _adaln_kernel(c_ref, w_ref, b_ref, o_ref):
    s = _silu(c_ref[...])
    o_ref[0] = jnp.dot(s, w_ref[0], preferred_element_type=F32, precision=lax.Precision.HIGHEST) + b_ref[0]


def _adaln(cvecs, ada_w, ada_b):
    depth, d, n = ada_w.shape
    tn = n // 4
    return pl.pallas_call(
        _adaln_kernel,
        grid=(depth, n // tn),
        in_specs=[
            pl.BlockSpec((8, d), lambda i, j: (0, 0)),
            pl.BlockSpec((1, d, tn), lambda i, j: (i, 0, j)),
            pl.BlockSpec((1, 1, tn), lambda i, j: (i, 0, j)),
        ],
        out_specs=pl.BlockSpec((1, 8, tn), lambda i, j: (i, 0, j)),
        out_shape=jax.ShapeDtypeStruct((depth, 8, n), F32),
        compiler_params=_params(("parallel", "parallel"), 32),
        name="adaln",
    )(cvecs, ada_w, ada_b.reshape(depth, 1, n))


def _pool_kernel(xp_ref, xc_ref, xn_ref, mod_ref, ng_ref, pw_ref, ps_ref, o_ref, *, seq, tm):
    t = pl.program_id(1)
    xc = xc_ref[0]
    xa = jnp.concatenate([xp_ref[0], xc, xn_ref[0]], axis=0)
    sh, sc, gate = mod_ref[0, 0:1, :], mod_ref[0, 1:2, :], mod_ref[0, 2:3, :]
    h = _rms(xa, ng_ref[...]) * (1.0 + sc) + sh
    hb = h.astype(BF16)
    wide = tm + 2 * POOL_HALO
    row = lax.broadcasted_iota(jnp.int32, (tm, wide), 0) + t * tm
    col = lax.broadcasted_iota(jnp.int32, (tm, wide), 1) + (t * tm - POOL_HALO)
    row1 = lax.broadcasted_iota(jnp.int32, (tm, 1), 0) + t * tm
    ys = []
    for g, w in enumerate(POOL_WINDOWS):
        half = w // 2
        lo = jnp.maximum(row - half, 0)
        hi = jnp.minimum(row + half, seq)
        band = jnp.where((col >= lo) & (col < hi), 1.0, 0.0).astype(BF16)
        cnt = (jnp.minimum(row1 + half, seq) - jnp.maximum(row1 - half, 0)).astype(F32)
        lanes = slice(g * POOL_GROUP_DIM, (g + 1) * POOL_GROUP_DIM)
        win_sum = _dot(band, hb[:, lanes])
        d = win_sum / cnt - h[POOL_HALO:POOL_HALO + tm, lanes]
        ys.append(_dot(d.astype(BF16), pw_ref[g]))
    y = jnp.concatenate(ys, axis=1) * ps_ref[...]
    o_ref[0] = xc + gate * y


def _pool_layer(x, mods, norm_g, pool_w, pool_scale):
    b, seq, d = x.shape
    tm = min(ROW_TILE, seq)
    per = tm // POOL_HALO
    last = seq // POOL_HALO - 1
    return pl.pallas_call(
        functools.partial(_pool_kernel, seq=seq, tm=tm),
        grid=(b, seq // tm),
        in_specs=[
            pl.BlockSpec((1, POOL_HALO, d), lambda i, t: (i, jnp.maximum(t * per - 1, 0), 0)),
            pl.BlockSpec((1, tm, d), lambda i, t: (i, t, 0)),
            pl.BlockSpec((1, POOL_HALO, d), lambda i, t: (i, jnp.minimum((t + 1) * per, last), 0)),
            pl.BlockSpec((1, N_MOD, d), lambda i, t: (i, 0, 0)),
            pl.BlockSpec((1, d), lambda i, t: (0, 0)),
            pl.BlockSpec(pool_w.shape, lambda i, t: (0, 0, 0)),
            pl.BlockSpec((1, d), lambda i, t: (0, 0)),
        ],
        out_specs=pl.BlockSpec((1, tm, d), lambda i, t: (i, t, 0)),
        out_shape=jax.ShapeDtypeStruct(x.shape, F32),
        compiler_params=_params(("parallel", "parallel"), 48),
        name="pool_mix",
    )(x, x, x, mods, norm_g.reshape(1, d), pool_w, pool_scale.reshape(1, d))


def _swiglu(h, w1_ref, w3_ref, w2_ref):
    acc = None
    for f in range(D_FF // FF_TILE):
        cols = slice(f * FF_TILE, (f + 1) * FF_TILE)
        u = _silu(_dot(h, w1_ref[:, cols])) * _dot(h, w3_ref[:, cols])
        part = _dot(u.astype(BF16), w2_ref[cols, :])
        acc = part if acc is None else acc + part
    return acc


def _ffn_dense_kernel(x_ref, mod_ref, ng_ref, w1_ref, w3_ref, w2_ref, o_ref):
    sh, sc, gate = mod_ref[0, 3:4, :], mod_ref[0, 4:5, :], mod_ref[0, 5:6, :]
    x = x_ref[...]
    h = (_rms(x, ng_ref[...]) * (1.0 + sc) + sh).astype(BF16)
    o_ref[...] = x + gate * _swiglu(h, w1_ref, w3_ref, w2_ref)


def _ffn_dense_layer(x, mods, norm_g, w1, w3, w2):
    b, seq, d = x.shape
    t = b * seq
    tm = min(ROW_TILE, seq)
    per_batch = seq // tm
    fixed = lambda i: (0, 0)
    out = pl.pallas_call(
        _ffn_dense_kernel,
        grid=(t // tm,),
        in_specs=[
            pl.BlockSpec((tm, d), lambda i: (i, 0)),
            pl.BlockSpec((1, N_MOD, d), lambda i: (i // per_batch, 0, 0)),
            pl.BlockSpec((1, d), fixed),
            pl.BlockSpec((d, D_FF), fixed),
            pl.BlockSpec((d, D_FF), fixed),
            pl.BlockSpec((D_FF, d), fixed),
        ],
        out_specs=pl.BlockSpec((tm, d), lambda i: (i, 0)),
        out_shape=jax.ShapeDtypeStruct((t, d), F32),
        compiler_params=_params(("parallel",), 60),
        name="ffn_dense",
    )(x.reshape(t, d), mods, norm_g.reshape(1, d), w1, w3, w2)
    return out.reshape(b, seq, d)


def _head_norm_rope(z, ones_blk, gain, cos, sin):
    sq = z * z
    hi = sq.astype(BF16)
    lo = (sq - hi.astype(F32)).astype(BF16)
    ss = _dot(hi, ones_blk) + _dot(lo, ones_blk)
    zn = z * lax.rsqrt(ss * (1.0 / HEAD_DIM) + NORM_EPS) * gain
    if cos is None:
        return zn
    width = z.shape[1]
    fwd = pltpu.roll(zn, width - 16, 1)
    bwd = pltpu.roll(zn, 16, 1)
    lane = lax.broadcasted_iota(jnp.int32, zn.shape, 1)
    partner = jnp.where((lane & 31) < 16, fwd, bwd)
    return zn * cos + partner * sin


def _qkv_kernel(*refs, n_q, rope):
    x_ref, mod_ref, ng_ref, w_ref, e_ref, qg_ref, kg_ref = refs[:7]
    refs = refs[7:]
    if rope:
        cos_ref, sin_ref = refs[:2]
        refs = refs[2:]
        cos, sin = cos_ref[...], sin_ref[...]
    else:
        cos = sin = None
    sh, sc = mod_ref[0, 0:1, :], mod_ref[0, 1:2, :]
    hb = (_rms(x_ref[0], ng_ref[...]) * (1.0 + sc) + sh).astype(BF16)
    qkv = _dot(hb, w_ref[...])
    ones_blk = e_ref[...]
    if n_q:
        q_ref, k_ref, v_ref = refs
        for c in range(n_q):
            z = _head_norm_rope(qkv[:, c * KV_SLAB:(c + 1) * KV_SLAB], ones_blk, qg_ref[...], cos, sin)
            q_ref[0, :, c * KV_SLAB:(c + 1) * KV_SLAB] = (z * Q_SCALE).astype(BF16)
    else:
        k_ref, v_ref = refs
    base = n_q * KV_SLAB
    k_ref[0] = _head_norm_rope(qkv[:, base:base + KV_SLAB], ones_blk, kg_ref[...], cos, sin).astype(BF16)
    v_ref[0] = qkv[:, base + KV_SLAB:base + 2 * KV_SLAB].astype(BF16)


def _qkv_layer(x, mods, norm_g, w, q_gain, k_gain, tables):
    b, seq, d = x.shape
    tm = min(ROW_TILE, seq)
    width = w.shape[1]
    n_q = (width - 2 * KV_SLAB) // KV_SLAB
    rope = tables is not None
    seg = lax.broadcasted_iota(jnp.int32, (KV_SLAB, KV_SLAB), 0) // HEAD_DIM
    ones_blk = (seg == seg.T).astype(BF16)
    in_specs = [
        pl.BlockSpec((1, tm, d), lambda i, t: (i, t, 0)),
        pl.BlockSpec((1, N_MOD, d), lambda i, t: (i, 0, 0)),
        pl.BlockSpec((1, d), lambda i, t: (0, 0)),
        pl.BlockSpec((d, width), lambda i, t: (0, 0)),
        pl.BlockSpec((KV_SLAB, KV_SLAB), lambda i, t: (0, 0)),
        pl.BlockSpec((1, KV_SLAB), lambda i, t: (0, 0)),
        pl.BlockSpec((1, KV_SLAB), lambda i, t: (0, 0)),
    ]
    args = [x, mods, norm_g.reshape(1, d), w, ones_blk,
            jnp.tile(q_gain, N_KV_HEADS).reshape(1, KV_SLAB), jnp.tile(k_gain, N_KV_HEADS).reshape(1, KV_SLAB)]
    if rope:
        in_specs += [pl.BlockSpec((tm, KV_SLAB), lambda i, t: (t, 0))] * 2
        args += list(tables)
    kv_spec = pl.BlockSpec((1, tm, KV_SLAB), lambda i, t: (i, t, 0))
    kv_shape = jax.ShapeDtypeStruct((b, seq, KV_SLAB), BF16)
    out_specs, out_shape = [kv_spec, kv_spec], [kv_shape, kv_shape]
    if n_q:
        out_specs = [pl.BlockSpec((1, tm, n_q * KV_SLAB), lambda i, t: (i, t, 0))] + out_specs
        out_shape = [jax.ShapeDtypeStruct((b, seq, n_q * KV_SLAB), BF16)] + out_shape
    outs = pl.pallas_call(
        functools.partial(_qkv_kernel, n_q=n_q, rope=rope),
        grid=(b, seq // tm),
        in_specs=in_specs,
        out_specs=out_specs,
        out_shape=out_shape,
        compiler_params=_params(("parallel", "parallel"), 48),
        name="qkv_rope" if rope else "kv_ctx",
    )(*args)
    return outs if n_q else [None] + list(outs)


def _rope_tables(seq):
    half = HEAD_DIM // 4
    t = jnp.arange(seq, dtype=jnp.int32)
    pos = jnp.stack([t // GRID_W, t % GRID_W], axis=1).astype(F32)
    freqs = ROPE_THETA ** (-jnp.arange(half, dtype=F32) / half)
    ang = pos[:, :, None] * freqs[None, None, :]
    cos = jnp.cos(ang)
    sin = jnp.sin(ang)
    cos_h = jnp.concatenate([cos, cos], axis=2).reshape(seq, HEAD_DIM)
    sin_h = jnp.concatenate([-sin, sin], axis=2).reshape(seq, HEAD_DIM)
    return jnp.tile(cos_h, (1, N_KV_HEADS)), jnp.tile(sin_h, (1, N_KV_HEADS))


def _attn_kernel(q_ref, k_ref, vt_ref, o_ref, qt_scr, m_scr, acc_scr, sa_scr, sb_scr, smaxa_scr, smaxb_scr, *,
                 tq, tk, n_kblk, n_tiles):
    odd_head = lax.rem(pl.program_id(1), 2) == 1

    def load_queries(t):
        rows = pl.ds(pl.multiple_of(t * tq, tq), tq)
        for j in range(GROUP // 2):
            pair_t = q_ref[0, rows, j * 2 * HEAD_DIM:(j + 1) * 2 * HEAD_DIM].astype(F32).T
            for h in range(2):
                qt = pair_t[h * HEAD_DIM:(h + 1) * HEAD_DIM, :]
                zero = jnp.zeros_like(qt)
                low, high = jnp.concatenate([qt, zero], axis=0), jnp.concatenate([zero, qt], axis=0)
                qt_scr[2 * j + h] = jnp.where(odd_head, high, low).astype(BF16)

    def reset_state():
        m_scr[...] = jnp.full_like(m_scr, -jnp.inf)
        acc_scr[...] = jnp.zeros_like(acc_scr)

    def write_output(t):
        for g in range(GROUP):
            acc = acc_scr[g]
            out = acc[:HEAD_DIM] / jnp.sum(acc[HEAD_DIM:], axis=0, keepdims=True)
            o_ref[0, t, g * HEAD_DIM:(g + 1) * HEAD_DIM, :] = out.astype(BF16)

    n_chunks = tk // KEY_CHUNK

    def scores_chunk(g, k, c, buf):
        keys = k_ref[0, pl.ds(pl.multiple_of(k * tk + c * KEY_CHUNK, KEY_CHUNK), KEY_CHUNK), :]
        s = _dot(keys, qt_scr[g])
        buf[0][g, c * KEY_CHUNK:(c + 1) * KEY_CHUNK, :] = s
        return jnp.max(s, axis=0, keepdims=True)

    def values_chunk(g, k, c, buf, m_next):
        p = jnp.exp2(buf[0][g, c * KEY_CHUNK:(c + 1) * KEY_CHUNK, :] - m_next[0:1, :])
        psum = jnp.sum(p.reshape(KEY_CHUNK // DENOM_ROWS, DENOM_ROWS, tq), axis=0)
        pv = _dot(vt_ref[0, 0, k, :, c * KEY_CHUNK:(c + 1) * KEY_CHUNK], p.astype(BF16))
        return jnp.concatenate([pv, psum], axis=0)

    def scores(k, buf):
        for g in range(GROUP):
            smax = None
            for c in range(n_chunks):
                cmax = scores_chunk(g, k, c, buf)
                smax = cmax if smax is None else jnp.maximum(smax, cmax)
            buf[1][g] = jnp.broadcast_to(smax, (8, tq))

    def scores_and_accumulate(k_new, buf_new, k_old, buf_old):
        for g in range(GROUP):
            m_prev = m_scr[g]
            m_next = jnp.maximum(m_prev, buf_old[1][g])
            alpha = jnp.exp2(m_prev - m_next)
            smax = pv = None
            for c in range(n_chunks):
                cmax = scores_chunk(g, k_new, c, buf_new)
                part = values_chunk(g, k_old, c, buf_old, m_next)
                smax = cmax if smax is None else jnp.maximum(smax, cmax)
                pv = part if pv is None else pv + part
            buf_new[1][g] = jnp.broadcast_to(smax, (8, tq))
            acc_scr[g] = alpha[0:1, :] * acc_scr[g] + pv
            m_scr[g] = m_next

    def query_tile(t, first, second):
        def key_block_pair(j, carry):
            k = 2 * j
            scores_and_accumulate(k + 1, second, k, first)
            scores_and_accumulate(k + 2, first, k + 1, second)
            return carry

        lax.fori_loop(0, (n_kblk - 1) // 2, key_block_pair, 0)
        load_queries(jnp.minimum(t + 1, n_tiles - 1))
        scores_and_accumulate(0, second, n_kblk - 1, first)
        write_output(t)
        reset_state()

    buf_a, buf_b = (sa_scr, smaxa_scr), (sb_scr, smaxb_scr)
    load_queries(0)
    reset_state()
    scores(0, buf_a)

    def query_tile_pair(i, carry):
        query_tile(2 * i, buf_a, buf_b)
        query_tile(2 * i + 1, buf_b, buf_a)
        return carry

    lax.fori_loop(0, n_tiles // 2, query_tile_pair, 0)


def _attention(q, k, vt_ext):
    b, seq, _ = q.shape
    n_kblk, v_rows, tk = vt_ext.shape[2:]
    lk = k.shape[1]
    tq, n_tiles = ATT_TQ, ATT_TILES
    assert n_kblk % 2 == 1 and n_tiles % 2 == 0 and seq % (tq * n_tiles) == 0 and lk == n_kblk * tk
    stat_rows = 8
    return pl.pallas_call(
        functools.partial(_attn_kernel, tq=tq, tk=tk, n_kblk=n_kblk, n_tiles=n_tiles),
        grid=(b, N_KV_HEADS, seq // (tq * n_tiles)),
        in_specs=[
            pl.BlockSpec((1, tq * n_tiles, KV_SLAB), lambda i, h, qi: (i, qi, h)),
            pl.BlockSpec((1, lk, 2 * HEAD_DIM), lambda i, h, qi: (i, 0, h // 2)),
            pl.BlockSpec((1, 1, n_kblk, v_rows, tk), lambda i, h, qi: (i, h, 0, 0, 0)),
        ],
        out_specs=pl.BlockSpec((1, n_tiles, KV_SLAB, tq), lambda i, h, qi: (i, qi, h, 0)),
        out_shape=jax.ShapeDtypeStruct((b, seq // tq, Q_DIM, tq), BF16),
        scratch_shapes=[
            pltpu.VMEM((GROUP, 2 * HEAD_DIM, tq), BF16),
            pltpu.VMEM((GROUP, stat_rows, tq), F32),
            pltpu.VMEM((GROUP, v_rows + DENOM_ROWS, tq), F32),
            pltpu.VMEM((GROUP, tk, tq), F32),
            pltpu.VMEM((GROUP, tk, tq), F32),
            pltpu.VMEM((GROUP, stat_rows, tq), F32),
            pltpu.VMEM((GROUP, stat_rows, tq), F32),
        ],
        compiler_params=_params(("parallel", "parallel", "parallel"), 60),
        name="gqa_attention",
    )(q, k, vt_ext)


def _oproj_router_kernel(a_ref, x_ref, mod_ref, ng_ref, wo_ref, rw_ref, x_out, h_out, e_out, g_out):
    g1 = mod_ref[0, 2:3, :]
    sh, sc = mod_ref[0, 3:4, :], mod_ref[0, 4:5, :]
    proj = lax.dot_general(a_ref[0], wo_ref[...], (((0,), (0,)), ((), ())), preferred_element_type=F32)
    x = x_ref[...] + g1 * proj
    x_out[...] = x
    h = _rms(x, ng_ref[...]) * (1.0 + sc) + sh
    _store_rows(h_out, h)
    logits = lax.dot_general(rw_ref[...], h, (((1,), (1,)), ((), ())),
                             preferred_element_type=F32, precision=lax.Precision.HIGHEST)
    eid = lax.broadcasted_iota(jnp.int32, logits.shape, 0)
    m1 = jnp.max(logits, axis=0, keepdims=True)
    i1 = jnp.min(jnp.where(logits == m1, eid, N_EXPERTS), axis=0, keepdims=True)
    rest = jnp.where(eid == i1, -jnp.inf, logits)
    m2 = jnp.max(rest, axis=0, keepdims=True)
    i2 = jnp.min(jnp.where(rest == m2, eid, N_EXPERTS), axis=0, keepdims=True)
    e = jnp.exp(m2 - m1)
    e_out[...] = jnp.concatenate([i1, i2], axis=0)
    g_out[...] = jnp.concatenate([1.0 / (1.0 + e), e / (1.0 + e)], axis=0)


def _oproj_router_layer(attn_t, x, mods, norm_g, w_o, router_w):
    b, seq, d = x.shape
    t = b * seq
    tm = ROW_TILE
    per_batch = seq // tm
    row = lambda i: (i, 0)
    fixed = lambda i: (0, 0)
    return pl.pallas_call(
        _oproj_router_kernel,
        grid=(t // tm,),
        in_specs=[
            pl.BlockSpec((1, d, tm), lambda i: (i, 0, 0)),
            pl.BlockSpec((tm, d), row),
            pl.BlockSpec((1, N_MOD, d), lambda i: (i // per_batch, 0, 0)),
            pl.BlockSpec((1, d), fixed),
            pl.BlockSpec((d, d), fixed),
            pl.BlockSpec((N_EXPERTS, d), fixed),
        ],
        out_specs=[
            pl.BlockSpec((tm, d), row),
            pl.BlockSpec((tm * ROW_SLABS, LANES), row),
            pl.BlockSpec((TOP_K, tm), lambda i: (0, i)),
            pl.BlockSpec((TOP_K, tm), lambda i: (0, i)),
        ],
        out_shape=[
            jax.ShapeDtypeStruct((t, d), F32),
            jax.ShapeDtypeStruct((t * ROW_SLABS, LANES), F32),
            jax.ShapeDtypeStruct((TOP_K, t), jnp.int32),
            jax.ShapeDtypeStruct((TOP_K, t), F32),
        ],
        compiler_params=_params(("parallel",), 48),
        name="oproj_router",
    )(attn_t.reshape(t // tm, d, tm), x.reshape(t, d), mods, norm_g.reshape(1, d), w_o, router_w.T)


def _row_copy(src_ref, dst_ref, src_row, dst_row, sem):
    return pltpu.make_async_copy(_tile_of_row(src_ref, src_row), _tile_of_row(dst_ref, dst_row), sem)


def _rows_copy(src_ref, dst_ref, rows, sem):
    return pltpu.make_async_copy(src_ref.at[pl.ds(0, rows * ROW_SLABS)], dst_ref.at[pl.ds(0, rows * ROW_SLABS)], sem)


def _dest_spec(tm):
    return pl.BlockSpec((1, 1, TOP_K * tm), lambda i: (i, 0, 0), memory_space=pltpu.SMEM)


def _dispatch_kernel(dest_ref, zero_ref, h_ref, xs_ref, zeros_scr, sem):
    tm = h_ref.shape[0] // ROW_SLABS

    @pl.when(pl.program_id(0) == 0)
    def _():
        zeros_scr[...] = jnp.zeros_like(zeros_scr)
        block_rows = MOE_ROWS * ROW_SLABS

        def zero_block(j):
            return pltpu.make_async_copy(
                zeros_scr, xs_ref.at[pl.ds(pl.multiple_of(j * block_rows, block_rows), block_rows)], sem)

        def start(j, carry):
            @pl.when(zero_ref[j] != 0)
            def _():
                zero_block(j).start()
            return carry

        def drain(j, carry):
            @pl.when(zero_ref[j] != 0)
            def _():
                zero_block(j).wait()
            return carry

        lax.fori_loop(0, zero_ref.shape[0], start, 0)
        lax.fori_loop(0, zero_ref.shape[0], drain, 0)

    def issue(r, carry):
        for k in range(TOP_K):
            _row_copy(h_ref, xs_ref, r, dest_ref[0, 0, TOP_K * r + k], sem).start()
        return carry

    lax.fori_loop(0, tm, issue, 0)
    for _ in range(TOP_K):
        _rows_copy(h_ref, xs_ref, tm, sem).wait()


def _dispatch(h, dest, zero_blocks, cap):
    t = h.shape[0] // ROW_SLABS
    tm = DISPATCH_TILE
    return pl.pallas_call(
        _dispatch_kernel,
        grid=(t // tm,),
        in_specs=[
            _dest_spec(tm),
            pl.BlockSpec(memory_space=pltpu.SMEM),
            pl.BlockSpec((tm * ROW_SLABS, LANES), lambda i: (i, 0)),
        ],
        out_specs=pl.BlockSpec(memory_space=pl.ANY),
        out_shape=jax.ShapeDtypeStruct((cap * ROW_SLABS, LANES), h.dtype),
        scratch_shapes=[pltpu.VMEM((MOE_ROWS * ROW_SLABS, LANES), h.dtype), pltpu.SemaphoreType.DMA(())],
        compiler_params=_params(("arbitrary",), 32),
        name="moe_dispatch",
    )(dest.reshape(t // tm, 1, TOP_K * tm), zero_blocks, h)


def _ffn_expert_kernel(blk_e_ref, x_ref, w1_ref, w3_ref, w2_ref, o_ref):
    del blk_e_ref
    h = _load_rows(x_ref, MOE_ROWS).astype(BF16)
    _store_rows(o_ref, _swiglu(h, w1_ref.at[0], w3_ref.at[0], w2_ref.at[0]))


def _ffn_experts(xs, blk_e, w1, w3, w2):
    cap = xs.shape[0] // ROW_SLABS
    d = D_MODEL
    n_blk = cap // MOE_ROWS
    row_spec = pl.BlockSpec((MOE_ROWS * ROW_SLABS, LANES), lambda i, e: (i, 0))
    grid_spec = pltpu.PrefetchScalarGridSpec(
        num_scalar_prefetch=1,
        grid=(n_blk,),
        in_specs=[
            row_spec,
            pl.BlockSpec((1, d, D_FF), lambda i, e: (e[i], 0, 0)),
            pl.BlockSpec((1, d, D_FF), lambda i, e: (e[i], 0, 0)),
            pl.BlockSpec((1, D_FF, d), lambda i, e: (e[i], 0, 0)),
        ],
        out_specs=row_spec,
    )
    return pl.pallas_call(
        _ffn_expert_kernel,
        grid_spec=grid_spec,
        out_shape=jax.ShapeDtypeStruct(xs.shape, F32),
        compiler_params=_params(("arbitrary",), 60),
        name="ffn_experts",
    )(blk_e, xs, w1, w3, w2)


def _routing_plan(e_idx):
    flat_e = e_idx.T.reshape(-1)
    n_assign = flat_e.shape[0]
    onehot = (flat_e[:, None] == jnp.arange(N_EXPERTS, dtype=jnp.int32)[None, :]).astype(jnp.int32)
    csum = jnp.cumsum(onehot, axis=0)
    sizes = csum[-1]
    padded = (sizes + MOE_ROWS - 1) // MOE_ROWS * MOE_ROWS
    ends = jnp.cumsum(padded)
    pstart = ends - padded
    dest = jnp.sum(onehot * (csum - 1 + pstart[None, :]), axis=1).astype(jnp.int32)
    cap = n_assign + N_EXPERTS * MOE_ROWS
    blk_start = jnp.arange(cap // MOE_ROWS, dtype=jnp.int32) * MOE_ROWS
    blk_e = jnp.sum((ends[None, :] <= blk_start[:, None]).astype(jnp.int32), axis=1)
    blk_e = jnp.minimum(blk_e, N_EXPERTS - 1)
    real_end = jnp.where(blk_start < ends[-1], (pstart + sizes)[blk_e], 0)
    zero_blocks = (real_end - blk_start < MOE_ROWS).astype(jnp.int32)
    return dest, blk_e, zero_blocks, cap


def _final_kernel(dest_ref, dest_next_ref, x_ref, gt_ref, mod_ref, fg_ref, y_ref, o_ref, y_buf, sems):
    tm = x_ref.shape[0]
    i = pl.program_id(0)
    slot = lax.rem(i, 2)
    half = TOP_K * tm

    def fetch(idx_ref, into):
        def issue(r, carry):
            for k in range(TOP_K):
                _row_copy(y_ref, y_buf, idx_ref[0, 0, TOP_K * r + k], into * half + k * tm + r, sems.at[into]).start()
            return carry

        lax.fori_loop(0, tm, issue, 0, unroll=8)

    @pl.when(i == 0)
    def _():
        fetch(dest_ref, slot)

    @pl.when(i + 1 < pl.num_programs(0))
    def _():
        fetch(dest_next_ref, 1 - slot)

    _rows_copy(y_ref, y_buf, half, sems.at[slot]).wait()
    gt = gt_ref[...]
    moe = (gt[:, 0:1] * _load_rows(y_buf, tm, slot * half) + gt[:, 1:2] * _load_rows(y_buf, tm, slot * half + tm))
    x = x_ref[...] + mod_ref[0, 5:6, :] * moe
    o_ref[...] = _rms(x, fg_ref[...])


def _final_layer(x, ys, dest, gates_t, mods, final_g, seq):
    t, d = x.shape
    tm = ROW_TILE
    per_batch = seq // tm
    n_steps = t // tm
    dest_tiles = dest.reshape(n_steps, 1, TOP_K * tm)
    return pl.pallas_call(
        _final_kernel,
        grid=(n_steps,),
        in_specs=[
            _dest_spec(tm),
            pl.BlockSpec((1, 1, TOP_K * tm), lambda i: (jnp.minimum(i + 1, n_steps - 1), 0, 0),
                         memory_space=pltpu.SMEM),
            pl.BlockSpec((tm, d), lambda i: (i, 0)),
            pl.BlockSpec((tm, TOP_K), lambda i: (i, 0)),
            pl.BlockSpec((1, N_MOD, d), lambda i: (i // per_batch, 0, 0)),
            pl.BlockSpec((1, d), lambda i: (0, 0)),
            pl.BlockSpec(memory_space=pl.ANY),
        ],
        out_specs=pl.BlockSpec((tm, d), lambda i: (i, 0)),
        out_shape=jax.ShapeDtypeStruct((t, d), F32),
        scratch_shapes=[pltpu.VMEM((2 * TOP_K * tm * ROW_SLABS, LANES), F32), pltpu.SemaphoreType.DMA((2,))],
        compiler_params=_params(("arbitrary",), 48),
        name="moe_sum_final_norm",
    )(dest_tiles, dest_tiles, x, gates_t, mods, final_g.reshape(1, d), ys)


def kernel(x, c, ctx, c_ctx, ada_w, ada_b, norm_g, pool_w, pool_scale, ffn_w1, ffn_w3, ffn_w2, w_qkv, w_o,
           q_norm_g, k_norm_g, router_w, moe_w1, moe_w3, moe_w2, final_g):
    b, seq, d = x.shape
    assert ada_w.shape[0] == 2 and b <= 7, "written for the two-layer block (pool+dense, attention+experts)"

    cvecs = jnp.zeros((8, d), F32).at[:b].set(c).at[b].set(c_ctx)
    mods = _adaln(cvecs, ada_w, ada_b).reshape(2, 8, N_MOD, d)
    mods_x = mods[:, :b]
    mods_c = jnp.broadcast_to(mods[:, b:b + 1], (2, b, N_MOD, d))

    pool_wb = pool_w[0].astype(BF16)
    w1b, w3b, w2b = ffn_w1[0].astype(BF16), ffn_w3[0].astype(BF16), ffn_w2[0].astype(BF16)
    x = _pool_layer(x, mods_x[0], norm_g[0, 0], pool_wb, pool_scale[0])
    ctx = _pool_layer(ctx, mods_c[0], norm_g[0, 0], pool_wb, pool_scale[0])
    x = _ffn_dense_layer(x, mods_x[0], norm_g[0, 1], w1b, w3b, w2b)
    ctx = _ffn_dense_layer(ctx, mods_c[0], norm_g[0, 1], w1b, w3b, w2b)

    wqkv = w_qkv[0].astype(BF16)
    q, kx, vx = _qkv_layer(x, mods_x[1], norm_g[1, 0], wqkv, q_norm_g[0], k_norm_g[0], _rope_tables(seq))
    _, kc, vc = _qkv_layer(ctx, mods_c[1], norm_g[1, 0], wqkv[:, Q_DIM:], q_norm_g[0], k_norm_g[0], None)
    k_all = jnp.concatenate([kx, kc], axis=1)
    v_all = jnp.concatenate([vx, vc], axis=1)
    lk = k_all.shape[1]
    n_kblk = lk // ATT_TK
    vt = v_all.reshape(b, n_kblk, ATT_TK, N_KV_HEADS, HEAD_DIM).transpose(0, 3, 1, 4, 2)
    attn_t = _attention(q, k_all, vt)

    assert ATT_TQ == ROW_TILE, "the output projection reads one transposed attention tile per row tile"
    x3, hf, e_idx, gates = _oproj_router_layer(attn_t, x, mods_x[1], norm_g[1, 1], w_o[0].astype(BF16), router_w[0])
    dest, blk_e, zero_blocks, cap = _routing_plan(e_idx)
    xs = _dispatch(hf, dest, zero_blocks, cap)
    ys = _ffn_experts(xs, blk_e, moe_w1[0].astype(BF16), moe_w3[0].astype(BF16), moe_w2[0].astype(BF16))
    out = _final_layer(x3, ys, dest, gates.T, mods_x[1], final_g, seq)
    return out.reshape(b, seq, d)
```

```python
import functools
import math

import jax
import jax.numpy as jnp
from jax import lax
from jax.experimental import pallas as pl
from jax.experimental.pallas import tpu as pltpu

F32 = jnp.float32
BF16 = jnp.bfloat16

D_MODEL = 1024
GRID_W = 64
POOL_WINDOWS = (2, 4, 8, 16)
POOL_GROUP_DIM = D_MODEL // len(POOL_WINDOWS)
HEAD_DIM = 64
N_KV_HEADS = 4
GROUP = 4
KV_SLAB = N_KV_HEADS * HEAD_DIM
Q_DIM = N_KV_HEADS * GROUP * HEAD_DIM
ROPE_THETA = 10000.0
Q_SCALE = HEAD_DIM ** -0.5 * math.log2(math.e)
D_FF = 2816
N_EXPERTS = 8
TOP_K = 2
NORM_EPS = 1e-6
N_MOD = 6

POOL_HALO = 16
ROW_TILE = 512
FF_TILE = 1408
MOE_ROWS = 512
DISPATCH_TILE = 2048
ATT_TQ = 512
ATT_TK = 1280
KEY_CHUNK = 256
DENOM_ROWS = 8
ATT_TILES = 8
MIB = 1024 * 1024
LANES = 128
ROW_SLABS = D_MODEL // LANES


def _dot(a, b):
    return jnp.dot(a, b, preferred_element_type=F32)


def _rms(x, g):
    ms = jnp.mean(x * x, axis=-1, keepdims=True)
    return x * lax.rsqrt(ms + NORM_EPS) * g


def _silu(x):
    return x * (1.0 / (1.0 + jnp.exp(-x)))


def _load_rows(ref, n_rows, first=0):
    return jnp.concatenate(
        [ref[pl.ds(ROW_SLABS * first + s, n_rows, stride=ROW_SLABS), :] for s in range(ROW_SLABS)], axis=1)


def _store_rows(ref, val):
    for s in range(ROW_SLABS):
        ref[pl.ds(s, val.shape[0], stride=ROW_SLABS), :] = val[:, s * LANES:(s + 1) * LANES]


def _tile_of_row(ref, row):
    return ref.at[pl.ds(pl.multiple_of(row * ROW_SLABS, ROW_SLABS), ROW_SLABS)]


def _params(semantics, vmem_mib):
    return pltpu.CompilerParams(dimension_semantics=semantics, vmem_limit_bytes=vmem_mib * MIB)


def _adaln_kernel(c_ref, w_ref, b_ref, o_ref):
    s = _silu(c_ref[...])
    o_ref[0] = jnp.dot(s, w_ref[0], preferred_element_type=F32, precision=lax.Precision.HIGHEST) + b_ref[0]


def _adaln(cvecs, ada_w, ada_b):
    depth, d, n = ada_w.shape
    tn = n // 4
    return pl.pallas_call(
        _adaln_kernel,
        grid=(depth, n // tn),
        in_specs=[
            pl.BlockSpec((8, d), lambda i, j: (0, 0)),
            pl.BlockSpec((1, d, tn), lambda i, j: (i, 0, j)),
            pl.BlockSpec((1, 1, tn), lambda i, j: (i, 0, j)),
        ],
        out_specs=pl.BlockSpec((1, 8, tn), lambda i, j: (i, 0, j)),
        out_shape=jax.ShapeDtypeStruct((depth, 8, n), F32),
        compiler_params=_params(("parallel", "parallel"), 32),
        name="adaln",
    )(cvecs, ada_w, ada_b.reshape(depth, 1, n))


def _pool_kernel(xp_ref, xc_ref, xn_ref, mod_ref, ng_ref, pw_ref, ps_ref, o_ref, *, seq, tm):
    t = pl.program_id(1)
    xc = xc_ref[0]
    xa = jnp.concatenate([xp_ref[0], xc, xn_ref[0]], axis=0)
    sh, sc, gate = mod_ref[0, 0:1, :], mod_ref[0, 1:2, :], mod_ref[0, 2:3, :]
    h = _rms(xa, ng_ref[...]) * (1.0 + sc) + sh
    hb = h.astype(BF16)
    wide = tm + 2 * POOL_HALO
    row = lax.broadcasted_iota(jnp.int32, (tm, wide), 0) + t * tm
    col = lax.broadcasted_iota(jnp.int32, (tm, wide), 1) + (t * tm - POOL_HALO)
    row1 = lax.broadcasted_iota(jnp.int32, (tm, 1), 0) + t * tm
    ys = []
    for g, w in enumerate(POOL_WINDOWS):
        half = w // 2
        lo = jnp.maximum(row - half, 0)
        hi = jnp.minimum(row + half, seq)
        band = jnp.where((col >= lo) & (col < hi), 1.0, 0.0).astype(BF16)
        cnt = (jnp.minimum(row1 + half, seq) - jnp.maximum(row1 - half, 0)).astype(F32)
        lanes = slice(g * POOL_GROUP_DIM, (g + 1) * POOL_GROUP_DIM)
        win_sum = _dot(band, hb[:, lanes])
        d = win_sum / cnt - h[POOL_HALO:POOL_HALO + tm, lanes]
        ys.append(_dot(d.astype(BF16), pw_ref[g]))
    y = jnp.concatenate(ys, axis=1) * ps_ref[...]
    o_ref[0] = xc + gate * y


def _pool_layer(x, mods, norm_g, pool_w, pool_scale):
    b, seq, d = x.shape
    tm = min(ROW_TILE, seq)
    per = tm // POOL_HALO
    last = seq // POOL_HALO - 1
    return pl.pallas_call(
        functools.partial(_pool_kernel, seq=seq, tm=tm),
        grid=(b, seq // tm),
        in_specs=[
            pl.BlockSpec((1, POOL_HALO, d), lambda i, t: (i, jnp.maximum(t * per - 1, 0), 0)),
            pl.BlockSpec((1, tm, d), lambda i, t: (i, t, 0)),
            pl.BlockSpec((1, POOL_HALO, d), lambda i, t: (i, jnp.minimum((t + 1) * per, last), 0)),
            pl.BlockSpec((1, N_MOD, d), lambda i, t: (i, 0, 0)),
            pl.BlockSpec((1, d), lambda i, t: (0, 0)),
            pl.BlockSpec(pool_w.shape, lambda i, t: (0, 0, 0)),
            pl.BlockSpec((1, d), lambda i, t: (0, 0)),
        ],
        out_specs=pl.BlockSpec((1, tm, d), lambda i, t: (i, t, 0)),
        out_shape=jax.ShapeDtypeStruct(x.shape, F32),
        compiler_params=_params(("parallel", "parallel"), 48),
        name="pool_mix",
    )(x, x, x, mods, norm_g.reshape(1, d), pool_w, pool_scale.reshape(1, d))


def _swiglu(h, w1_ref, w3_ref, w2_ref):
    acc = None
    for f in range(D_FF // FF_TILE):
        cols = slice(f * FF_TILE, (f + 1) * FF_TILE)
        u = _silu(_dot(h, w1_ref[:, cols])) * _dot(h, w3_ref[:, cols])
        part = _dot(u.astype(BF16), w2_ref[cols, :])
        acc = part if acc is None else acc + part
    return acc


def _ffn_dense_kernel(x_ref, mod_ref, ng_ref, w1_ref, w3_ref, w2_ref, o_ref):
    sh, sc, gate = mod_ref[0, 3:4, :], mod_ref[0, 4:5, :], mod_ref[0, 5:6, :]
    x = x_ref[...]
    h = (_rms(x, ng_ref[...]) * (1.0 + sc) + sh).astype(BF16)
    o_ref[...] = x + gate * _swiglu(h, w1_ref, w3_ref, w2_ref)


def _ffn_dense_layer(x, mods, norm_g, w1, w3, w2):
    b, seq, d = x.shape
    t = b * seq
    tm = min(ROW_TILE, seq)
    per_batch = seq // tm
    fixed = lambda i: (0, 0)
    out = pl.pallas_call(
        _ffn_dense_kernel,
        grid=(t // tm,),
        in_specs=[
            pl.BlockSpec((tm, d), lambda i: (i, 0)),
            pl.BlockSpec((1, N_MOD, d), lambda i: (i // per_batch, 0, 0)),
            pl.BlockSpec((1, d), fixed),
            pl.BlockSpec((d, D_FF), fixed),
            pl.BlockSpec((d, D_FF), fixed),
            pl.BlockSpec((D_FF, d), fixed),
        ],
        out_specs=pl.BlockSpec((tm, d), lambda i: (i, 0)),
        out_shape=jax.ShapeDtypeStruct((t, d), F32),
        compiler_params=_params(("parallel",), 60),
        name="ffn_dense",
    )(x.reshape(t, d), mods, norm_g.reshape(1, d), w1, w3, w2)
    return out.reshape(b, seq, d)


def _head_norm_rope(z, ones_blk, gain, cos, sin):
    sq = z * z
    hi = sq.astype(BF16)
    lo = (sq - hi.astype(F32)).astype(BF16)
    ss = _dot(hi, ones_blk) + _dot(lo, ones_blk)
    zn = z * lax.rsqrt(ss * (1.0 / HEAD_DIM) + NORM_EPS) * gain
    if cos is None:
        return zn
    width = z.shape[1]
    fwd = pltpu.roll(zn, width - 16, 1)
    bwd = pltpu.roll(zn, 16, 1)
    lane = lax.broadcasted_iota(jnp.int32, zn.shape, 1)
    partner = jnp.where((lane & 31) < 16, fwd, bwd)
    return zn * cos + partner * sin


def _qkv_kernel(*refs, n_q, rope):
    x_ref, mod_ref, ng_ref, w_ref, e_ref, qg_ref, kg_ref = refs[:7]
    refs = refs[7:]
    if rope:
        cos_ref, sin_ref = refs[:2]
        refs = refs[2:]
        cos, sin = cos_ref[...], sin_ref[...]
    else:
        cos = sin = None
    sh, sc = mod_ref[0, 0:1, :], mod_ref[0, 1:2, :]
    hb = (_rms(x_ref[0], ng_ref[...]) * (1.0 + sc) + sh).astype(BF16)
    qkv = _dot(hb, w_ref[...])
    ones_blk = e_ref[...]
    if n_q:
        q_ref, k_ref, v_ref = refs
        for c in range(n_q):
            z = _head_norm_rope(qkv[:, c * KV_SLAB:(c + 1) * KV_SLAB], ones_blk, qg_ref[...], cos, sin)
            q_ref[0, :, c * KV_SLAB:(c + 1) * KV_SLAB] = (z * Q_SCALE).astype(BF16)
    else:
        k_ref, v_ref = refs
    base = n_q * KV_SLAB
    k_ref[0] = _head_norm_rope(qkv[:, base:base + KV_SLAB], ones_blk, kg_ref[...], cos, sin).astype(BF16)
    v_ref[0] = qkv[:, base + KV_SLAB:base + 2 * KV_SLAB].astype(BF16)


def _qkv_layer(x, mods, norm_g, w, q_gain, k_gain, tables):
    b, seq, d = x.shape
    tm = min(ROW_TILE, seq)
    width = w.shape[1]
    n_q = (width - 2 * KV_SLAB) // KV_SLAB
    rope = tables is not None
    seg = lax.broadcasted_iota(jnp.int32, (KV_SLAB, KV_SLAB), 0) // HEAD_DIM
    ones_blk = (seg == seg.T).astype(BF16)
    in_specs = [
        pl.BlockSpec((1, tm, d), lambda i, t: (i, t, 0)),
        pl.BlockSpec((1, N_MOD, d), lambda i, t: (i, 0, 0)),
        pl.BlockSpec((1, d), lambda i, t: (0, 0)),
        pl.BlockSpec((d, width), lambda i, t: (0, 0)),
        pl.BlockSpec((KV_SLAB, KV_SLAB), lambda i, t: (0, 0)),
        pl.BlockSpec((1, KV_SLAB), lambda i, t: (0, 0)),
        pl.BlockSpec((1, KV_SLAB), lambda i, t: (0, 0)),
    ]
    args = [x, mods, norm_g.reshape(1, d), w, ones_blk,
            jnp.tile(q_gain, N_KV_HEADS).reshape(1, KV_SLAB), jnp.tile(k_gain, N_KV_HEADS).reshape(1, KV_SLAB)]
    if rope:
        in_specs += [pl.BlockSpec((tm, KV_SLAB), lambda i, t: (t, 0))] * 2
        args += list(tables)
    kv_spec = pl.BlockSpec((1, tm, KV_SLAB), lambda i, t: (i, t, 0))
    kv_shape = jax.ShapeDtypeStruct((b, seq, KV_SLAB), BF16)
    out_specs, out_shape = [kv_spec, kv_spec], [kv_shape, kv_shape]
    if n_q:
        out_specs = [pl.BlockSpec((1, tm, n_q * KV_SLAB), lambda i, t: (i, t, 0))] + out_specs
        out_shape = [jax.ShapeDtypeStruct((b, seq, n_q * KV_SLAB), BF16)] + out_shape
    outs = pl.pallas_call(
        functools.partial(_qkv_kernel, n_q=n_q, rope=rope),
        grid=(b, seq // tm),
        in_specs=in_specs,
        out_specs=out_specs,
        out_shape=out_shape,
        compiler_params=_params(("parallel", "parallel"), 48),
        name="qkv_rope" if rope else "kv_ctx",
    )(*args)
    return outs if n_q else [None] + list(outs)


def _rope_tables(seq):
    half = HEAD_DIM // 4
    t = jnp.arange(seq, dtype=jnp.int32)
    pos = jnp.stack([t // GRID_W, t % GRID_W], axis=1).astype(F32)
    freqs = ROPE_THETA ** (-jnp.arange(half, dtype=F32) / half)
    ang = pos[:, :, None] * freqs[None, None, :]
    cos = jnp.cos(ang)
    sin = jnp.sin(ang)
    cos_h = jnp.concatenate([cos, cos], axis=2).reshape(seq, HEAD_DIM)
    sin_h = jnp.concatenate([-sin, sin], axis=2).reshape(seq, HEAD_DIM)
    return jnp.tile(cos_h, (1, N_KV_HEADS)), jnp.tile(sin_h, (1, N_KV_HEADS))


def _attn_kernel(q_ref, k_ref, vt_ref, o_ref, qt_scr, m_scr, acc_scr, sa_scr, sb_scr, smaxa_scr, smaxb_scr, *,
                 tq, tk, n_kblk, n_tiles):
    odd_head = lax.rem(pl.program_id(1), 2) == 1

    def load_queries(t):
        rows = pl.ds(pl.multiple_of(t * tq, tq), tq)
        for j in range(GROUP // 2):
            pair_t = q_ref[0, rows, j * 2 * HEAD_DIM:(j + 1) * 2 * HEAD_DIM].astype(F32).T
            for h in range(2):
                qt = pair_t[h * HEAD_DIM:(h + 1) * HEAD_DIM, :]
                zero = jnp.zeros_like(qt)
                low, high = jnp.concatenate([qt, zero], axis=0), jnp.concatenate([zero, qt], axis=0)
                qt_scr[2 * j + h] = jnp.where(odd_head, high, low).astype(BF16)

    def reset_state():
        m_scr[...] = jnp.full_like(m_scr, -jnp.inf)
        acc_scr[...] = jnp.zeros_like(acc_scr)

    def write_output(t):
        for g in range(GROUP):
            acc = acc_scr[g]
            out = acc[:HEAD_DIM] / jnp.sum(acc[HEAD_DIM:], axis=0, keepdims=True)
            o_ref[0, t, g * HEAD_DIM:(g + 1) * HEAD_DIM, :] = out.astype(BF16)

    n_chunks = tk // KEY_CHUNK

    def scores_chunk(g, k, c, buf):
        keys = k_ref[0, pl.ds(pl.multiple_of(k * tk + c * KEY_CHUNK, KEY_CHUNK), KEY_CHUNK), :]
        s = _dot(keys, qt_scr[g])
        buf[0][g, c * KEY_CHUNK:(c + 1) * KEY_CHUNK, :] = s
        return jnp.max(s, axis=0, keepdims=True)

    def values_chunk(g, k, c, buf, m_next):
        p = jnp.exp2(buf[0][g, c * KEY_CHUNK:(c + 1) * KEY_CHUNK, :] - m_next[0:1, :])
        psum = jnp.sum(p.reshape(KEY_CHUNK // DENOM_ROWS, DENOM_ROWS, tq), axis=0)
        pv = _dot(vt_ref[0, 0, k, :, c * KEY_CHUNK:(c + 1) * KEY_CHUNK], p.astype(BF16))
        return jnp.concatenate([pv, psum], axis=0)

    def scores(k, buf):
        for g in range(GROUP):
            smax = None
            for c in range(n_chunks):
                cmax = scores_chunk(g, k, c, buf)
                smax = cmax if smax is None else jnp.maximum(smax, cmax)
            buf[1][g] = jnp.broadcast_to(smax, (8, tq))

    def scores_and_accumulate(k_new, buf_new, k_old, buf_old):
        for g in range(GROUP):
            m_prev = m_scr[g]
            m_next = jnp.maximum(m_prev, buf_old[1][g])
            alpha = jnp.exp2(m_prev - m_next)
            smax = pv = None
            for c in range(n_chunks):
                cmax = scores_chunk(g, k_new, c, buf_new)
                part = values_chunk(g, k_old, c, buf_old, m_next)
                smax = cmax if smax is None else jnp.maximum(smax, cmax)
                pv = part if pv is None else pv + part
            buf_new[1][g] = jnp.broadcast_to(smax, (8, tq))
            acc_scr[g] = alpha[0:1, :] * acc_scr[g] + pv
            m_scr[g] = m_next

    def query_tile(t, first, second):
        def key_block_pair(j, carry):
            k = 2 * j
            scores_and_accumulate(k + 1, second, k, first)
            scores_and_accumulate(k + 2, first, k + 1, second)
            return carry

        lax.fori_loop(0, (n_kblk - 1) // 2, key_block_pair, 0)
        load_queries(jnp.minimum(t + 1, n_tiles - 1))
        scores_and_accumulate(0, second, n_kblk - 1, first)
        write_output(t)
        reset_state()

    buf_a, buf_b = (sa_scr, smaxa_scr), (sb_scr, smaxb_scr)
    load_queries(0)
    reset_state()
    scores(0, buf_a)

    def query_tile_pair(i, carry):
        query_tile(2 * i, buf_a, buf_b)
        query_tile(2 * i + 1, buf_b, buf_a)
        return carry

    lax.fori_loop(0, n_tiles // 2, query_tile_pair, 0)


def _attention(q, k, vt_ext):
    b, seq, _ = q.shape
    n_kblk, v_rows, tk = vt_ext.shape[2:]
    lk = k.shape[1]
    tq, n_tiles = ATT_TQ, ATT_TILES
    assert n_kblk % 2 == 1 and n_tiles % 2 == 0 and seq % (tq * n_tiles) == 0 and lk == n_kblk * tk
    stat_rows = 8
    return pl.pallas_call(
        functools.partial(_attn_kernel, tq=tq, tk=tk, n_kblk=n_kblk, n_tiles=n_tiles),
        grid=(b, N_KV_HEADS, seq // (tq * n_tiles)),
        in_specs=[
            pl.BlockSpec((1, tq * n_tiles, KV_SLAB), lambda i, h, qi: (i, qi, h)),
            pl.BlockSpec((1, lk, 2 * HEAD_DIM), lambda i, h, qi: (i, 0, h // 2)),
            pl.BlockSpec((1, 1, n_kblk, v_rows, tk), lambda i, h, qi: (i, h, 0, 0, 0)),
        ],
        out_specs=pl.BlockSpec((1, n_tiles, KV_SLAB, tq), lambda i, h, qi: (i, qi, h, 0)),
        out_shape=jax.ShapeDtypeStruct((b, seq // tq, Q_DIM, tq), BF16),
        scratch_shapes=[
            pltpu.VMEM((GROUP, 2 * HEAD_DIM, tq), BF16),
            pltpu.VMEM((GROUP, stat_rows, tq), F32),
            pltpu.VMEM((GROUP, v_rows + DENOM_ROWS, tq), F32),
            pltpu.VMEM((GROUP, tk, tq), F32),
            pltpu.VMEM((GROUP, tk, tq), F32),
            pltpu.VMEM((GROUP, stat_rows, tq), F32),
            pltpu.VMEM((GROUP, stat_rows, tq), F32),
        ],
        compiler_params=_params(("parallel", "parallel", "parallel"), 60),
        name="gqa_attention",
    )(q, k, vt_ext)


def _oproj_router_kernel(a_ref, x_ref, mod_ref, ng_ref, wo_ref, rw_ref, x_out, h_out, e_out, g_out, r_out, n_out,
                         count_scr):
    @pl.when(pl.program_id(0) == 0)
    def _():
        count_scr[...] = jnp.zeros_like(count_scr)

    g1 = mod_ref[0, 2:3, :]
    sh, sc = mod_ref[0, 3:4, :], mod_ref[0, 4:5, :]
    proj = lax.dot_general(a_ref[0], wo_ref[...], (((0,), (0,)), ((), ())), preferred_element_type=F32)
    x = x_ref[...] + g1 * proj
    x_out[...] = x
    h = _rms(x, ng_ref[...]) * (1.0 + sc) + sh
    _store_rows(h_out, h)
    logits = lax.dot_general(rw_ref[...], h, (((1,), (1,)), ((), ())),
                             preferred_element_type=F32, precision=lax.Precision.HIGHEST)
    eid = lax.broadcasted_iota(jnp.int32, logits.shape, 0)
    m1 = jnp.max(logits, axis=0, keepdims=True)
    i1 = jnp.min(jnp.where(logits == m1, eid, N_EXPERTS), axis=0, keepdims=True)
    rest = jnp.where(eid == i1, -jnp.inf, logits)
    m2 = jnp.max(rest, axis=0, keepdims=True)
    i2 = jnp.min(jnp.where(rest == m2, eid, N_EXPERTS), axis=0, keepdims=True)
    e = jnp.exp(m2 - m1)
    e_out[...] = jnp.concatenate([i1, i2], axis=0)
    g_out[...] = jnp.concatenate([1.0 / (1.0 + e), e / (1.0 + e)], axis=0)
    tm = logits.shape[1]
    chose = jnp.where((eid == i1) | (eid == i2), 1.0, 0.0)
    earlier = (lax.broadcasted_iota(jnp.int32, (tm, tm), 0) < lax.broadcasted_iota(jnp.int32, (tm, tm), 1))
    before = _dot(chose.astype(BF16), jnp.where(earlier, 1.0, 0.0).astype(BF16)) + count_scr[:, 0:1]
    r1 = jnp.sum(jnp.where(eid == i1, before, 0.0), axis=0, keepdims=True)
    r2 = jnp.sum(jnp.where(eid == i2, before, 0.0), axis=0, keepdims=True)
    r_out[...] = jnp.concatenate([r1, r2], axis=0).astype(jnp.int32)
    count_scr[...] += jnp.sum(chose, axis=1, keepdims=True)
    n_out[...] = count_scr[...].astype(jnp.int32)


def _oproj_router_layer(attn_t, x, mods, norm_g, w_o, router_w):
    b, seq, d = x.shape
    t = b * seq
    tm = ROW_TILE
    per_batch = seq // tm
    row = lambda i: (i, 0)
    fixed = lambda i: (0, 0)
    return pl.pallas_call(
        _oproj_router_kernel,
        grid=(t // tm,),
        in_specs=[
            pl.BlockSpec((1, d, tm), lambda i: (i, 0, 0)),
            pl.BlockSpec((tm, d), row),
            pl.BlockSpec((1, N_MOD, d), lambda i: (i // per_batch, 0, 0)),
            pl.BlockSpec((1, d), fixed),
            pl.BlockSpec((d, d), fixed),
            pl.BlockSpec((N_EXPERTS, d), fixed),
        ],
        out_specs=[
            pl.BlockSpec((tm, d), row),
            pl.BlockSpec((tm * ROW_SLABS, LANES), row),
            pl.BlockSpec((TOP_K, tm), lambda i: (0, i)),
            pl.BlockSpec((TOP_K, tm), lambda i: (0, i)),
            pl.BlockSpec((TOP_K, tm), lambda i: (0, i)),
            pl.BlockSpec((N_EXPERTS, LANES), fixed),
        ],
        out_shape=[
            jax.ShapeDtypeStruct((t, d), F32),
            jax.ShapeDtypeStruct((t * ROW_SLABS, LANES), F32),
            jax.ShapeDtypeStruct((TOP_K, t), jnp.int32),
            jax.ShapeDtypeStruct((TOP_K, t), F32),
            jax.ShapeDtypeStruct((TOP_K, t), jnp.int32),
            jax.ShapeDtypeStruct((N_EXPERTS, LANES), jnp.int32),
        ],
        scratch_shapes=[pltpu.VMEM((N_EXPERTS, LANES), F32)],
        compiler_params=_params(("arbitrary",), 48),
        name="oproj_router",
    )(attn_t.reshape(t // tm, d, tm), x.reshape(t, d), mods, norm_g.reshape(1, d), w_o, router_w.T)


def _row_copy(src_ref, dst_ref, src_row, dst_row, sem):
    return pltpu.make_async_copy(_tile_of_row(src_ref, src_row), _tile_of_row(dst_ref, dst_row), sem)


def _rows_copy(src_ref, dst_ref, rows, sem):
    return pltpu.make_async_copy(src_ref.at[pl.ds(0, rows * ROW_SLABS)], dst_ref.at[pl.ds(0, rows * ROW_SLABS)], sem)


def _dest_spec(tm):
    return pl.BlockSpec((1, 1, TOP_K * tm), lambda i: (i, 0, 0), memory_space=pltpu.SMEM)


def _dispatch_kernel(dest_ref, zero_ref, h_ref, xs_ref, zeros_scr, sem):
    tm = h_ref.shape[0] // ROW_SLABS

    @pl.when(pl.program_id(0) == 0)
    def _():
        zeros_scr[...] = jnp.zeros_like(zeros_scr)
        block_rows = MOE_ROWS * ROW_SLABS

        def zero_block(j):
            return pltpu.make_async_copy(
                zeros_scr, xs_ref.at[pl.ds(pl.multiple_of(j * block_rows, block_rows), block_rows)], sem)

        def start(j, carry):
            @pl.when(zero_ref[j] != 0)
            def _():
                zero_block(j).start()
            return carry

        def drain(j, carry):
            @pl.when(zero_ref[j] != 0)
            def _():
                zero_block(j).wait()
            return carry

        lax.fori_loop(0, zero_ref.shape[0], start, 0)
        lax.fori_loop(0, zero_ref.shape[0], drain, 0)

    def issue(r, carry):
        for k in range(TOP_K):
            _row_copy(h_ref, xs_ref, r, dest_ref[0, 0, TOP_K * r + k], sem).start(priority=k)
        return carry

    lax.fori_loop(0, tm, issue, 0)
    for _ in range(TOP_K):
        _rows_copy(h_ref, xs_ref, tm, sem).wait()


def _dispatch(h, dest, zero_blocks, cap):
    t = h.shape[0] // ROW_SLABS
    tm = DISPATCH_TILE
    return pl.pallas_call(
        _dispatch_kernel,
        grid=(t // tm,),
        in_specs=[
            _dest_spec(tm),
            pl.BlockSpec(memory_space=pltpu.SMEM),
            pl.BlockSpec((tm * ROW_SLABS, LANES), lambda i: (i, 0)),
        ],
        out_specs=pl.BlockSpec(memory_space=pl.ANY),
        out_shape=jax.ShapeDtypeStruct((cap * ROW_SLABS, LANES), h.dtype),
        scratch_shapes=[pltpu.VMEM((MOE_ROWS * ROW_SLABS, LANES), h.dtype), pltpu.SemaphoreType.DMA(())],
        compiler_params=_params(("arbitrary",), 32),
        name="moe_dispatch",
    )(dest.reshape(t // tm, 1, TOP_K * tm), zero_blocks, h)


def _ffn_expert_kernel(blk_ref, x_ref, w1_ref, w3_ref, w2_ref, o_ref):
    in_use = pl.program_id(0) < blk_ref[pl.num_programs(0)]

    @pl.when(in_use)
    def _():
        h = _load_rows(x_ref, MOE_ROWS).astype(BF16)
        _store_rows(o_ref, _swiglu(h, w1_ref.at[0], w3_ref.at[0], w2_ref.at[0]))

    @pl.when(jnp.logical_not(in_use))
    def _():
        o_ref[...] = jnp.zeros_like(o_ref)


def _ffn_experts(xs, blk_e, w1, w3, w2):
    cap = xs.shape[0] // ROW_SLABS
    d = D_MODEL
    n_blk = cap // MOE_ROWS
    row_spec = pl.BlockSpec((MOE_ROWS * ROW_SLABS, LANES), lambda i, e: (i, 0))
    grid_spec = pltpu.PrefetchScalarGridSpec(
        num_scalar_prefetch=1,
        grid=(n_blk,),
        in_specs=[
            row_spec,
            pl.BlockSpec((1, d, D_FF), lambda i, e: (e[i], 0, 0)),
            pl.BlockSpec((1, d, D_FF), lambda i, e: (e[i], 0, 0)),
            pl.BlockSpec((1, D_FF, d), lambda i, e: (e[i], 0, 0)),
        ],
        out_specs=row_spec,
    )
    return pl.pallas_call(
        _ffn_expert_kernel,
        grid_spec=grid_spec,
        out_shape=jax.ShapeDtypeStruct(xs.shape, F32),
        compiler_params=_params(("arbitrary",), 60),
        name="ffn_experts",
    )(blk_e, xs, w1, w3, w2)


def _routing_plan(e_idx, ranks, sizes):
    n_assign = e_idx.size
    padded = (sizes + MOE_ROWS - 1) // MOE_ROWS * MOE_ROWS
    ends = jnp.cumsum(padded)
    pstart = ends - padded
    dest = (ranks + pstart[e_idx]).T.reshape(-1).astype(jnp.int32)
    cap = n_assign + N_EXPERTS * MOE_ROWS
    blk_start = jnp.arange(cap // MOE_ROWS, dtype=jnp.int32) * MOE_ROWS
    blk_e = jnp.sum((ends[None, :] <= blk_start[:, None]).astype(jnp.int32), axis=1)
    blk_e = jnp.minimum(blk_e, N_EXPERTS - 1)
    real_end = jnp.where(blk_start < ends[-1], (pstart + sizes)[blk_e], 0)
    zero_blocks = (real_end - blk_start < MOE_ROWS).astype(jnp.int32)
    blk_info = jnp.concatenate([blk_e, ends[-1:] // MOE_ROWS]).astype(jnp.int32)
    return dest, blk_info, zero_blocks, cap


def _final_kernel(dest_ref, dest_next_ref, x_ref, gt_ref, mod_ref, fg_ref, y_ref, o_ref, y_buf, sems):
    tm = x_ref.shape[0]
    i = pl.program_id(0)
    slot = lax.rem(i, 2)
    half = TOP_K * tm

    def fetch(idx_ref, into):
        def issue(r, carry):
            for k in range(TOP_K):
                _row_copy(y_ref, y_buf, idx_ref[0, 0, TOP_K * r + k], into * half + k * tm + r,
                          sems.at[into]).start(priority=k)
            return carry

        lax.fori_loop(0, tm, issue, 0, unroll=8)

    @pl.when(i == 0)
    def _():
        fetch(dest_ref, slot)

    @pl.when(i + 1 < pl.num_programs(0))
    def _():
        fetch(dest_next_ref, 1 - slot)

    _rows_copy(y_ref, y_buf, half, sems.at[slot]).wait()
    gt = gt_ref[...]
    moe = (gt[:, 0:1] * _load_rows(y_buf, tm, slot * half) + gt[:, 1:2] * _load_rows(y_buf, tm, slot * half + tm))
    x = x_ref[...] + mod_ref[0, 5:6, :] * moe
    o_ref[...] = _rms(x, fg_ref[...])


def _final_layer(x, ys, dest, gates_t, mods, final_g, seq):
    t, d = x.shape
    tm = ROW_TILE
    per_batch = seq // tm
    n_steps = t // tm
    dest_tiles = dest.reshape(n_steps, 1, TOP_K * tm)
    return pl.pallas_call(
        _final_kernel,
        grid=(n_steps,),
        in_specs=[
            _dest_spec(tm),
            pl.BlockSpec((1, 1, TOP_K * tm), lambda i: (jnp.minimum(i + 1, n_steps - 1), 0, 0),
                         memory_space=pltpu.SMEM),
            pl.BlockSpec((tm, d), lambda i: (i, 0)),
            pl.BlockSpec((tm, TOP_K), lambda i: (i, 0)),
            pl.BlockSpec((1, N_MOD, d), lambda i: (i // per_batch, 0, 0)),
            pl.BlockSpec((1, d), lambda i: (0, 0)),
            pl.BlockSpec(memory_space=pl.ANY),
        ],
        out_specs=pl.BlockSpec((tm, d), lambda i: (i, 0)),
        out_shape=jax.ShapeDtypeStruct((t, d), F32),
        scratch_shapes=[pltpu.VMEM((2 * TOP_K * tm * ROW_SLABS, LANES), F32), pltpu.SemaphoreType.DMA((2,))],
        compiler_params=_params(("arbitrary",), 48),
        name="moe_sum_final_norm",
    )(dest_tiles, dest_tiles, x, gates_t, mods, final_g.reshape(1, d), ys)


def kernel(x, c, ctx, c_ctx, ada_w, ada_b, norm_g, pool_w, pool_scale, ffn_w1, ffn_w3, ffn_w2, w_qkv, w_o,
           q_norm_g, k_norm_g, router_w, moe_w1, moe_w3, moe_w2, final_g):
    b, seq, d = x.shape
    assert ada_w.shape[0] == 2 and b <= 7, "written for the two-layer block (pool+dense, attention+experts)"

    cvecs = jnp.zeros((8, d), F32).at[:b].set(c).at[b].set(c_ctx)
    mods = _adaln(cvecs, ada_w, ada_b).reshape(2, 8, N_MOD, d)
    mods_x = mods[:, :b]
    mods_c = jnp.broadcast_to(mods[:, b:b + 1], (2, b, N_MOD, d))

    pool_wb = pool_w[0].astype(BF16)
    w1b, w3b, w2b = ffn_w1[0].astype(BF16), ffn_w3[0].astype(BF16), ffn_w2[0].astype(BF16)
    x = _pool_layer(x, mods_x[0], norm_g[0, 0], pool_wb, pool_scale[0])
    ctx = _pool_layer(ctx, mods_c[0], norm_g[0, 0], pool_wb, pool_scale[0])
    x = _ffn_dense_layer(x, mods_x[0], norm_g[0, 1], w1b, w3b, w2b)
    ctx = _ffn_dense_layer(ctx, mods_c[0], norm_g[0, 1], w1b, w3b, w2b)

    wqkv = w_qkv[0].astype(BF16)
    q, kx, vx = _qkv_layer(x, mods_x[1], norm_g[1, 0], wqkv, q_norm_g[0], k_norm_g[0], _rope_tables(seq))
    _, kc, vc = _qkv_layer(ctx, mods_c[1], norm_g[1, 0], wqkv[:, Q_DIM:], q_norm_g[0], k_norm_g[0], None)
    k_all = jnp.concatenate([kx, kc], axis=1)
    v_all = jnp.concatenate([vx, vc], axis=1)
    lk = k_all.shape[1]
    n_kblk = lk // ATT_TK
    vt = v_all.reshape(b, n_kblk, ATT_TK, N_KV_HEADS, HEAD_DIM).transpose(0, 3, 1, 4, 2)
    attn_t = _attention(q, k_all, vt)

    assert ATT_TQ == ROW_TILE, "the output projection reads one transposed attention tile per row tile"
    x3, hf, e_idx, gates, ranks, counts = _oproj_router_layer(
        attn_t, x, mods_x[1], norm_g[1, 1], w_o[0].astype(BF16), router_w[0])
    dest, blk_e, zero_blocks, cap = _routing_plan(e_idx, ranks, counts[:, 0])
    xs = _dispatch(hf, dest, zero_blocks, cap)
    ys = _ffn_experts(xs, blk_e, moe_w1[0].astype(BF16), moe_w3[0].astype(BF16), moe_w2[0].astype(BF16))
    out = _final_layer(x3, ys, dest, gates.T, mods_x[1], final_g, seq)
    return out.reshape(b, seq, d)
```

```python
import functools
import math

import jax
import jax.numpy as jnp
from jax import lax
from jax.experimental import pallas as pl
from jax.experimental.pallas import tpu as pltpu

F32 = jnp.float32
BF16 = jnp.bfloat16

D_MODEL = 1024
GRID_W = 64
POOL_WINDOWS = (2, 4, 8, 16)
POOL_GROUP_DIM = D_MODEL // len(POOL_WINDOWS)
HEAD_DIM = 64
N_KV_HEADS = 4
GROUP = 4
KV_SLAB = N_KV_HEADS * HEAD_DIM
Q_DIM = N_KV_HEADS * GROUP * HEAD_DIM
ROPE_THETA = 10000.0
Q_SCALE = HEAD_DIM ** -0.5 * math.log2(math.e)
D_FF = 2816
N_EXPERTS = 8
TOP_K = 2
NORM_EPS = 1e-6
N_MOD = 6

POOL_HALO = 16
ROW_TILE = 512
FF_TILE = 1408
MOE_ROWS = 512
DISPATCH_TILE = 2048
ATT_TQ = 512
ATT_TK = 1280
KEY_CHUNK = 256
DENOM_ROWS = 8
ATT_TILES = 8
MIB = 1024 * 1024
LANES = 128
ROW_SLABS = D_MODEL // LANES


def _dot(a, b):
    return jnp.dot(a, b, preferred_element_type=F32)


def _rms(x, g):
    ms = jnp.mean(x * x, axis=-1, keepdims=True)
    return x * lax.rsqrt(ms + NORM_EPS) * g


def _silu(x):
    return x * (1.0 / (1.0 + jnp.exp(-x)))


def _load_rows(ref, n_rows, first=0):
    return jnp.concatenate(
        [ref[pl.ds(ROW_SLABS * first + s, n_rows, stride=ROW_SLABS), :] for s in range(ROW_SLABS)], axis=1)


def _store_rows(ref, val):
    for s in range(ROW_SLABS):
        ref[pl.ds(s, val.shape[0], stride=ROW_SLABS), :] = val[:, s * LANES:(s + 1) * LANES]


def _tile_of_row(ref, row):
    return ref.at[pl.ds(pl.multiple_of(row * ROW_SLABS, ROW_SLABS), ROW_SLABS)]


def _params(semantics, vmem_mib):
    return pltpu.CompilerParams(dimension_semantics=semantics, vmem_limit_bytes=vmem_mib * MIB)


def _adaln_kernel(c_ref, w_ref, b_ref, o_ref):
    s = _silu(c_ref[...])
    o_ref[0] = jnp.dot(s, w_ref[0], preferred_element_type=F32, precision=lax.Precision.HIGHEST) + b_ref[0]


def _adaln(cvecs, ada_w, ada_b):
    depth, d, n = ada_w.shape
    tn = n // 4
    return pl.pallas_call(
        _adaln_kernel,
        grid=(depth, n // tn),
        in_specs=[
            pl.BlockSpec((8, d), lambda i, j: (0, 0)),
            pl.BlockSpec((1, d, tn), lambda i, j: (i, 0, j)),
            pl.BlockSpec((1, 1, tn), lambda i, j: (i, 0, j)),
        ],
        out_specs=pl.BlockSpec((1, 8, tn), lambda i, j: (i, 0, j)),
        out_shape=jax.ShapeDtypeStruct((depth, 8, n), F32),
        compiler_params=_params(("parallel", "parallel"), 32),
        name="adaln",
    )(cvecs, ada_w, ada_b.reshape(depth, 1, n))


def _pool_kernel(xp_ref, xc_ref, xn_ref, mod_ref, ng_ref, pw_ref, ps_ref, o_ref, *, seq, tm):
    t = pl.program_id(1)
    xc = xc_ref[0]
    xa = jnp.concatenate([xp_ref[0], xc, xn_ref[0]], axis=0)
    sh, sc, gate = mod_ref[0, 0:1, :], mod_ref[0, 1:2, :], mod_ref[0, 2:3, :]
    h = _rms(xa, ng_ref[...]) * (1.0 + sc) + sh
    hb = h.astype(BF16)
    wide = tm + 2 * POOL_HALO
    row = lax.broadcasted_iota(jnp.int32, (tm, wide), 0) + t * tm
    col = lax.broadcasted_iota(jnp.int32, (tm, wide), 1) + (t * tm - POOL_HALO)
    row1 = lax.broadcasted_iota(jnp.int32, (tm, 1), 0) + t * tm
    ys = []
    for g, w in enumerate(POOL_WINDOWS):
        half = w // 2
        lo = jnp.maximum(row - half, 0)
        hi = jnp.minimum(row + half, seq)
        band = jnp.where((col >= lo) & (col < hi), 1.0, 0.0).astype(BF16)
        cnt = (jnp.minimum(row1 + half, seq) - jnp.maximum(row1 - half, 0)).astype(F32)
        lanes = slice(g * POOL_GROUP_DIM, (g + 1) * POOL_GROUP_DIM)
        win_sum = _dot(band, hb[:, lanes])
        d = win_sum / cnt - h[POOL_HALO:POOL_HALO + tm, lanes]
        ys.append(_dot(d.astype(BF16), pw_ref[g]))
    y = jnp.concatenate(ys, axis=1) * ps_ref[...]
    o_ref[0] = xc + gate * y


def _pool_layer(x, mods, norm_g, pool_w, pool_scale):
    b, seq, d = x.shape
    tm = min(ROW_TILE, seq)
    per = tm // POOL_HALO
    last = seq // POOL_HALO - 1
    return pl.pallas_call(
        functools.partial(_pool_kernel, seq=seq, tm=tm),
        grid=(b, seq // tm),
        in_specs=[
            pl.BlockSpec((1, POOL_HALO, d), lambda i, t: (i, jnp.maximum(t * per - 1, 0), 0)),
            pl.BlockSpec((1, tm, d), lambda i, t: (i, t, 0)),
            pl.BlockSpec((1, POOL_HALO, d), lambda i, t: (i, jnp.minimum((t + 1) * per, last), 0)),
            pl.BlockSpec((1, N_MOD, d), lambda i, t: (i, 0, 0)),
            pl.BlockSpec((1, d), lambda i, t: (0, 0)),
            pl.BlockSpec(pool_w.shape, lambda i, t: (0, 0, 0)),
            pl.BlockSpec((1, d), lambda i, t: (0, 0)),
        ],
        out_specs=pl.BlockSpec((1, tm, d), lambda i, t: (i, t, 0)),
        out_shape=jax.ShapeDtypeStruct(x.shape, F32),
        compiler_params=_params(("parallel", "parallel"), 48),
        name="pool_mix",
    )(x, x, x, mods, norm_g.reshape(1, d), pool_w, pool_scale.reshape(1, d))


def _swiglu(h, w1_ref, w3_ref, w2_ref):
    acc = None
    for f in range(D_FF // FF_TILE):
        cols = slice(f * FF_TILE, (f + 1) * FF_TILE)
        u = _silu(_dot(h, w1_ref[:, cols])) * _dot(h, w3_ref[:, cols])
        part = _dot(u.astype(BF16), w2_ref[cols, :])
        acc = part if acc is None else acc + part
    return acc


def _ffn_dense_kernel(x_ref, mod_ref, ng_ref, w1_ref, w3_ref, w2_ref, o_ref):
    sh, sc, gate = mod_ref[0, 3:4, :], mod_ref[0, 4:5, :], mod_ref[0, 5:6, :]
    x = x_ref[...]
    h = (_rms(x, ng_ref[...]) * (1.0 + sc) + sh).astype(BF16)
    o_ref[...] = x + gate * _swiglu(h, w1_ref, w3_ref, w2_ref)


def _ffn_dense_layer(x, mods, norm_g, w1, w3, w2):
    b, seq, d = x.shape
    t = b * seq
    tm = min(ROW_TILE, seq)
    per_batch = seq // tm
    fixed = lambda i: (0, 0)
    out = pl.pallas_call(
        _ffn_dense_kernel,
        grid=(t // tm,),
        in_specs=[
            pl.BlockSpec((tm, d), lambda i: (i, 0)),
            pl.BlockSpec((1, N_MOD, d), lambda i: (i // per_batch, 0, 0)),
            pl.BlockSpec((1, d), fixed),
            pl.BlockSpec((d, D_FF), fixed),
            pl.BlockSpec((d, D_FF), fixed),
            pl.BlockSpec((D_FF, d), fixed),
        ],
        out_specs=pl.BlockSpec((tm, d), lambda i: (i, 0)),
        out_shape=jax.ShapeDtypeStruct((t, d), F32),
        compiler_params=_params(("parallel",), 60),
        name="ffn_dense",
    )(x.reshape(t, d), mods, norm_g.reshape(1, d), w1, w3, w2)
    return out.reshape(b, seq, d)


def _head_norm_rope(z, ones_blk, gain, cos, sin):
    sq = z * z
    hi = sq.astype(BF16)
    lo = (sq - hi.astype(F32)).astype(BF16)
    ss = _dot(hi, ones_blk) + _dot(lo, ones_blk)
    zn = z * lax.rsqrt(ss * (1.0 / HEAD_DIM) + NORM_EPS) * gain
    if cos is None:
        return zn
    width = z.shape[1]
    fwd = pltpu.roll(zn, width - 16, 1)
    bwd = pltpu.roll(zn, 16, 1)
    lane = lax.broadcasted_iota(jnp.int32, zn.shape, 1)
    partner = jnp.where((lane & 31) < 16, fwd, bwd)
    return zn * cos + partner * sin


def _qkv_kernel(*refs, n_q, rope, n_aliased):
    x_ref, mod_ref, ng_ref, w_ref, e_ref, qg_ref, kg_ref = refs[:7]
    refs = refs[7:]
    if rope:
        cos_ref, sin_ref = refs[:2]
        refs = refs[2:]
        cos, sin = cos_ref[...], sin_ref[...]
    else:
        cos = sin = None
    refs = refs[n_aliased:]
    sh, sc = mod_ref[0, 0:1, :], mod_ref[0, 1:2, :]
    hb = (_rms(x_ref[0], ng_ref[...]) * (1.0 + sc) + sh).astype(BF16)
    qkv = _dot(hb, w_ref[...])
    ones_blk = e_ref[...]
    if n_q:
        q_ref, k_ref, v_ref = refs
        for c in range(n_q):
            z = _head_norm_rope(qkv[:, c * KV_SLAB:(c + 1) * KV_SLAB], ones_blk, qg_ref[...], cos, sin)
            q_ref[0, :, c * KV_SLAB:(c + 1) * KV_SLAB] = (z * Q_SCALE).astype(BF16)
    else:
        k_ref, v_ref = refs
    base = n_q * KV_SLAB
    k_ref[0] = _head_norm_rope(qkv[:, base:base + KV_SLAB], ones_blk, kg_ref[...], cos, sin).astype(BF16)
    v = qkv[:, base + KV_SLAB:base + 2 * KV_SLAB]
    for j in range(v_ref.shape[1]):
        v_ref[0, j] = v[j * KEY_CHUNK:(j + 1) * KEY_CHUNK, :].T.astype(BF16)


def _qkv_layer(x, mods, norm_g, w, q_gain, k_gain, tables, key_offset, kv_buffers):
    b, seq, d = x.shape
    tm = min(ROW_TILE, seq)
    width = w.shape[1]
    n_q = (width - 2 * KV_SLAB) // KV_SLAB
    rope = tables is not None
    total_keys = kv_buffers[0].shape[1]
    assert tm % KEY_CHUNK == 0 and key_offset % tm == 0 and total_keys % KEY_CHUNK == 0
    seg = lax.broadcasted_iota(jnp.int32, (KV_SLAB, KV_SLAB), 0) // HEAD_DIM
    ones_blk = (seg == seg.T).astype(BF16)
    in_specs = [
        pl.BlockSpec((1, tm, d), lambda i, t: (i, t, 0)),
        pl.BlockSpec((1, N_MOD, d), lambda i, t: (i, 0, 0)),
        pl.BlockSpec((1, d), lambda i, t: (0, 0)),
        pl.BlockSpec((d, width), lambda i, t: (0, 0)),
        pl.BlockSpec((KV_SLAB, KV_SLAB), lambda i, t: (0, 0)),
        pl.BlockSpec((1, KV_SLAB), lambda i, t: (0, 0)),
        pl.BlockSpec((1, KV_SLAB), lambda i, t: (0, 0)),
    ]
    args = [x, mods, norm_g.reshape(1, d), w, ones_blk,
            jnp.tile(q_gain, N_KV_HEADS).reshape(1, KV_SLAB), jnp.tile(k_gain, N_KV_HEADS).reshape(1, KV_SLAB)]
    if rope:
        in_specs += [pl.BlockSpec((tm, KV_SLAB), lambda i, t: (t, 0))] * 2
        args += list(tables)
    first_tile, chunks = key_offset // tm, tm // KEY_CHUNK
    out_specs = [pl.BlockSpec((1, tm, KV_SLAB), lambda i, t: (i, first_tile + t, 0)),
                 pl.BlockSpec((1, chunks, KV_SLAB, KEY_CHUNK), lambda i, t: (i, first_tile + t, 0, 0))]
    out_shape = [jax.ShapeDtypeStruct((b, total_keys, KV_SLAB), BF16),
                 jax.ShapeDtypeStruct((b, total_keys // KEY_CHUNK, KV_SLAB, KEY_CHUNK), BF16)]
    if n_q:
        out_specs = [pl.BlockSpec((1, tm, n_q * KV_SLAB), lambda i, t: (i, t, 0))] + out_specs
        out_shape = [jax.ShapeDtypeStruct((b, seq, n_q * KV_SLAB), BF16)] + out_shape
    aliases = {len(args) + j: len(out_shape) - 2 + j for j in range(2)}
    in_specs += [pl.BlockSpec(memory_space=pl.ANY)] * 2
    args += list(kv_buffers)
    outs = pl.pallas_call(
        functools.partial(_qkv_kernel, n_q=n_q, rope=rope, n_aliased=len(aliases)),
        grid=(b, seq // tm),
        in_specs=in_specs,
        out_specs=out_specs,
        out_shape=out_shape,
        input_output_aliases=aliases,
        compiler_params=_params(("parallel", "parallel"), 48),
        name="qkv_rope" if rope else "kv_ctx",
    )(*args)
    return outs if n_q else [None] + list(outs)


def _rope_tables(seq):
    half = HEAD_DIM // 4
    t = jnp.arange(seq, dtype=jnp.int32)
    pos = jnp.stack([t // GRID_W, t % GRID_W], axis=1).astype(F32)
    freqs = ROPE_THETA ** (-jnp.arange(half, dtype=F32) / half)
    ang = pos[:, :, None] * freqs[None, None, :]
    cos = jnp.cos(ang)
    sin = jnp.sin(ang)
    cos_h = jnp.concatenate([cos, cos], axis=2).reshape(seq, HEAD_DIM)
    sin_h = jnp.concatenate([-sin, sin], axis=2).reshape(seq, HEAD_DIM)
    return jnp.tile(cos_h, (1, N_KV_HEADS)), jnp.tile(sin_h, (1, N_KV_HEADS))


def _attn_kernel(q_ref, k_ref, vt_ref, o_ref, qt_scr, m_scr, acc_scr, sa_scr, sb_scr, smaxa_scr, smaxb_scr, *,
                 tq, tk, n_kblk, n_tiles):
    odd_head = lax.rem(pl.program_id(1), 2) == 1

    def load_queries(t):
        rows = pl.ds(pl.multiple_of(t * tq, tq), tq)
        for j in range(GROUP // 2):
            pair_t = q_ref[0, rows, j * 2 * HEAD_DIM:(j + 1) * 2 * HEAD_DIM].astype(F32).T
            for h in range(2):
                qt = pair_t[h * HEAD_DIM:(h + 1) * HEAD_DIM, :]
                zero = jnp.zeros_like(qt)
                low, high = jnp.concatenate([qt, zero], axis=0), jnp.concatenate([zero, qt], axis=0)
                qt_scr[2 * j + h] = jnp.where(odd_head, high, low).astype(BF16)

    def reset_state():
        m_scr[...] = jnp.full_like(m_scr, -jnp.inf)
        acc_scr[...] = jnp.zeros_like(acc_scr)

    def write_output(t):
        for g in range(GROUP):
            acc = acc_scr[g]
            out = acc[:HEAD_DIM] / jnp.sum(acc[HEAD_DIM:], axis=0, keepdims=True)
            o_ref[0, t, g * HEAD_DIM:(g + 1) * HEAD_DIM, :] = out.astype(BF16)

    n_chunks = tk // KEY_CHUNK

    def scores_chunk(g, k, c, buf):
        keys = k_ref[0, pl.ds(pl.multiple_of(k * tk + c * KEY_CHUNK, KEY_CHUNK), KEY_CHUNK), :]
        s = _dot(keys, qt_scr[g])
        buf[0][g, c * KEY_CHUNK:(c + 1) * KEY_CHUNK, :] = s
        return jnp.max(s, axis=0, keepdims=True)

    def values_chunk(g, k, c, buf, m_next):
        p = jnp.exp2(buf[0][g, c * KEY_CHUNK:(c + 1) * KEY_CHUNK, :] - m_next[0:1, :])
        psum = jnp.sum(p.reshape(KEY_CHUNK // DENOM_ROWS, DENOM_ROWS, tq), axis=0)
        pv = _dot(vt_ref[0, k * n_chunks + c], p.astype(BF16))
        return jnp.concatenate([pv, psum], axis=0)

    def scores(k, buf):
        for g in range(GROUP):
            smax = None
            for c in range(n_chunks):
                cmax = scores_chunk(g, k, c, buf)
                smax = cmax if smax is None else jnp.maximum(smax, cmax)
            buf[1][g] = jnp.broadcast_to(smax, (8, tq))

    def scores_and_accumulate(k_new, buf_new, k_old, buf_old):
        for g in range(GROUP):
            m_prev = m_scr[g]
            m_next = jnp.maximum(m_prev, buf_old[1][g])
            alpha = jnp.exp2(m_prev - m_next)
            smax = pv = None
            for c in range(n_chunks):
                cmax = scores_chunk(g, k_new, c, buf_new)
                part = values_chunk(g, k_old, c, buf_old, m_next)
                smax = cmax if smax is None else jnp.maximum(smax, cmax)
                pv = part if pv is None else pv + part
            buf_new[1][g] = jnp.broadcast_to(smax, (8, tq))
            acc_scr[g] = alpha[0:1, :] * acc_scr[g] + pv
            m_scr[g] = m_next

    def query_tile(t, first, second):
        def key_block_pair(j, carry):
            k = 2 * j
            scores_and_accumulate(k + 1, second, k, first)
            scores_and_accumulate(k + 2, first, k + 1, second)
            return carry

        lax.fori_loop(0, (n_kblk - 1) // 2, key_block_pair, 0)
        load_queries(jnp.minimum(t + 1, n_tiles - 1))
        scores_and_accumulate(0, second, n_kblk - 1, first)
        write_output(t)
        reset_state()

    buf_a, buf_b = (sa_scr, smaxa_scr), (sb_scr, smaxb_scr)
    load_queries(0)
    reset_state()
    scores(0, buf_a)

    def query_tile_pair(i, carry):
        query_tile(2 * i, buf_a, buf_b)
        query_tile(2 * i + 1, buf_b, buf_a)
        return carry

    lax.fori_loop(0, n_tiles // 2, query_tile_pair, 0)


def _attention(q, k, vt):
    b, seq, _ = q.shape
    lk = k.shape[1]
    tq, tk, n_tiles = ATT_TQ, ATT_TK, ATT_TILES
    n_kblk = lk // tk
    assert n_kblk % 2 == 1 and n_tiles % 2 == 0 and seq % (tq * n_tiles) == 0 and lk == n_kblk * tk
    assert tk % KEY_CHUNK == 0 and vt.shape == (b, lk // KEY_CHUNK, KV_SLAB, KEY_CHUNK)
    stat_rows = 8
    return pl.pallas_call(
        functools.partial(_attn_kernel, tq=tq, tk=tk, n_kblk=n_kblk, n_tiles=n_tiles),
        grid=(b, N_KV_HEADS, seq // (tq * n_tiles)),
        in_specs=[
            pl.BlockSpec((1, tq * n_tiles, KV_SLAB), lambda i, h, qi: (i, qi, h)),
            pl.BlockSpec((1, lk, 2 * HEAD_DIM), lambda i, h, qi: (i, 0, h // 2)),
            pl.BlockSpec((1, lk // KEY_CHUNK, HEAD_DIM, KEY_CHUNK), lambda i, h, qi: (i, 0, h, 0)),
        ],
        out_specs=pl.BlockSpec((1, n_tiles, KV_SLAB, tq), lambda i, h, qi: (i, qi, h, 0)),
        out_shape=jax.ShapeDtypeStruct((b, seq // tq, Q_DIM, tq), BF16),
        scratch_shapes=[
            pltpu.VMEM((GROUP, 2 * HEAD_DIM, tq), BF16),
            pltpu.VMEM((GROUP, stat_rows, tq), F32),
            pltpu.VMEM((GROUP, HEAD_DIM + DENOM_ROWS, tq), F32),
            pltpu.VMEM((GROUP, tk, tq), F32),
            pltpu.VMEM((GROUP, tk, tq), F32),
            pltpu.VMEM((GROUP, stat_rows, tq), F32),
            pltpu.VMEM((GROUP, stat_rows, tq), F32),
        ],
        compiler_params=_params(("parallel", "parallel", "parallel"), 60),
        name="gqa_attention",
    )(q, k, vt)


def _oproj_router_kernel(a_ref, x_ref, mod_ref, ng_ref, wo_ref, rw_ref, x_out, h_out, e_out, g_out, r_out, n_out,
                         count_scr):
    @pl.when(pl.program_id(0) == 0)
    def _():
        count_scr[...] = jnp.zeros_like(count_scr)

    g1 = mod_ref[0, 2:3, :]
    sh, sc = mod_ref[0, 3:4, :], mod_ref[0, 4:5, :]
    proj = lax.dot_general(a_ref[0], wo_ref[...], (((0,), (0,)), ((), ())), preferred_element_type=F32)
    x = x_ref[...] + g1 * proj
    x_out[...] = x
    h = _rms(x, ng_ref[...]) * (1.0 + sc) + sh
    _store_rows(h_out, h)
    logits = lax.dot_general(rw_ref[...], h, (((1,), (1,)), ((), ())),
                             preferred_element_type=F32, precision=lax.Precision.HIGHEST)
    eid = lax.broadcasted_iota(jnp.int32, logits.shape, 0)
    m1 = jnp.max(logits, axis=0, keepdims=True)
    i1 = jnp.min(jnp.where(logits == m1, eid, N_EXPERTS), axis=0, keepdims=True)
    rest = jnp.where(eid == i1, -jnp.inf, logits)
    m2 = jnp.max(rest, axis=0, keepdims=True)
    i2 = jnp.min(jnp.where(rest == m2, eid, N_EXPERTS), axis=0, keepdims=True)
    e = jnp.exp(m2 - m1)
    e_out[...] = jnp.concatenate([i1, i2], axis=0)
    g_out[...] = jnp.concatenate([1.0 / (1.0 + e), e / (1.0 + e)], axis=0)
    tm = logits.shape[1]
    chose = jnp.where((eid == i1) | (eid == i2), 1.0, 0.0)
    earlier = (lax.broadcasted_iota(jnp.int32, (tm, tm), 0) < lax.broadcasted_iota(jnp.int32, (tm, tm), 1))
    before = _dot(chose.astype(BF16), jnp.where(earlier, 1.0, 0.0).astype(BF16)) + count_scr[:, 0:1]
    r1 = jnp.sum(jnp.where(eid == i1, before, 0.0), axis=0, keepdims=True)
    r2 = jnp.sum(jnp.where(eid == i2, before, 0.0), axis=0, keepdims=True)
    r_out[...] = jnp.concatenate([r1, r2], axis=0).astype(jnp.int32)
    count_scr[...] += jnp.sum(chose, axis=1, keepdims=True)
    n_out[...] = count_scr[...].astype(jnp.int32)


def _oproj_router_layer(attn_t, x, mods, norm_g, w_o, router_w):
    b, seq, d = x.shape
    t = b * seq
    tm = ROW_TILE
    per_batch = seq // tm
    row = lambda i: (i, 0)
    fixed = lambda i: (0, 0)
    return pl.pallas_call(
        _oproj_router_kernel,
        grid=(t // tm,),
        in_specs=[
            pl.BlockSpec((1, d, tm), lambda i: (i, 0, 0)),
            pl.BlockSpec((tm, d), row),
            pl.BlockSpec((1, N_MOD, d), lambda i: (i // per_batch, 0, 0)),
            pl.BlockSpec((1, d), fixed),
            pl.BlockSpec((d, d), fixed),
            pl.BlockSpec((N_EXPERTS, d), fixed),
        ],
        out_specs=[
            pl.BlockSpec((tm, d), row),
            pl.BlockSpec((tm * ROW_SLABS, LANES), row),
            pl.BlockSpec((TOP_K, tm), lambda i: (0, i)),
            pl.BlockSpec((TOP_K, tm), lambda i: (0, i)),
            pl.BlockSpec((TOP_K, tm), lambda i: (0, i)),
            pl.BlockSpec((N_EXPERTS, LANES), fixed),
        ],
        out_shape=[
            jax.ShapeDtypeStruct((t, d), F32),
            jax.ShapeDtypeStruct((t * ROW_SLABS, LANES), F32),
            jax.ShapeDtypeStruct((TOP_K, t), jnp.int32),
            jax.ShapeDtypeStruct((TOP_K, t), F32),
            jax.ShapeDtypeStruct((TOP_K, t), jnp.int32),
            jax.ShapeDtypeStruct((N_EXPERTS, LANES), jnp.int32),
        ],
        scratch_shapes=[pltpu.VMEM((N_EXPERTS, LANES), F32)],
        compiler_params=_params(("arbitrary",), 48),
        name="oproj_router",
    )(attn_t.reshape(t // tm, d, tm), x.reshape(t, d), mods, norm_g.reshape(1, d), w_o, router_w.T)


def _row_copy(src_ref, dst_ref, src_row, dst_row, sem):
    return pltpu.make_async_copy(_tile_of_row(src_ref, src_row), _tile_of_row(dst_ref, dst_row), sem)


def _rows_copy(src_ref, dst_ref, rows, sem):
    return pltpu.make_async_copy(src_ref.at[pl.ds(0, rows * ROW_SLABS)], dst_ref.at[pl.ds(0, rows * ROW_SLABS)], sem)


def _dest_spec(tm):
    return pl.BlockSpec((1, 1, TOP_K * tm), lambda i: (i, 0, 0), memory_space=pltpu.SMEM)


def _dispatch_kernel(dest_ref, zero_ref, h_ref, xs_ref, zeros_scr, sem):
    tm = h_ref.shape[0] // ROW_SLABS

    @pl.when(pl.program_id(0) == 0)
    def _():
        zeros_scr[...] = jnp.zeros_like(zeros_scr)
        block_rows = MOE_ROWS * ROW_SLABS

        def zero_block(j):
            return pltpu.make_async_copy(
                zeros_scr, xs_ref.at[pl.ds(pl.multiple_of(j * block_rows, block_rows), block_rows)], sem)

        def start(j, carry):
            @pl.when(zero_ref[j] != 0)
            def _():
                zero_block(j).start()
            return carry

        def drain(j, carry):
            @pl.when(zero_ref[j] != 0)
            def _():
                zero_block(j).wait()
            return carry

        lax.fori_loop(0, zero_ref.shape[0], start, 0)
        lax.fori_loop(0, zero_ref.shape[0], drain, 0)

    def issue(r, carry):
        for k in range(TOP_K):
            _row_copy(h_ref, xs_ref, r, dest_ref[0, 0, TOP_K * r + k], sem).start(priority=k)
        return carry

    lax.fori_loop(0, tm, issue, 0)
    for _ in range(TOP_K):
        _rows_copy(h_ref, xs_ref, tm, sem).wait()


def _dispatch(h, dest, zero_blocks, cap):
    t = h.shape[0] // ROW_SLABS
    tm = DISPATCH_TILE
    return pl.pallas_call(
        _dispatch_kernel,
        grid=(t // tm,),
        in_specs=[
            _dest_spec(tm),
            pl.BlockSpec(memory_space=pltpu.SMEM),
            pl.BlockSpec((tm * ROW_SLABS, LANES), lambda i: (i, 0)),
        ],
        out_specs=pl.BlockSpec(memory_space=pl.ANY),
        out_shape=jax.ShapeDtypeStruct((cap * ROW_SLABS, LANES), h.dtype),
        scratch_shapes=[pltpu.VMEM((MOE_ROWS * ROW_SLABS, LANES), h.dtype), pltpu.SemaphoreType.DMA(())],
        compiler_params=_params(("arbitrary",), 32),
        name="moe_dispatch",
    )(dest.reshape(t // tm, 1, TOP_K * tm), zero_blocks, h)


def _ffn_expert_kernel(blk_ref, x_ref, w1_ref, w3_ref, w2_ref, o_ref):
    in_use = pl.program_id(0) < blk_ref[pl.num_programs(0)]

    @pl.when(in_use)
    def _():
        h = _load_rows(x_ref, MOE_ROWS).astype(BF16)
        _store_rows(o_ref, _swiglu(h, w1_ref.at[0], w3_ref.at[0], w2_ref.at[0]))

    @pl.when(jnp.logical_not(in_use))
    def _():
        o_ref[...] = jnp.zeros_like(o_ref)


def _ffn_experts(xs, blk_e, w1, w3, w2):
    cap = xs.shape[0] // ROW_SLABS
    d = D_MODEL
    n_blk = cap // MOE_ROWS
    row_spec = pl.BlockSpec((MOE_ROWS * ROW_SLABS, LANES), lambda i, e: (i, 0))
    grid_spec = pltpu.PrefetchScalarGridSpec(
        num_scalar_prefetch=1,
        grid=(n_blk,),
        in_specs=[
            row_spec,
            pl.BlockSpec((1, d, D_FF), lambda i, e: (e[i], 0, 0)),
            pl.BlockSpec((1, d, D_FF), lambda i, e: (e[i], 0, 0)),
            pl.BlockSpec((1, D_FF, d), lambda i, e: (e[i], 0, 0)),
        ],
        out_specs=row_spec,
    )
    return pl.pallas_call(
        _ffn_expert_kernel,
        grid_spec=grid_spec,
        out_shape=jax.ShapeDtypeStruct(xs.shape, F32),
        compiler_params=_params(("arbitrary",), 60),
        name="ffn_experts",
    )(blk_e, xs, w1, w3, w2)


def _routing_plan(e_idx, ranks, sizes):
    n_assign = e_idx.size
    padded = (sizes + MOE_ROWS - 1) // MOE_ROWS * MOE_ROWS
    ends = jnp.cumsum(padded)
    pstart = ends - padded
    seg_start = sum(jnp.where(e_idx == e, pstart[e], 0) for e in range(N_EXPERTS))
    dest = (ranks + seg_start).T.reshape(-1).astype(jnp.int32)
    cap = n_assign + N_EXPERTS * MOE_ROWS
    blk_start = jnp.arange(cap // MOE_ROWS, dtype=jnp.int32) * MOE_ROWS
    blk_e = jnp.sum((ends[None, :] <= blk_start[:, None]).astype(jnp.int32), axis=1)
    blk_e = jnp.minimum(blk_e, N_EXPERTS - 1)
    real_end = jnp.where(blk_start < ends[-1], (pstart + sizes)[blk_e], 0)
    zero_blocks = (real_end - blk_start < MOE_ROWS).astype(jnp.int32)
    blk_info = jnp.concatenate([blk_e, ends[-1:] // MOE_ROWS]).astype(jnp.int32)
    return dest, blk_info, zero_blocks, cap


def _final_kernel(dest_ref, dest_next_ref, x_ref, gt_ref, mod_ref, fg_ref, y_ref, o_ref, y_buf, sems):
    tm = x_ref.shape[0]
    i = pl.program_id(0)
    slot = lax.rem(i, 2)
    half = TOP_K * tm

    def fetch(idx_ref, into):
        def issue(r, carry):
            for k in range(TOP_K):
                _row_copy(y_ref, y_buf, idx_ref[0, 0, TOP_K * r + k], into * half + k * tm + r,
                          sems.at[into]).start(priority=k)
            return carry

        lax.fori_loop(0, tm, issue, 0, unroll=8)

    @pl.when(i == 0)
    def _():
        fetch(dest_ref, slot)

    @pl.when(i + 1 < pl.num_programs(0))
    def _():
        fetch(dest_next_ref, 1 - slot)

    _rows_copy(y_ref, y_buf, half, sems.at[slot]).wait()
    gt = gt_ref[...]
    moe = (gt[:, 0:1] * _load_rows(y_buf, tm, slot * half) + gt[:, 1:2] * _load_rows(y_buf, tm, slot * half + tm))
    x = x_ref[...] + mod_ref[0, 5:6, :] * moe
    o_ref[...] = _rms(x, fg_ref[...])


def _final_layer(x, ys, dest, gates_t, mods, final_g, seq):
    t, d = x.shape
    tm = ROW_TILE
    per_batch = seq // tm
    n_steps = t // tm
    dest_tiles = dest.reshape(n_steps, 1, TOP_K * tm)
    return pl.pallas_call(
        _final_kernel,
        grid=(n_steps,),
        in_specs=[
            _dest_spec(tm),
            pl.BlockSpec((1, 1, TOP_K * tm), lambda i: (jnp.minimum(i + 1, n_steps - 1), 0, 0),
                         memory_space=pltpu.SMEM),
            pl.BlockSpec((tm, d), lambda i: (i, 0)),
            pl.BlockSpec((tm, TOP_K), lambda i: (i, 0)),
            pl.BlockSpec((1, N_MOD, d), lambda i: (i // per_batch, 0, 0)),
            pl.BlockSpec((1, d), lambda i: (0, 0)),
            pl.BlockSpec(memory_space=pl.ANY),
        ],
        out_specs=pl.BlockSpec((tm, d), lambda i: (i, 0)),
        out_shape=jax.ShapeDtypeStruct((t, d), F32),
        scratch_shapes=[pltpu.VMEM((2 * TOP_K * tm * ROW_SLABS, LANES), F32), pltpu.SemaphoreType.DMA((2,))],
        compiler_params=_params(("arbitrary",), 48),
        name="moe_sum_final_norm",
    )(dest_tiles, dest_tiles, x, gates_t, mods, final_g.reshape(1, d), ys)


def kernel(x, c, ctx, c_ctx, ada_w, ada_b, norm_g, pool_w, pool_scale, ffn_w1, ffn_w3, ffn_w2, w_qkv, w_o,
           q_norm_g, k_norm_g, router_w, moe_w1, moe_w3, moe_w2, final_g):
    b, seq, d = x.shape
    assert ada_w.shape[0] == 2 and b <= 7, "written for the two-layer block (pool+dense, attention+experts)"

    cvecs = jnp.zeros((8, d), F32).at[:b].set(c).at[b].set(c_ctx)
    mods = _adaln(cvecs, ada_w, ada_b).reshape(2, 8, N_MOD, d)
    mods_x = mods[:, :b]
    mods_c = jnp.broadcast_to(mods[:, b:b + 1], (2, b, N_MOD, d))

    pool_wb = pool_w[0].astype(BF16)
    w1b, w3b, w2b = ffn_w1[0].astype(BF16), ffn_w3[0].astype(BF16), ffn_w2[0].astype(BF16)
    x = _pool_layer(x, mods_x[0], norm_g[0, 0], pool_wb, pool_scale[0])
    ctx = _pool_layer(ctx, mods_c[0], norm_g[0, 0], pool_wb, pool_scale[0])
    x = _ffn_dense_layer(x, mods_x[0], norm_g[0, 1], w1b, w3b, w2b)
    ctx = _ffn_dense_layer(ctx, mods_c[0], norm_g[0, 1], w1b, w3b, w2b)

    wqkv = w_qkv[0].astype(BF16)
    lk = seq + ctx.shape[1]
    kv = (jnp.zeros((b, lk, KV_SLAB), BF16), jnp.zeros((b, lk // KEY_CHUNK, KV_SLAB, KEY_CHUNK), BF16))
    q, *kv = _qkv_layer(x, mods_x[1], norm_g[1, 0], wqkv, q_norm_g[0], k_norm_g[0], _rope_tables(seq), 0, kv)
    _, k_all, vt = _qkv_layer(ctx, mods_c[1], norm_g[1, 0], wqkv[:, Q_DIM:], q_norm_g[0], k_norm_g[0], None, seq, kv)
    attn_t = _attention(q, k_all, vt)

    assert ATT_TQ == ROW_TILE, "the output projection reads one transposed attention tile per row tile"
    x3, hf, e_idx, gates, ranks, counts = _oproj_router_layer(
        attn_t, x, mods_x[1], norm_g[1, 1], w_o[0].astype(BF16), router_w[0])
    dest, blk_e, zero_blocks, cap = _routing_plan(e_idx, ranks, counts[:, 0])
    xs = _dispatch(hf, dest, zero_blocks, cap)
    ys = _ffn_experts(xs, blk_e, moe_w1[0].astype(BF16), moe_w3[0].astype(BF16), moe_w2[0].astype(BF16))
    out = _final_layer(x3, ys, dest, gates.T, mods_x[1], final_g, seq)
    return out.reshape(b, seq, d)
```

```python
import functools
import math

import jax
import jax.numpy as jnp
from jax import lax
from jax.experimental import pallas as pl
from jax.experimental.pallas import tpu as pltpu

F32 = jnp.float32
BF16 = jnp.bfloat16

D_MODEL = 1024
GRID_W = 64
POOL_WINDOWS = (2, 4, 8, 16)
POOL_GROUP_DIM = D_MODEL // len(POOL_WINDOWS)
HEAD_DIM = 64
N_KV_HEADS = 4
GROUP = 4
KV_SLAB = N_KV_HEADS * HEAD_DIM
Q_DIM = N_KV_HEADS * GROUP * HEAD_DIM
ROPE_THETA = 10000.0
Q_SCALE = HEAD_DIM ** -0.5 * math.log2(math.e)
D_FF = 2816
N_EXPERTS = 8
TOP_K = 2
NORM_EPS = 1e-6
N_MOD = 6

POOL_HALO = 16
ROW_TILE = 512
FF_TILE = 1408
MOE_ROWS = 512
DISPATCH_TILE = 2048
ATT_TQ = 512
ATT_TK = 1280
KEY_CHUNK = 256
DENOM_ROWS = 8
ATT_TILES = 16
MIB = 1024 * 1024
LANES = 128
ROW_SLABS = D_MODEL // LANES


def _dot(a, b):
    return jnp.dot(a, b, preferred_element_type=F32)


def _rms(x, g):
    ms = jnp.mean(x * x, axis=-1, keepdims=True)
    return x * lax.rsqrt(ms + NORM_EPS) * g


def _silu(x):
    return x * (1.0 / (1.0 + jnp.exp(-x)))


def _load_rows(ref, n_rows, first=0):
    return jnp.concatenate(
        [ref[pl.ds(ROW_SLABS * first + s, n_rows, stride=ROW_SLABS), :] for s in range(ROW_SLABS)], axis=1)


def _store_rows(ref, val):
    for s in range(ROW_SLABS):
        ref[pl.ds(s, val.shape[0], stride=ROW_SLABS), :] = val[:, s * LANES:(s + 1) * LANES]


def _tile_of_row(ref, row):
    return ref.at[pl.ds(pl.multiple_of(row * ROW_SLABS, ROW_SLABS), ROW_SLABS)]


def _params(semantics, vmem_mib):
    return pltpu.CompilerParams(dimension_semantics=semantics, vmem_limit_bytes=vmem_mib * MIB)


def _adaln_kernel(c_ref, w_ref, b_ref, o_ref):
    s = _silu(c_ref[...])
    o_ref[0] = jnp.dot(s, w_ref[0], preferred_element_type=F32, precision=lax.Precision.HIGHEST) + b_ref[0]


def _adaln(cvecs, ada_w, ada_b):
    depth, d, n = ada_w.shape
    tn = n // 4
    return pl.pallas_call(
        _adaln_kernel,
        grid=(depth, n // tn),
        in_specs=[
            pl.BlockSpec((8, d), lambda i, j: (0, 0)),
            pl.BlockSpec((1, d, tn), lambda i, j: (i, 0, j)),
            pl.BlockSpec((1, 1, tn), lambda i, j: (i, 0, j)),
        ],
        out_specs=pl.BlockSpec((1, 8, tn), lambda i, j: (i, 0, j)),
        out_shape=jax.ShapeDtypeStruct((depth, 8, n), F32),
        compiler_params=_params(("parallel", "parallel"), 32),
        name="adaln",
    )(cvecs, ada_w, ada_b.reshape(depth, 1, n))


def _pool_kernel(xp_ref, xc_ref, xn_ref, mod_ref, ng_ref, pw_ref, ps_ref, o_ref, *, seq, tm):
    t = pl.program_id(1)
    xc = xc_ref[0]
    xa = jnp.concatenate([xp_ref[0], xc, xn_ref[0]], axis=0)
    sh, sc, gate = mod_ref[0, 0:1, :], mod_ref[0, 1:2, :], mod_ref[0, 2:3, :]
    h = _rms(xa, ng_ref[...]) * (1.0 + sc) + sh
    hb = h.astype(BF16)
    wide = tm + 2 * POOL_HALO
    row = lax.broadcasted_iota(jnp.int32, (tm, wide), 0) + t * tm
    col = lax.broadcasted_iota(jnp.int32, (tm, wide), 1) + (t * tm - POOL_HALO)
    row1 = lax.broadcasted_iota(jnp.int32, (tm, 1), 0) + t * tm
    ys = []
    for g, w in enumerate(POOL_WINDOWS):
        half = w // 2
        lo = jnp.maximum(row - half, 0)
        hi = jnp.minimum(row + half, seq)
        band = jnp.where((col >= lo) & (col < hi), 1.0, 0.0).astype(BF16)
        cnt = (jnp.minimum(row1 + half, seq) - jnp.maximum(row1 - half, 0)).astype(F32)
        lanes = slice(g * POOL_GROUP_DIM, (g + 1) * POOL_GROUP_DIM)
        win_sum = _dot(band, hb[:, lanes])
        d = win_sum / cnt - h[POOL_HALO:POOL_HALO + tm, lanes]
        ys.append(_dot(d.astype(BF16), pw_ref[g]))
    y = jnp.concatenate(ys, axis=1) * ps_ref[...]
    o_ref[0] = xc + gate * y


def _pool_layer(x, mods, norm_g, pool_w, pool_scale):
    b, seq, d = x.shape
    tm = min(ROW_TILE, seq)
    per = tm // POOL_HALO
    last = seq // POOL_HALO - 1
    return pl.pallas_call(
        functools.partial(_pool_kernel, seq=seq, tm=tm),
        grid=(b, seq // tm),
        in_specs=[
            pl.BlockSpec((1, POOL_HALO, d), lambda i, t: (i, jnp.maximum(t * per - 1, 0), 0)),
            pl.BlockSpec((1, tm, d), lambda i, t: (i, t, 0)),
            pl.BlockSpec((1, POOL_HALO, d), lambda i, t: (i, jnp.minimum((t + 1) * per, last), 0)),
            pl.BlockSpec((1, N_MOD, d), lambda i, t: (i, 0, 0)),
            pl.BlockSpec((1, d), lambda i, t: (0, 0)),
            pl.BlockSpec(pool_w.shape, lambda i, t: (0, 0, 0)),
            pl.BlockSpec((1, d), lambda i, t: (0, 0)),
        ],
        out_specs=pl.BlockSpec((1, tm, d), lambda i, t: (i, t, 0)),
        out_shape=jax.ShapeDtypeStruct(x.shape, F32),
        compiler_params=_params(("parallel", "parallel"), 48),
        name="pool_mix",
    )(x, x, x, mods, norm_g.reshape(1, d), pool_w, pool_scale.reshape(1, d))


def _swiglu(h, w1_ref, w3_ref, w2_ref):
    acc = None
    for f in range(D_FF // FF_TILE):
        cols = slice(f * FF_TILE, (f + 1) * FF_TILE)
        u = _silu(_dot(h, w1_ref[:, cols])) * _dot(h, w3_ref[:, cols])
        part = _dot(u.astype(BF16), w2_ref[cols, :])
        acc = part if acc is None else acc + part
    return acc


def _ffn_dense_kernel(x_ref, mod_ref, ng_ref, w1_ref, w3_ref, w2_ref, o_ref):
    sh, sc, gate = mod_ref[0, 3:4, :], mod_ref[0, 4:5, :], mod_ref[0, 5:6, :]
    x = x_ref[...]
    h = (_rms(x, ng_ref[...]) * (1.0 + sc) + sh).astype(BF16)
    o_ref[...] = x + gate * _swiglu(h, w1_ref, w3_ref, w2_ref)


def _ffn_dense_layer(x, mods, norm_g, w1, w3, w2):
    b, seq, d = x.shape
    t = b * seq
    tm = min(ROW_TILE, seq)
    per_batch = seq // tm
    fixed = lambda i: (0, 0)
    out = pl.pallas_call(
        _ffn_dense_kernel,
        grid=(t // tm,),
        in_specs=[
            pl.BlockSpec((tm, d), lambda i: (i, 0)),
            pl.BlockSpec((1, N_MOD, d), lambda i: (i // per_batch, 0, 0)),
            pl.BlockSpec((1, d), fixed),
            pl.BlockSpec((d, D_FF), fixed),
            pl.BlockSpec((d, D_FF), fixed),
            pl.BlockSpec((D_FF, d), fixed),
        ],
        out_specs=pl.BlockSpec((tm, d), lambda i: (i, 0)),
        out_shape=jax.ShapeDtypeStruct((t, d), F32),
        compiler_params=_params(("parallel",), 60),
        name="ffn_dense",
    )(x.reshape(t, d), mods, norm_g.reshape(1, d), w1, w3, w2)
    return out.reshape(b, seq, d)


def _head_norm_rope(z, ones_blk, gain, cos, sin):
    sq = z * z
    hi = sq.astype(BF16)
    lo = (sq - hi.astype(F32)).astype(BF16)
    ss = _dot(hi, ones_blk) + _dot(lo, ones_blk)
    zn = z * lax.rsqrt(ss * (1.0 / HEAD_DIM) + NORM_EPS) * gain
    if cos is None:
        return zn
    width = z.shape[1]
    fwd = pltpu.roll(zn, width - 16, 1)
    bwd = pltpu.roll(zn, 16, 1)
    lane = lax.broadcasted_iota(jnp.int32, zn.shape, 1)
    partner = jnp.where((lane & 31) < 16, fwd, bwd)
    return zn * cos + partner * sin


def _qkv_kernel(*refs, n_q, rope, n_aliased):
    x_ref, mod_ref, ng_ref, w_ref, e_ref, qg_ref, kg_ref = refs[:7]
    refs = refs[7:]
    if rope:
        cos_ref, sin_ref = refs[:2]
        refs = refs[2:]
        cos, sin = cos_ref[...], sin_ref[...]
    else:
        cos = sin = None
    refs = refs[n_aliased:]
    sh, sc = mod_ref[0, 0:1, :], mod_ref[0, 1:2, :]
    hb = (_rms(x_ref[0], ng_ref[...]) * (1.0 + sc) + sh).astype(BF16)
    qkv = _dot(hb, w_ref[...])
    ones_blk = e_ref[...]
    if n_q:
        q_ref, k_ref, v_ref = refs
        for c in range(n_q):
            z = _head_norm_rope(qkv[:, c * KV_SLAB:(c + 1) * KV_SLAB], ones_blk, qg_ref[...], cos, sin)
            q_ref[0, :, c * KV_SLAB:(c + 1) * KV_SLAB] = (z * Q_SCALE).astype(BF16)
    else:
        k_ref, v_ref = refs
    base = n_q * KV_SLAB
    k_ref[0] = _head_norm_rope(qkv[:, base:base + KV_SLAB], ones_blk, kg_ref[...], cos, sin).astype(BF16)
    v = qkv[:, base + KV_SLAB:base + 2 * KV_SLAB]
    for j in range(v_ref.shape[1]):
        v_ref[0, j] = v[j * KEY_CHUNK:(j + 1) * KEY_CHUNK, :].T.astype(BF16)


def _qkv_layer(x, mods, norm_g, w, q_gain, k_gain, tables, key_offset, kv_buffers):
    b, seq, d = x.shape
    tm = min(ROW_TILE, seq)
    width = w.shape[1]
    n_q = (width - 2 * KV_SLAB) // KV_SLAB
    rope = tables is not None
    total_keys = kv_buffers[0].shape[1]
    assert tm % KEY_CHUNK == 0 and key_offset % tm == 0 and total_keys % KEY_CHUNK == 0
    seg = lax.broadcasted_iota(jnp.int32, (KV_SLAB, KV_SLAB), 0) // HEAD_DIM
    ones_blk = (seg == seg.T).astype(BF16)
    in_specs = [
        pl.BlockSpec((1, tm, d), lambda i, t: (i, t, 0)),
        pl.BlockSpec((1, N_MOD, d), lambda i, t: (i, 0, 0)),
        pl.BlockSpec((1, d), lambda i, t: (0, 0)),
        pl.BlockSpec((d, width), lambda i, t: (0, 0)),
        pl.BlockSpec((KV_SLAB, KV_SLAB), lambda i, t: (0, 0)),
        pl.BlockSpec((1, KV_SLAB), lambda i, t: (0, 0)),
        pl.BlockSpec((1, KV_SLAB), lambda i, t: (0, 0)),
    ]
    args = [x, mods, norm_g.reshape(1, d), w, ones_blk,
            jnp.tile(q_gain, N_KV_HEADS).reshape(1, KV_SLAB), jnp.tile(k_gain, N_KV_HEADS).reshape(1, KV_SLAB)]
    if rope:
        in_specs += [pl.BlockSpec((tm, KV_SLAB), lambda i, t: (t, 0))] * 2
        args += list(tables)
    first_tile, chunks = key_offset // tm, tm // KEY_CHUNK
    out_specs = [pl.BlockSpec((1, tm, KV_SLAB), lambda i, t: (i, first_tile + t, 0)),
                 pl.BlockSpec((1, chunks, KV_SLAB, KEY_CHUNK), lambda i, t: (i, first_tile + t, 0, 0))]
    out_shape = [jax.ShapeDtypeStruct((b, total_keys, KV_SLAB), BF16),
                 jax.ShapeDtypeStruct((b, total_keys // KEY_CHUNK, KV_SLAB, KEY_CHUNK), BF16)]
    if n_q:
        out_specs = [pl.BlockSpec((1, tm, n_q * KV_SLAB), lambda i, t: (i, t, 0))] + out_specs
        out_shape = [jax.ShapeDtypeStruct((b, seq, n_q * KV_SLAB), BF16)] + out_shape
    aliases = {len(args) + j: len(out_shape) - 2 + j for j in range(2)}
    in_specs += [pl.BlockSpec(memory_space=pl.ANY)] * 2
    args += list(kv_buffers)
    outs = pl.pallas_call(
        functools.partial(_qkv_kernel, n_q=n_q, rope=rope, n_aliased=len(aliases)),
        grid=(b, seq // tm),
        in_specs=in_specs,
        out_specs=out_specs,
        out_shape=out_shape,
        input_output_aliases=aliases,
        compiler_params=_params(("parallel", "parallel"), 48),
        name="qkv_rope" if rope else "kv_ctx",
    )(*args)
    return outs if n_q else [None] + list(outs)


def _rope_tables(seq):
    half = HEAD_DIM // 4
    t = jnp.arange(seq, dtype=jnp.int32)
    pos = jnp.stack([t // GRID_W, t % GRID_W], axis=1).astype(F32)
    freqs = ROPE_THETA ** (-jnp.arange(half, dtype=F32) / half)
    ang = pos[:, :, None] * freqs[None, None, :]
    cos = jnp.cos(ang)
    sin = jnp.sin(ang)
    cos_h = jnp.concatenate([cos, cos], axis=2).reshape(seq, HEAD_DIM)
    sin_h = jnp.concatenate([-sin, sin], axis=2).reshape(seq, HEAD_DIM)
    return jnp.tile(cos_h, (1, N_KV_HEADS)), jnp.tile(sin_h, (1, N_KV_HEADS))


def _attn_kernel(q_ref, k_ref, vt_ref, o_ref, qt_scr, m_scr, acc_scr, sa_scr, sb_scr, smaxa_scr, smaxb_scr, *,
                 tq, tk, n_kblk, n_tiles):
    odd_head = lax.rem(pl.program_id(1), 2) == 1

    def load_queries(t):
        rows = pl.ds(pl.multiple_of(t * tq, tq), tq)
        for j in range(GROUP // 2):
            pair_t = q_ref[0, rows, j * 2 * HEAD_DIM:(j + 1) * 2 * HEAD_DIM].astype(F32).T
            for h in range(2):
                qt = pair_t[h * HEAD_DIM:(h + 1) * HEAD_DIM, :]
                zero = jnp.zeros_like(qt)
                low, high = jnp.concatenate([qt, zero], axis=0), jnp.concatenate([zero, qt], axis=0)
                qt_scr[2 * j + h] = jnp.where(odd_head, high, low).astype(BF16)

    def reset_state():
        m_scr[...] = jnp.full_like(m_scr, -jnp.inf)
        acc_scr[...] = jnp.zeros_like(acc_scr)

    def write_output(t):
        for g in range(GROUP):
            acc = acc_scr[g]
            out = acc[:HEAD_DIM] / jnp.sum(acc[HEAD_DIM:], axis=0, keepdims=True)
            o_ref[0, t, g * HEAD_DIM:(g + 1) * HEAD_DIM, :] = out.astype(BF16)

    n_chunks = tk // KEY_CHUNK

    def scores_chunk(g, k, c, buf):
        keys = k_ref[0, pl.ds(pl.multiple_of(k * tk + c * KEY_CHUNK, KEY_CHUNK), KEY_CHUNK), :]
        s = _dot(keys, qt_scr[g])
        buf[0][g, c * KEY_CHUNK:(c + 1) * KEY_CHUNK, :] = s
        return jnp.max(s, axis=0, keepdims=True)

    def values_chunk(g, k, c, buf, m_next):
        p = jnp.exp2(buf[0][g, c * KEY_CHUNK:(c + 1) * KEY_CHUNK, :] - m_next[0:1, :])
        psum = jnp.sum(p.reshape(KEY_CHUNK // DENOM_ROWS, DENOM_ROWS, tq), axis=0)
        pv = _dot(vt_ref[0, k * n_chunks + c], p.astype(BF16))
        return jnp.concatenate([pv, psum], axis=0)

    def scores(k, buf):
        for g in range(GROUP):
            smax = None
            for c in range(n_chunks):
                cmax = scores_chunk(g, k, c, buf)
                smax = cmax if smax is None else jnp.maximum(smax, cmax)
            buf[1][g] = jnp.broadcast_to(smax, (8, tq))

    def scores_and_accumulate(k_new, buf_new, k_old, buf_old):
        for g in range(GROUP):
            m_prev = m_scr[g]
            m_next = jnp.maximum(m_prev, buf_old[1][g])
            alpha = jnp.exp2(m_prev - m_next)
            smax = pv = None
            for c in range(n_chunks):
                cmax = scores_chunk(g, k_new, c, buf_new)
                part = values_chunk(g, k_old, c, buf_old, m_next)
                smax = cmax if smax is None else jnp.maximum(smax, cmax)
                pv = part if pv is None else pv + part
            buf_new[1][g] = jnp.broadcast_to(smax, (8, tq))
            acc_scr[g] = alpha[0:1, :] * acc_scr[g] + pv
            m_scr[g] = m_next

    def query_tile(t, first, second):
        def key_block_pair(j, carry):
            k = 2 * j
            scores_and_accumulate(k + 1, second, k, first)
            scores_and_accumulate(k + 2, first, k + 1, second)
            return carry

        lax.fori_loop(0, (n_kblk - 1) // 2, key_block_pair, 0, unroll=2)
        load_queries(jnp.minimum(t + 1, n_tiles - 1))
        scores_and_accumulate(0, second, n_kblk - 1, first)
        write_output(t)
        reset_state()

    buf_a, buf_b = (sa_scr, smaxa_scr), (sb_scr, smaxb_scr)
    load_queries(0)
    reset_state()
    scores(0, buf_a)

    def query_tile_pair(i, carry):
        query_tile(2 * i, buf_a, buf_b)
        query_tile(2 * i + 1, buf_b, buf_a)
        return carry

    lax.fori_loop(0, n_tiles // 2, query_tile_pair, 0)


def _attention(q, k, vt):
    b, seq, _ = q.shape
    lk = k.shape[1]
    tq, tk, n_tiles = ATT_TQ, ATT_TK, ATT_TILES
    n_kblk = lk // tk
    assert n_kblk % 2 == 1 and n_tiles % 2 == 0 and seq % (tq * n_tiles) == 0 and lk == n_kblk * tk
    assert tk % KEY_CHUNK == 0 and vt.shape == (b, lk // KEY_CHUNK, KV_SLAB, KEY_CHUNK)
    stat_rows = 8
    return pl.pallas_call(
        functools.partial(_attn_kernel, tq=tq, tk=tk, n_kblk=n_kblk, n_tiles=n_tiles),
        grid=(b, N_KV_HEADS, seq // (tq * n_tiles)),
        in_specs=[
            pl.BlockSpec((1, tq * n_tiles, KV_SLAB), lambda i, h, qi: (i, qi, h)),
            pl.BlockSpec((1, lk, 2 * HEAD_DIM), lambda i, h, qi: (i, 0, h // 2)),
            pl.BlockSpec((1, lk // KEY_CHUNK, HEAD_DIM, KEY_CHUNK), lambda i, h, qi: (i, 0, h, 0)),
        ],
        out_specs=pl.BlockSpec((1, n_tiles, KV_SLAB, tq), lambda i, h, qi: (i, qi, h, 0)),
        out_shape=jax.ShapeDtypeStruct((b, seq // tq, Q_DIM, tq), BF16),
        scratch_shapes=[
            pltpu.VMEM((GROUP, 2 * HEAD_DIM, tq), BF16),
            pltpu.VMEM((GROUP, stat_rows, tq), F32),
            pltpu.VMEM((GROUP, HEAD_DIM + DENOM_ROWS, tq), F32),
            pltpu.VMEM((GROUP, tk, tq), F32),
            pltpu.VMEM((GROUP, tk, tq), F32),
            pltpu.VMEM((GROUP, stat_rows, tq), F32),
            pltpu.VMEM((GROUP, stat_rows, tq), F32),
        ],
        compiler_params=_params(("parallel", "parallel", "parallel"), 60),
        name="gqa_attention",
    )(q, k, vt)


def _oproj_router_kernel(a_ref, x_ref, mod_ref, ng_ref, wo_ref, rw_ref, x_out, h_out, e_out, g_out, r_out, n_out,
                         count_scr):
    @pl.when(pl.program_id(0) == 0)
    def _():
        count_scr[...] = jnp.zeros_like(count_scr)

    g1 = mod_ref[0, 2:3, :]
    sh, sc = mod_ref[0, 3:4, :], mod_ref[0, 4:5, :]
    proj = lax.dot_general(a_ref[0], wo_ref[...], (((0,), (0,)), ((), ())), preferred_element_type=F32)
    x = x_ref[...] + g1 * proj
    x_out[...] = x
    h = _rms(x, ng_ref[...]) * (1.0 + sc) + sh
    _store_rows(h_out, h)
    logits = lax.dot_general(rw_ref[...], h, (((1,), (1,)), ((), ())),
                             preferred_element_type=F32, precision=lax.Precision.HIGHEST)
    eid = lax.broadcasted_iota(jnp.int32, logits.shape, 0)
    m1 = jnp.max(logits, axis=0, keepdims=True)
    i1 = jnp.min(jnp.where(logits == m1, eid, N_EXPERTS), axis=0, keepdims=True)
    rest = jnp.where(eid == i1, -jnp.inf, logits)
    m2 = jnp.max(rest, axis=0, keepdims=True)
    i2 = jnp.min(jnp.where(rest == m2, eid, N_EXPERTS), axis=0, keepdims=True)
    e = jnp.exp(m2 - m1)
    e_out[...] = jnp.concatenate([i1, i2], axis=0)
    g_out[...] = jnp.concatenate([1.0 / (1.0 + e), e / (1.0 + e)], axis=0)
    tm = logits.shape[1]
    chose = jnp.where((eid == i1) | (eid == i2), 1.0, 0.0)
    earlier = (lax.broadcasted_iota(jnp.int32, (tm, tm), 0) < lax.broadcasted_iota(jnp.int32, (tm, tm), 1))
    before = _dot(chose.astype(BF16), jnp.where(earlier, 1.0, 0.0).astype(BF16)) + count_scr[:, 0:1]
    r1 = jnp.sum(jnp.where(eid == i1, before, 0.0), axis=0, keepdims=True)
    r2 = jnp.sum(jnp.where(eid == i2, before, 0.0), axis=0, keepdims=True)
    r_out[...] = jnp.concatenate([r1, r2], axis=0).astype(jnp.int32)
    count_scr[...] += jnp.sum(chose, axis=1, keepdims=True)
    n_out[...] = count_scr[...].astype(jnp.int32)


def _oproj_router_layer(attn_t, x, mods, norm_g, w_o, router_w):
    b, seq, d = x.shape
    t = b * seq
    tm = ROW_TILE
    per_batch = seq // tm
    row = lambda i: (i, 0)
    fixed = lambda i: (0, 0)
    return pl.pallas_call(
        _oproj_router_kernel,
        grid=(t // tm,),
        in_specs=[
            pl.BlockSpec((1, d, tm), lambda i: (i, 0, 0)),
            pl.BlockSpec((tm, d), row),
            pl.BlockSpec((1, N_MOD, d), lambda i: (i // per_batch, 0, 0)),
            pl.BlockSpec((1, d), fixed),
            pl.BlockSpec((d, d), fixed),
            pl.BlockSpec((N_EXPERTS, d), fixed),
        ],
        out_specs=[
            pl.BlockSpec((tm, d), row),
            pl.BlockSpec((tm * ROW_SLABS, LANES), row),
            pl.BlockSpec((TOP_K, tm), lambda i: (0, i)),
            pl.BlockSpec((TOP_K, tm), lambda i: (0, i)),
            pl.BlockSpec((TOP_K, tm), lambda i: (0, i)),
            pl.BlockSpec((N_EXPERTS, LANES), fixed),
        ],
        out_shape=[
            jax.ShapeDtypeStruct((t, d), F32),
            jax.ShapeDtypeStruct((t * ROW_SLABS, LANES), F32),
            jax.ShapeDtypeStruct((TOP_K, t), jnp.int32),
            jax.ShapeDtypeStruct((TOP_K, t), F32),
            jax.ShapeDtypeStruct((TOP_K, t), jnp.int32),
            jax.ShapeDtypeStruct((N_EXPERTS, LANES), jnp.int32),
        ],
        scratch_shapes=[pltpu.VMEM((N_EXPERTS, LANES), F32)],
        compiler_params=_params(("arbitrary",), 48),
        name="oproj_router",
    )(attn_t.reshape(t // tm, d, tm), x.reshape(t, d), mods, norm_g.reshape(1, d), w_o, router_w.T)


def _row_copy(src_ref, dst_ref, src_row, dst_row, sem):
    return pltpu.make_async_copy(_tile_of_row(src_ref, src_row), _tile_of_row(dst_ref, dst_row), sem)


def _rows_copy(src_ref, dst_ref, rows, sem):
    return pltpu.make_async_copy(src_ref.at[pl.ds(0, rows * ROW_SLABS)], dst_ref.at[pl.ds(0, rows * ROW_SLABS)], sem)


def _dest_spec(tm):
    return pl.BlockSpec((1, 1, TOP_K * tm), lambda i: (i, 0, 0), memory_space=pltpu.SMEM)


def _dispatch_kernel(dest_ref, zero_ref, h_ref, xs_ref, zeros_scr, sem):
    tm = h_ref.shape[0] // ROW_SLABS

    @pl.when(pl.program_id(0) == 0)
    def _():
        zeros_scr[...] = jnp.zeros_like(zeros_scr)
        block_rows = MOE_ROWS * ROW_SLABS

        def zero_block(j):
            return pltpu.make_async_copy(
                zeros_scr, xs_ref.at[pl.ds(pl.multiple_of(j * block_rows, block_rows), block_rows)], sem)

        def start(j, carry):
            @pl.when(zero_ref[j] != 0)
            def _():
                zero_block(j).start()
            return carry

        def drain(j, carry):
            @pl.when(zero_ref[j] != 0)
            def _():
                zero_block(j).wait()
            return carry

        lax.fori_loop(0, zero_ref.shape[0], start, 0)
        lax.fori_loop(0, zero_ref.shape[0], drain, 0)

    def issue(r, carry):
        for k in range(TOP_K):
            _row_copy(h_ref, xs_ref, r, dest_ref[0, 0, TOP_K * r + k], sem).start(priority=k)
        return carry

    lax.fori_loop(0, tm, issue, 0)
    for _ in range(TOP_K):
        _rows_copy(h_ref, xs_ref, tm, sem).wait()


def _dispatch(h, dest, zero_blocks, cap):
    t = h.shape[0] // ROW_SLABS
    tm = DISPATCH_TILE
    return pl.pallas_call(
        _dispatch_kernel,
        grid=(t // tm,),
        in_specs=[
            _dest_spec(tm),
            pl.BlockSpec(memory_space=pltpu.SMEM),
            pl.BlockSpec((tm * ROW_SLABS, LANES), lambda i: (i, 0)),
        ],
        out_specs=pl.BlockSpec(memory_space=pl.ANY),
        out_shape=jax.ShapeDtypeStruct((cap * ROW_SLABS, LANES), h.dtype),
        scratch_shapes=[pltpu.VMEM((MOE_ROWS * ROW_SLABS, LANES), h.dtype), pltpu.SemaphoreType.DMA(())],
        compiler_params=_params(("arbitrary",), 32),
        name="moe_dispatch",
    )(dest.reshape(t // tm, 1, TOP_K * tm), zero_blocks, h)


def _ffn_expert_kernel(blk_ref, x_ref, w1_ref, w3_ref, w2_ref, o_ref):
    in_use = pl.program_id(0) < blk_ref[pl.num_programs(0)]

    @pl.when(in_use)
    def _():
        h = _load_rows(x_ref, MOE_ROWS).astype(BF16)
        _store_rows(o_ref, _swiglu(h, w1_ref.at[0], w3_ref.at[0], w2_ref.at[0]))

    @pl.when(jnp.logical_not(in_use))
    def _():
        o_ref[...] = jnp.zeros_like(o_ref)


def _ffn_experts(xs, blk_e, w1, w3, w2):
    cap = xs.shape[0] // ROW_SLABS
    d = D_MODEL
    n_blk = cap // MOE_ROWS
    row_spec = pl.BlockSpec((MOE_ROWS * ROW_SLABS, LANES), lambda i, e: (i, 0))
    grid_spec = pltpu.PrefetchScalarGridSpec(
        num_scalar_prefetch=1,
        grid=(n_blk,),
        in_specs=[
            row_spec,
            pl.BlockSpec((1, d, D_FF), lambda i, e: (e[i], 0, 0)),
            pl.BlockSpec((1, d, D_FF), lambda i, e: (e[i], 0, 0)),
            pl.BlockSpec((1, D_FF, d), lambda i, e: (e[i], 0, 0)),
        ],
        out_specs=row_spec,
    )
    return pl.pallas_call(
        _ffn_expert_kernel,
        grid_spec=grid_spec,
        out_shape=jax.ShapeDtypeStruct(xs.shape, F32),
        compiler_params=_params(("arbitrary",), 60),
        name="ffn_experts",
    )(blk_e, xs, w1, w3, w2)


def _routing_plan(e_idx, ranks, sizes):
    n_assign = e_idx.size
    padded = (sizes + MOE_ROWS - 1) // MOE_ROWS * MOE_ROWS
    ends = jnp.cumsum(padded)
    pstart = ends - padded
    seg_start = sum(jnp.where(e_idx == e, pstart[e], 0) for e in range(N_EXPERTS))
    dest = (ranks + seg_start).T.reshape(-1).astype(jnp.int32)
    cap = n_assign + N_EXPERTS * MOE_ROWS
    blk_start = jnp.arange(cap // MOE_ROWS, dtype=jnp.int32) * MOE_ROWS
    blk_e = jnp.sum((ends[None, :] <= blk_start[:, None]).astype(jnp.int32), axis=1)
    blk_e = jnp.minimum(blk_e, N_EXPERTS - 1)
    real_end = jnp.where(blk_start < ends[-1], (pstart + sizes)[blk_e], 0)
    zero_blocks = (real_end - blk_start < MOE_ROWS).astype(jnp.int32)
    blk_info = jnp.concatenate([blk_e, ends[-1:] // MOE_ROWS]).astype(jnp.int32)
    return dest, blk_info, zero_blocks, cap


def _final_kernel(dest_ref, dest_next_ref, x_ref, gt_ref, mod_ref, fg_ref, y_ref, o_ref, y_buf, sems):
    tm = x_ref.shape[0]
    i = pl.program_id(0)
    slot = lax.rem(i, 2)
    half = TOP_K * tm

    def fetch(idx_ref, into):
        def issue(r, carry):
            for k in range(TOP_K):
                _row_copy(y_ref, y_buf, idx_ref[0, 0, TOP_K * r + k], into * half + k * tm + r,
                          sems.at[into]).start(priority=k)
            return carry

        lax.fori_loop(0, tm, issue, 0, unroll=8)

    @pl.when(i == 0)
    def _():
        fetch(dest_ref, slot)

    @pl.when(i + 1 < pl.num_programs(0))
    def _():
        fetch(dest_next_ref, 1 - slot)

    _rows_copy(y_ref, y_buf, half, sems.at[slot]).wait()
    gt = gt_ref[...]
    moe = (gt[:, 0:1] * _load_rows(y_buf, tm, slot * half) + gt[:, 1:2] * _load_rows(y_buf, tm, slot * half + tm))
    x = x_ref[...] + mod_ref[0, 5:6, :] * moe
    o_ref[...] = _rms(x, fg_ref[...])


def _final_layer(x, ys, dest, gates_t, mods, final_g, seq):
    t, d = x.shape
    tm = ROW_TILE
    per_batch = seq // tm
    n_steps = t // tm
    dest_tiles = dest.reshape(n_steps, 1, TOP_K * tm)
    return pl.pallas_call(
        _final_kernel,
        grid=(n_steps,),
        in_specs=[
            _dest_spec(tm),
            pl.BlockSpec((1, 1, TOP_K * tm), lambda i: (jnp.minimum(i + 1, n_steps - 1), 0, 0),
                         memory_space=pltpu.SMEM),
            pl.BlockSpec((tm, d), lambda i: (i, 0)),
            pl.BlockSpec((tm, TOP_K), lambda i: (i, 0)),
            pl.BlockSpec((1, N_MOD, d), lambda i: (i // per_batch, 0, 0)),
            pl.BlockSpec((1, d), lambda i: (0, 0)),
            pl.BlockSpec(memory_space=pl.ANY),
        ],
        out_specs=pl.BlockSpec((tm, d), lambda i: (i, 0)),
        out_shape=jax.ShapeDtypeStruct((t, d), F32),
        scratch_shapes=[pltpu.VMEM((2 * TOP_K * tm * ROW_SLABS, LANES), F32), pltpu.SemaphoreType.DMA((2,))],
        compiler_params=_params(("arbitrary",), 48),
        name="moe_sum_final_norm",
    )(dest_tiles, dest_tiles, x, gates_t, mods, final_g.reshape(1, d), ys)


def kernel(x, c, ctx, c_ctx, ada_w, ada_b, norm_g, pool_w, pool_scale, ffn_w1, ffn_w3, ffn_w2, w_qkv, w_o,
           q_norm_g, k_norm_g, router_w, moe_w1, moe_w3, moe_w2, final_g):
    b, seq, d = x.shape
    assert ada_w.shape[0] == 2 and b <= 7, "written for the two-layer block (pool+dense, attention+experts)"

    cvecs = jnp.zeros((8, d), F32).at[:b].set(c).at[b].set(c_ctx)
    mods = _adaln(cvecs, ada_w, ada_b).reshape(2, 8, N_MOD, d)
    mods_x = mods[:, :b]
    mods_c = jnp.broadcast_to(mods[:, b:b + 1], (2, b, N_MOD, d))

    pool_wb = pool_w[0].astype(BF16)
    w1b, w3b, w2b = ffn_w1[0].astype(BF16), ffn_w3[0].astype(BF16), ffn_w2[0].astype(BF16)
    x = _pool_layer(x, mods_x[0], norm_g[0, 0], pool_wb, pool_scale[0])
    ctx = _pool_layer(ctx, mods_c[0], norm_g[0, 0], pool_wb, pool_scale[0])
    x = _ffn_dense_layer(x, mods_x[0], norm_g[0, 1], w1b, w3b, w2b)
    ctx = _ffn_dense_layer(ctx, mods_c[0], norm_g[0, 1], w1b, w3b, w2b)

    wqkv = w_qkv[0].astype(BF16)
    lk = seq + ctx.shape[1]
    kv = (jnp.zeros((b, lk, KV_SLAB), BF16), jnp.zeros((b, lk // KEY_CHUNK, KV_SLAB, KEY_CHUNK), BF16))
    q, *kv = _qkv_layer(x, mods_x[1], norm_g[1, 0], wqkv, q_norm_g[0], k_norm_g[0], _rope_tables(seq), 0, kv)
    _, k_all, vt = _qkv_layer(ctx, mods_c[1], norm_g[1, 0], wqkv[:, Q_DIM:], q_norm_g[0], k_norm_g[0], None, seq, kv)
    attn_t = _attention(q, k_all, vt)

    assert ATT_TQ == ROW_TILE, "the output projection reads one transposed attention tile per row tile"
    x3, hf, e_idx, gates, ranks, counts = _oproj_router_layer(
        attn_t, x, mods_x[1], norm_g[1, 1], w_o[0].astype(BF16), router_w[0])
    dest, blk_e, zero_blocks, cap = _routing_plan(e_idx, ranks, counts[:, 0])
    xs = _dispatch(hf, dest, zero_blocks, cap)
    ys = _ffn_experts(xs, blk_e, moe_w1[0].astype(BF16), moe_w3[0].astype(BF16), moe_w2[0].astype(BF16))
    out = _final_layer(x3, ys, dest, gates.T, mods_x[1], final_g, seq)
    return out.reshape(b, seq, d)
```

```python
import functools
import math

import jax
import jax.numpy as jnp
from jax import lax
from jax.experimental import pallas as pl
from jax.experimental.pallas import tpu as pltpu

F32 = jnp.float32
BF16 = jnp.bfloat16

D_MODEL = 1024
GRID_W = 64
POOL_WINDOWS = (2, 4, 8, 16)
POOL_GROUP_DIM = D_MODEL // len(POOL_WINDOWS)
HEAD_DIM = 64
N_KV_HEADS = 4
GROUP = 4
KV_SLAB = N_KV_HEADS * HEAD_DIM
Q_DIM = N_KV_HEADS * GROUP * HEAD_DIM
ROPE_THETA = 10000.0
Q_SCALE = HEAD_DIM ** -0.5 * math.log2(math.e)
D_FF = 2816
N_EXPERTS = 8
TOP_K = 2
NORM_EPS = 1e-6
N_MOD = 6

POOL_HALO = 16
ROW_TILE = 512
MXU_TILE = 256
FF_CHUNKS = ((0, 6 * MXU_TILE), (6 * MXU_TILE, D_FF))
MOE_ROWS = 512
DISPATCH_TILE = 2048
ATT_TQ = 512
ATT_TK = 1280
KEY_CHUNK = 256
DENOM_ROWS = 8
ATT_TILES = 16
MIB = 1024 * 1024
LANES = 128
ROW_SLABS = D_MODEL // LANES


def _dot(a, b):
    return jnp.dot(a, b, preferred_element_type=F32)


def _rms(x, g):
    ms = jnp.mean(x * x, axis=-1, keepdims=True)
    return x * lax.rsqrt(ms + NORM_EPS) * g


def _silu(x):
    return x * (1.0 / (1.0 + jnp.exp(-x)))


def _load_rows(ref, n_rows, first=0):
    return jnp.concatenate(
        [ref[pl.ds(ROW_SLABS * first + s, n_rows, stride=ROW_SLABS), :] for s in range(ROW_SLABS)], axis=1)


def _store_rows(ref, val):
    for s in range(ROW_SLABS):
        ref[pl.ds(s, val.shape[0], stride=ROW_SLABS), :] = val[:, s * LANES:(s + 1) * LANES]


def _tile_of_row(ref, row):
    return ref.at[pl.ds(pl.multiple_of(row * ROW_SLABS, ROW_SLABS), ROW_SLABS)]


def _params(semantics, vmem_mib):
    return pltpu.CompilerParams(dimension_semantics=semantics, vmem_limit_bytes=vmem_mib * MIB)


def _adaln_kernel(c_ref, w_ref, b_ref, o_ref):
    s = _silu(c_ref[...])
    o_ref[0] = jnp.dot(s, w_ref[0], preferred_element_type=F32, precision=lax.Precision.HIGHEST) + b_ref[0]


def _adaln(cvecs, ada_w, ada_b):
    depth, d, n = ada_w.shape
    tn = n // 4
    return pl.pallas_call(
        _adaln_kernel,
        grid=(depth, n // tn),
        in_specs=[
            pl.BlockSpec((8, d), lambda i, j: (0, 0)),
            pl.BlockSpec((1, d, tn), lambda i, j: (i, 0, j)),
            pl.BlockSpec((1, 1, tn), lambda i, j: (i, 0, j)),
        ],
        out_specs=pl.BlockSpec((1, 8, tn), lambda i, j: (i, 0, j)),
        out_shape=jax.ShapeDtypeStruct((depth, 8, n), F32),
        compiler_params=_params(("parallel", "parallel"), 32),
        name="adaln",
    )(cvecs, ada_w, ada_b.reshape(depth, 1, n))


def _pool_kernel(xp_ref, xc_ref, xn_ref, mod_ref, ng_ref, pw_ref, ps_ref, o_ref, *, seq, tm):
    t = pl.program_id(1)
    xc = xc_ref[0]
    xa = jnp.concatenate([xp_ref[0], xc, xn_ref[0]], axis=0)
    sh, sc, gate = mod_ref[0, 0:1, :], mod_ref[0, 1:2, :], mod_ref[0, 2:3, :]
    h = _rms(xa, ng_ref[...]) * (1.0 + sc) + sh
    hb = h.astype(BF16)
    wide = tm + 2 * POOL_HALO
    row = lax.broadcasted_iota(jnp.int32, (tm, wide), 0) + t * tm
    col = lax.broadcasted_iota(jnp.int32, (tm, wide), 1) + (t * tm - POOL_HALO)
    row1 = lax.broadcasted_iota(jnp.int32, (tm, 1), 0) + t * tm
    ys = []
    for g, w in enumerate(POOL_WINDOWS):
        half = w // 2
        lo = jnp.maximum(row - half, 0)
        hi = jnp.minimum(row + half, seq)
        band = jnp.where((col >= lo) & (col < hi), 1.0, 0.0).astype(BF16)
        cnt = (jnp.minimum(row1 + half, seq) - jnp.maximum(row1 - half, 0)).astype(F32)
        lanes = slice(g * POOL_GROUP_DIM, (g + 1) * POOL_GROUP_DIM)
        win_sum = _dot(band, hb[:, lanes])
        d = win_sum / cnt - h[POOL_HALO:POOL_HALO + tm, lanes]
        ys.append(_dot(d.astype(BF16), pw_ref[g]))
    y = jnp.concatenate(ys, axis=1) * ps_ref[...]
    o_ref[0] = xc + gate * y


def _pool_layer(x, mods, norm_g, pool_w, pool_scale):
    b, seq, d = x.shape
    tm = min(ROW_TILE, seq)
    per = tm // POOL_HALO
    last = seq // POOL_HALO - 1
    return pl.pallas_call(
        functools.partial(_pool_kernel, seq=seq, tm=tm),
        grid=(b, seq // tm),
        in_specs=[
            pl.BlockSpec((1, POOL_HALO, d), lambda i, t: (i, jnp.maximum(t * per - 1, 0), 0)),
            pl.BlockSpec((1, tm, d), lambda i, t: (i, t, 0)),
            pl.BlockSpec((1, POOL_HALO, d), lambda i, t: (i, jnp.minimum((t + 1) * per, last), 0)),
            pl.BlockSpec((1, N_MOD, d), lambda i, t: (i, 0, 0)),
            pl.BlockSpec((1, d), lambda i, t: (0, 0)),
            pl.BlockSpec(pool_w.shape, lambda i, t: (0, 0, 0)),
            pl.BlockSpec((1, d), lambda i, t: (0, 0)),
        ],
        out_specs=pl.BlockSpec((1, tm, d), lambda i, t: (i, t, 0)),
        out_shape=jax.ShapeDtypeStruct(x.shape, F32),
        compiler_params=_params(("parallel", "parallel"), 48),
        name="pool_mix",
    )(x, x, x, mods, norm_g.reshape(1, d), pool_w, pool_scale.reshape(1, d))


def _swiglu(h, w1_ref, w3_ref, w2_ref):
    acc = None
    for start, stop in FF_CHUNKS:
        cols = slice(start, stop)
        u = _silu(_dot(h, w1_ref[:, cols])) * _dot(h, w3_ref[:, cols])
        part = _dot(u.astype(BF16), w2_ref[cols, :])
        acc = part if acc is None else acc + part
    return acc


def _ffn_dense_kernel(x_ref, mod_ref, ng_ref, w1_ref, w3_ref, w2_ref, o_ref):
    sh, sc, gate = mod_ref[0, 3:4, :], mod_ref[0, 4:5, :], mod_ref[0, 5:6, :]
    x = x_ref[...]
    h = (_rms(x, ng_ref[...]) * (1.0 + sc) + sh).astype(BF16)
    o_ref[...] = x + gate * _swiglu(h, w1_ref, w3_ref, w2_ref)


def _ffn_dense_layer(x, mods, norm_g, w1, w3, w2):
    b, seq, d = x.shape
    t = b * seq
    tm = min(ROW_TILE, seq)
    per_batch = seq // tm
    fixed = lambda i: (0, 0)
    out = pl.pallas_call(
        _ffn_dense_kernel,
        grid=(t // tm,),
        in_specs=[
            pl.BlockSpec((tm, d), lambda i: (i, 0)),
            pl.BlockSpec((1, N_MOD, d), lambda i: (i // per_batch, 0, 0)),
            pl.BlockSpec((1, d), fixed),
            pl.BlockSpec((d, D_FF), fixed),
            pl.BlockSpec((d, D_FF), fixed),
            pl.BlockSpec((D_FF, d), fixed),
        ],
        out_specs=pl.BlockSpec((tm, d), lambda i: (i, 0)),
        out_shape=jax.ShapeDtypeStruct((t, d), F32),
        compiler_params=_params(("parallel",), 60),
        name="ffn_dense",
    )(x.reshape(t, d), mods, norm_g.reshape(1, d), w1, w3, w2)
    return out.reshape(b, seq, d)


def _head_norm_rope(z, ones_blk, gain, cos, sin):
    sq = z * z
    hi = sq.astype(BF16)
    lo = (sq - hi.astype(F32)).astype(BF16)
    ss = _dot(hi, ones_blk) + _dot(lo, ones_blk)
    zn = z * lax.rsqrt(ss * (1.0 / HEAD_DIM) + NORM_EPS) * gain
    if cos is None:
        return zn
    width = z.shape[1]
    fwd = pltpu.roll(zn, width - 16, 1)
    bwd = pltpu.roll(zn, 16, 1)
    lane = lax.broadcasted_iota(jnp.int32, zn.shape, 1)
    partner = jnp.where((lane & 31) < 16, fwd, bwd)
    return zn * cos + partner * sin


def _qkv_kernel(*refs, n_q, rope, n_aliased):
    x_ref, mod_ref, ng_ref, w_ref, e_ref, qg_ref, kg_ref = refs[:7]
    refs = refs[7:]
    if rope:
        cos_ref, sin_ref = refs[:2]
        refs = refs[2:]
        cos, sin = cos_ref[...], sin_ref[...]
    else:
        cos = sin = None
    refs = refs[n_aliased:]
    sh, sc = mod_ref[0, 0:1, :], mod_ref[0, 1:2, :]
    hb = (_rms(x_ref[0], ng_ref[...]) * (1.0 + sc) + sh).astype(BF16)
    qkv = _dot(hb, w_ref[...])
    ones_blk = e_ref[...]
    if n_q:
        q_ref, k_ref, v_ref = refs
        for c in range(n_q):
            z = _head_norm_rope(qkv[:, c * KV_SLAB:(c + 1) * KV_SLAB], ones_blk, qg_ref[...], cos, sin)
            q_ref[0, :, c * KV_SLAB:(c + 1) * KV_SLAB] = (z * Q_SCALE).astype(BF16)
    else:
        k_ref, v_ref = refs
    base = n_q * KV_SLAB
    k_ref[0] = _head_norm_rope(qkv[:, base:base + KV_SLAB], ones_blk, kg_ref[...], cos, sin).astype(BF16)
    v = qkv[:, base + KV_SLAB:base + 2 * KV_SLAB]
    for j in range(v_ref.shape[1]):
        v_ref[0, j] = v[j * KEY_CHUNK:(j + 1) * KEY_CHUNK, :].T.astype(BF16)


def _qkv_layer(x, mods, norm_g, w, q_gain, k_gain, tables, key_offset, kv_buffers):
    b, seq, d = x.shape
    tm = min(ROW_TILE, seq)
    width = w.shape[1]
    n_q = (width - 2 * KV_SLAB) // KV_SLAB
    rope = tables is not None
    total_keys = kv_buffers[0].shape[1]
    assert tm % KEY_CHUNK == 0 and key_offset % tm == 0 and total_keys % KEY_CHUNK == 0
    seg = lax.broadcasted_iota(jnp.int32, (KV_SLAB, KV_SLAB), 0) // HEAD_DIM
    ones_blk = (seg == seg.T).astype(BF16)
    in_specs = [
        pl.BlockSpec((1, tm, d), lambda i, t: (i, t, 0)),
        pl.BlockSpec((1, N_MOD, d), lambda i, t: (i, 0, 0)),
        pl.BlockSpec((1, d), lambda i, t: (0, 0)),
        pl.BlockSpec((d, width), lambda i, t: (0, 0)),
        pl.BlockSpec((KV_SLAB, KV_SLAB), lambda i, t: (0, 0)),
        pl.BlockSpec((1, KV_SLAB), lambda i, t: (0, 0)),
        pl.BlockSpec((1, KV_SLAB), lambda i, t: (0, 0)),
    ]
    args = [x, mods, norm_g.reshape(1, d), w, ones_blk,
            jnp.tile(q_gain, N_KV_HEADS).reshape(1, KV_SLAB), jnp.tile(k_gain, N_KV_HEADS).reshape(1, KV_SLAB)]
    if rope:
        in_specs += [pl.BlockSpec((tm, KV_SLAB), lambda i, t: (t, 0))] * 2
        args += list(tables)
    first_tile, chunks = key_offset // tm, tm // KEY_CHUNK
    out_specs = [pl.BlockSpec((1, tm, KV_SLAB), lambda i, t: (i, first_tile + t, 0)),
                 pl.BlockSpec((1, chunks, KV_SLAB, KEY_CHUNK), lambda i, t: (i, first_tile + t, 0, 0))]
    out_shape = [jax.ShapeDtypeStruct((b, total_keys, KV_SLAB), BF16),
                 jax.ShapeDtypeStruct((b, total_keys // KEY_CHUNK, KV_SLAB, KEY_CHUNK), BF16)]
    if n_q:
        out_specs = [pl.BlockSpec((1, tm, n_q * KV_SLAB), lambda i, t: (i, t, 0))] + out_specs
        out_shape = [jax.ShapeDtypeStruct((b, seq, n_q * KV_SLAB), BF16)] + out_shape
    aliases = {len(args) + j: len(out_shape) - 2 + j for j in range(2)}
    in_specs += [pl.BlockSpec(memory_space=pl.ANY)] * 2
    args += list(kv_buffers)
    outs = pl.pallas_call(
        functools.partial(_qkv_kernel, n_q=n_q, rope=rope, n_aliased=len(aliases)),
        grid=(b, seq // tm),
        in_specs=in_specs,
        out_specs=out_specs,
        out_shape=out_shape,
        input_output_aliases=aliases,
        compiler_params=_params(("parallel", "parallel"), 48),
        name="qkv_rope" if rope else "kv_ctx",
    )(*args)
    return outs if n_q else [None] + list(outs)


def _rope_tables(seq):
    half = HEAD_DIM // 4
    t = jnp.arange(seq, dtype=jnp.int32)
    pos = jnp.stack([t // GRID_W, t % GRID_W], axis=1).astype(F32)
    freqs = ROPE_THETA ** (-jnp.arange(half, dtype=F32) / half)
    ang = pos[:, :, None] * freqs[None, None, :]
    cos = jnp.cos(ang)
    sin = jnp.sin(ang)
    cos_h = jnp.concatenate([cos, cos], axis=2).reshape(seq, HEAD_DIM)
    sin_h = jnp.concatenate([-sin, sin], axis=2).reshape(seq, HEAD_DIM)
    return jnp.tile(cos_h, (1, N_KV_HEADS)), jnp.tile(sin_h, (1, N_KV_HEADS))


def _attn_kernel(q_ref, k_ref, vt_ref, o_ref, qt_scr, m_scr, acc_scr, sa_scr, sb_scr, smaxa_scr, smaxb_scr, *,
                 tq, tk, n_kblk, n_tiles):
    odd_head = lax.rem(pl.program_id(1), 2) == 1

    def load_queries(t):
        rows = pl.ds(pl.multiple_of(t * tq, tq), tq)
        for j in range(GROUP // 2):
            pair_t = q_ref[0, rows, j * 2 * HEAD_DIM:(j + 1) * 2 * HEAD_DIM].astype(F32).T
            for h in range(2):
                qt = pair_t[h * HEAD_DIM:(h + 1) * HEAD_DIM, :]
                zero = jnp.zeros_like(qt)
                low, high = jnp.concatenate([qt, zero], axis=0), jnp.concatenate([zero, qt], axis=0)
                qt_scr[2 * j + h] = jnp.where(odd_head, high, low).astype(BF16)

    def reset_state():
        m_scr[...] = jnp.full_like(m_scr, -jnp.inf)
        acc_scr[...] = jnp.zeros_like(acc_scr)

    def write_output(t):
        for g in range(GROUP):
            acc = acc_scr[g]
            out = acc[:HEAD_DIM] / jnp.sum(acc[HEAD_DIM:], axis=0, keepdims=True)
            o_ref[0, t, g * HEAD_DIM:(g + 1) * HEAD_DIM, :] = out.astype(BF16)

    n_chunks = tk // KEY_CHUNK

    def scores_chunk(g, k, c, buf):
        keys = k_ref[0, pl.ds(pl.multiple_of(k * tk + c * KEY_CHUNK, KEY_CHUNK), KEY_CHUNK), :]
        s = _dot(keys, qt_scr[g])
        buf[0][g, c * KEY_CHUNK:(c + 1) * KEY_CHUNK, :] = s
        return jnp.max(s, axis=0, keepdims=True)

    def values_chunk(g, k, c, buf, m_next):
        p = jnp.exp2(buf[0][g, c * KEY_CHUNK:(c + 1) * KEY_CHUNK, :] - m_next[0:1, :])
        psum = jnp.sum(p.reshape(KEY_CHUNK // DENOM_ROWS, DENOM_ROWS, tq), axis=0)
        pv = _dot(vt_ref[0, k * n_chunks + c], p.astype(BF16))
        return jnp.concatenate([pv, psum], axis=0)

    def scores(k, buf):
        for g in range(GROUP):
            smax = None
            for c in range(n_chunks):
                cmax = scores_chunk(g, k, c, buf)
                smax = cmax if smax is None else jnp.maximum(smax, cmax)
            buf[1][g] = jnp.broadcast_to(smax, (8, tq))

    def scores_and_accumulate(k_new, buf_new, k_old, buf_old):
        for g in range(GROUP):
            m_prev = m_scr[g]
            m_next = jnp.maximum(m_prev, buf_old[1][g])
            alpha = jnp.exp2(m_prev - m_next)
            smax = pv = None
            for c in range(n_chunks):
                cmax = scores_chunk(g, k_new, c, buf_new)
                part = values_chunk(g, k_old, c, buf_old, m_next)
                smax = cmax if smax is None else jnp.maximum(smax, cmax)
                pv = part if pv is None else pv + part
            buf_new[1][g] = jnp.broadcast_to(smax, (8, tq))
            acc_scr[g] = alpha[0:1, :] * acc_scr[g] + pv
            m_scr[g] = m_next

    def query_tile(t, first, second):
        def key_block_pair(j, carry):
            k = 2 * j
            scores_and_accumulate(k + 1, second, k, first)
            scores_and_accumulate(k + 2, first, k + 1, second)
            return carry

        lax.fori_loop(0, (n_kblk - 1) // 2, key_block_pair, 0, unroll=2)
        load_queries(jnp.minimum(t + 1, n_tiles - 1))
        scores_and_accumulate(0, second, n_kblk - 1, first)
        write_output(t)
        reset_state()

    buf_a, buf_b = (sa_scr, smaxa_scr), (sb_scr, smaxb_scr)
    load_queries(0)
    reset_state()
    scores(0, buf_a)

    def query_tile_pair(i, carry):
        query_tile(2 * i, buf_a, buf_b)
        query_tile(2 * i + 1, buf_b, buf_a)
        return carry

    lax.fori_loop(0, n_tiles // 2, query_tile_pair, 0)


def _attention(q, k, vt):
    b, seq, _ = q.shape
    lk = k.shape[1]
    tq, tk, n_tiles = ATT_TQ, ATT_TK, ATT_TILES
    n_kblk = lk // tk
    assert n_kblk % 2 == 1 and n_tiles % 2 == 0 and seq % (tq * n_tiles) == 0 and lk == n_kblk * tk
    assert tk % KEY_CHUNK == 0 and vt.shape == (b, lk // KEY_CHUNK, KV_SLAB, KEY_CHUNK)
    stat_rows = 8
    return pl.pallas_call(
        functools.partial(_attn_kernel, tq=tq, tk=tk, n_kblk=n_kblk, n_tiles=n_tiles),
        grid=(b, N_KV_HEADS, seq // (tq * n_tiles)),
        in_specs=[
            pl.BlockSpec((1, tq * n_tiles, KV_SLAB), lambda i, h, qi: (i, qi, h)),
            pl.BlockSpec((1, lk, 2 * HEAD_DIM), lambda i, h, qi: (i, 0, h // 2)),
            pl.BlockSpec((1, lk // KEY_CHUNK, HEAD_DIM, KEY_CHUNK), lambda i, h, qi: (i, 0, h, 0)),
        ],
        out_specs=pl.BlockSpec((1, n_tiles, KV_SLAB, tq), lambda i, h, qi: (i, qi, h, 0)),
        out_shape=jax.ShapeDtypeStruct((b, seq // tq, Q_DIM, tq), BF16),
        scratch_shapes=[
            pltpu.VMEM((GROUP, 2 * HEAD_DIM, tq), BF16),
            pltpu.VMEM((GROUP, stat_rows, tq), F32),
            pltpu.VMEM((GROUP, HEAD_DIM + DENOM_ROWS, tq), F32),
            pltpu.VMEM((GROUP, tk, tq), F32),
            pltpu.VMEM((GROUP, tk, tq), F32),
            pltpu.VMEM((GROUP, stat_rows, tq), F32),
            pltpu.VMEM((GROUP, stat_rows, tq), F32),
        ],
        compiler_params=_params(("parallel", "parallel", "parallel"), 60),
        name="gqa_attention",
    )(q, k, vt)


def _oproj_router_kernel(a_ref, x_ref, mod_ref, ng_ref, wo_ref, rw_ref, x_out, h_out, e_out, g_out, r_out, n_out,
                         count_scr):
    @pl.when(pl.program_id(0) == 0)
    def _():
        count_scr[...] = jnp.zeros_like(count_scr)

    g1 = mod_ref[0, 2:3, :]
    sh, sc = mod_ref[0, 3:4, :], mod_ref[0, 4:5, :]
    proj = lax.dot_general(a_ref[0], wo_ref[...], (((0,), (0,)), ((), ())), preferred_element_type=F32)
    x = x_ref[...] + g1 * proj
    x_out[...] = x
    h = _rms(x, ng_ref[...]) * (1.0 + sc) + sh
    _store_rows(h_out, h)
    logits = lax.dot_general(rw_ref[...], h, (((1,), (1,)), ((), ())),
                             preferred_element_type=F32, precision=lax.Precision.HIGHEST)
    eid = lax.broadcasted_iota(jnp.int32, logits.shape, 0)
    m1 = jnp.max(logits, axis=0, keepdims=True)
    i1 = jnp.min(jnp.where(logits == m1, eid, N_EXPERTS), axis=0, keepdims=True)
    rest = jnp.where(eid == i1, -jnp.inf, logits)
    m2 = jnp.max(rest, axis=0, keepdims=True)
    i2 = jnp.min(jnp.where(rest == m2, eid, N_EXPERTS), axis=0, keepdims=True)
    e = jnp.exp(m2 - m1)
    e_out[...] = jnp.concatenate([i1, i2], axis=0)
    g_out[...] = jnp.concatenate([1.0 / (1.0 + e), e / (1.0 + e)], axis=0)
    tm = logits.shape[1]
    chose = jnp.where((eid == i1) | (eid == i2), 1.0, 0.0)
    earlier = (lax.broadcasted_iota(jnp.int32, (tm, tm), 0) < lax.broadcasted_iota(jnp.int32, (tm, tm), 1))
    before = _dot(chose.astype(BF16), jnp.where(earlier, 1.0, 0.0).astype(BF16)) + count_scr[:, 0:1]
    r1 = jnp.sum(jnp.where(eid == i1, before, 0.0), axis=0, keepdims=True)
    r2 = jnp.sum(jnp.where(eid == i2, before, 0.0), axis=0, keepdims=True)
    r_out[...] = jnp.concatenate([r1, r2], axis=0).astype(jnp.int32)
    count_scr[...] += jnp.sum(chose, axis=1, keepdims=True)
    n_out[...] = count_scr[...].astype(jnp.int32)


def _oproj_router_layer(attn_t, x, mods, norm_g, w_o, router_w):
    b, seq, d = x.shape
    t = b * seq
    tm = ROW_TILE
    per_batch = seq // tm
    row = lambda i: (i, 0)
    fixed = lambda i: (0, 0)
    return pl.pallas_call(
        _oproj_router_kernel,
        grid=(t // tm,),
        in_specs=[
            pl.BlockSpec((1, d, tm), lambda i: (i, 0, 0)),
            pl.BlockSpec((tm, d), row),
            pl.BlockSpec((1, N_MOD, d), lambda i: (i // per_batch, 0, 0)),
            pl.BlockSpec((1, d), fixed),
            pl.BlockSpec((d, d), fixed),
            pl.BlockSpec((N_EXPERTS, d), fixed),
        ],
        out_specs=[
            pl.BlockSpec((tm, d), row),
            pl.BlockSpec((tm * ROW_SLABS, LANES), row),
            pl.BlockSpec((TOP_K, tm), lambda i: (0, i)),
            pl.BlockSpec((TOP_K, tm), lambda i: (0, i)),
            pl.BlockSpec((TOP_K, tm), lambda i: (0, i)),
            pl.BlockSpec((N_EXPERTS, LANES), fixed),
        ],
        out_shape=[
            jax.ShapeDtypeStruct((t, d), F32),
            jax.ShapeDtypeStruct((t * ROW_SLABS, LANES), F32),
            jax.ShapeDtypeStruct((TOP_K, t), jnp.int32),
            jax.ShapeDtypeStruct((TOP_K, t), F32),
            jax.ShapeDtypeStruct((TOP_K, t), jnp.int32),
            jax.ShapeDtypeStruct((N_EXPERTS, LANES), jnp.int32),
        ],
        scratch_shapes=[pltpu.VMEM((N_EXPERTS, LANES), F32)],
        compiler_params=_params(("arbitrary",), 48),
        name="oproj_router",
    )(attn_t.reshape(t // tm, d, tm), x.reshape(t, d), mods, norm_g.reshape(1, d), w_o, router_w.T)


def _row_copy(src_ref, dst_ref, src_row, dst_row, sem):
    return pltpu.make_async_copy(_tile_of_row(src_ref, src_row), _tile_of_row(dst_ref, dst_row), sem)


def _rows_copy(src_ref, dst_ref, rows, sem):
    return pltpu.make_async_copy(src_ref.at[pl.ds(0, rows * ROW_SLABS)], dst_ref.at[pl.ds(0, rows * ROW_SLABS)], sem)


def _dest_spec(tm):
    return pl.BlockSpec((1, 1, TOP_K * tm), lambda i: (i, 0, 0), memory_space=pltpu.SMEM)


def _dispatch_kernel(dest_ref, zero_ref, h_ref, xs_ref, zeros_scr, sem):
    tm = h_ref.shape[0] // ROW_SLABS

    @pl.when(pl.program_id(0) == 0)
    def _():
        zeros_scr[...] = jnp.zeros_like(zeros_scr)
        block_rows = MOE_ROWS * ROW_SLABS

        def zero_block(j):
            return pltpu.make_async_copy(
                zeros_scr, xs_ref.at[pl.ds(pl.multiple_of(j * block_rows, block_rows), block_rows)], sem)

        def start(j, carry):
            @pl.when(zero_ref[j] != 0)
            def _():
                zero_block(j).start()
            return carry

        def drain(j, carry):
            @pl.when(zero_ref[j] != 0)
            def _():
                zero_block(j).wait()
            return carry

        lax.fori_loop(0, zero_ref.shape[0], start, 0)
        lax.fori_loop(0, zero_ref.shape[0], drain, 0)

    def issue(r, carry):
        for k in range(TOP_K):
            _row_copy(h_ref, xs_ref, r, dest_ref[0, 0, TOP_K * r + k], sem).start(priority=k)
        return carry

    lax.fori_loop(0, tm, issue, 0)
    for _ in range(TOP_K):
        _rows_copy(h_ref, xs_ref, tm, sem).wait()


def _dispatch(h, dest, zero_blocks, cap):
    t = h.shape[0] // ROW_SLABS
    tm = DISPATCH_TILE
    return pl.pallas_call(
        _dispatch_kernel,
        grid=(t // tm,),
        in_specs=[
            _dest_spec(tm),
            pl.BlockSpec(memory_space=pltpu.SMEM),
            pl.BlockSpec((tm * ROW_SLABS, LANES), lambda i: (i, 0)),
        ],
        out_specs=pl.BlockSpec(memory_space=pl.ANY),
        out_shape=jax.ShapeDtypeStruct((cap * ROW_SLABS, LANES), h.dtype),
        scratch_shapes=[pltpu.VMEM((MOE_ROWS * ROW_SLABS, LANES), h.dtype), pltpu.SemaphoreType.DMA(())],
        compiler_params=_params(("arbitrary",), 32),
        name="moe_dispatch",
    )(dest.reshape(t // tm, 1, TOP_K * tm), zero_blocks, h)


def _ffn_expert_kernel(blk_ref, x_ref, w1_ref, w3_ref, w2_ref, o_ref):
    in_use = pl.program_id(0) < blk_ref[pl.num_programs(0)]

    @pl.when(in_use)
    def _():
        h = _load_rows(x_ref, MOE_ROWS).astype(BF16)
        _store_rows(o_ref, _swiglu(h, w1_ref.at[0], w3_ref.at[0], w2_ref.at[0]))

    @pl.when(jnp.logical_not(in_use))
    def _():
        o_ref[...] = jnp.zeros_like(o_ref)


def _ffn_experts(xs, blk_e, w1, w3, w2):
    cap = xs.shape[0] // ROW_SLABS
    d = D_MODEL
    n_blk = cap // MOE_ROWS
    row_spec = pl.BlockSpec((MOE_ROWS * ROW_SLABS, LANES), lambda i, e: (i, 0))
    grid_spec = pltpu.PrefetchScalarGridSpec(
        num_scalar_prefetch=1,
        grid=(n_blk,),
        in_specs=[
            row_spec,
            pl.BlockSpec((1, d, D_FF), lambda i, e: (e[i], 0, 0)),
            pl.BlockSpec((1, d, D_FF), lambda i, e: (e[i], 0, 0)),
            pl.BlockSpec((1, D_FF, d), lambda i, e: (e[i], 0, 0)),
        ],
        out_specs=row_spec,
    )
    return pl.pallas_call(
        _ffn_expert_kernel,
        grid_spec=grid_spec,
        out_shape=jax.ShapeDtypeStruct(xs.shape, F32),
        compiler_params=_params(("arbitrary",), 60),
        name="ffn_experts",
    )(blk_e, xs, w1, w3, w2)


def _routing_plan(e_idx, ranks, sizes):
    n_assign = e_idx.size
    padded = (sizes + MOE_ROWS - 1) // MOE_ROWS * MOE_ROWS
    ends = jnp.cumsum(padded)
    pstart = ends - padded
    seg_start = sum(jnp.where(e_idx == e, pstart[e], 0) for e in range(N_EXPERTS))
    dest = (ranks + seg_start).T.reshape(-1).astype(jnp.int32)
    cap = n_assign + N_EXPERTS * MOE_ROWS
    blk_start = jnp.arange(cap // MOE_ROWS, dtype=jnp.int32) * MOE_ROWS
    blk_e = jnp.sum((ends[None, :] <= blk_start[:, None]).astype(jnp.int32), axis=1)
    blk_e = jnp.minimum(blk_e, N_EXPERTS - 1)
    real_end = jnp.where(blk_start < ends[-1], (pstart + sizes)[blk_e], 0)
    zero_blocks = (real_end - blk_start < MOE_ROWS).astype(jnp.int32)
    blk_info = jnp.concatenate([blk_e, ends[-1:] // MOE_ROWS]).astype(jnp.int32)
    return dest, blk_info, zero_blocks, cap


def _final_kernel(dest_ref, dest_next_ref, x_ref, gt_ref, mod_ref, fg_ref, y_ref, o_ref, y_buf, sems):
    tm = x_ref.shape[0]
    i = pl.program_id(0)
    slot = lax.rem(i, 2)
    half = TOP_K * tm

    def fetch(idx_ref, into):
        def issue(r, carry):
            for k in range(TOP_K):
                _row_copy(y_ref, y_buf, idx_ref[0, 0, TOP_K * r + k], into * half + k * tm + r,
                          sems.at[into]).start(priority=k)
            return carry

        lax.fori_loop(0, tm, issue, 0, unroll=8)

    @pl.when(i == 0)
    def _():
        fetch(dest_ref, slot)

    @pl.when(i + 1 < pl.num_programs(0))
    def _():
        fetch(dest_next_ref, 1 - slot)

    _rows_copy(y_ref, y_buf, half, sems.at[slot]).wait()
    gt = gt_ref[...]
    moe = (gt[:, 0:1] * _load_rows(y_buf, tm, slot * half) + gt[:, 1:2] * _load_rows(y_buf, tm, slot * half + tm))
    x = x_ref[...] + mod_ref[0, 5:6, :] * moe
    o_ref[...] = _rms(x, fg_ref[...])


def _final_layer(x, ys, dest, gates_t, mods, final_g, seq):
    t, d = x.shape
    tm = ROW_TILE
    per_batch = seq // tm
    n_steps = t // tm
    dest_tiles = dest.reshape(n_steps, 1, TOP_K * tm)
    return pl.pallas_call(
        _final_kernel,
        grid=(n_steps,),
        in_specs=[
            _dest_spec(tm),
            pl.BlockSpec((1, 1, TOP_K * tm), lambda i: (jnp.minimum(i + 1, n_steps - 1), 0, 0),
                         memory_space=pltpu.SMEM),
            pl.BlockSpec((tm, d), lambda i: (i, 0)),
            pl.BlockSpec((tm, TOP_K), lambda i: (i, 0)),
            pl.BlockSpec((1, N_MOD, d), lambda i: (i // per_batch, 0, 0)),
            pl.BlockSpec((1, d), lambda i: (0, 0)),
            pl.BlockSpec(memory_space=pl.ANY),
        ],
        out_specs=pl.BlockSpec((tm, d), lambda i: (i, 0)),
        out_shape=jax.ShapeDtypeStruct((t, d), F32),
        scratch_shapes=[pltpu.VMEM((2 * TOP_K * tm * ROW_SLABS, LANES), F32), pltpu.SemaphoreType.DMA((2,))],
        compiler_params=_params(("arbitrary",), 48),
        name="moe_sum_final_norm",
    )(dest_tiles, dest_tiles, x, gates_t, mods, final_g.reshape(1, d), ys)


def kernel(x, c, ctx, c_ctx, ada_w, ada_b, norm_g, pool_w, pool_scale, ffn_w1, ffn_w3, ffn_w2, w_qkv, w_o,
           q_norm_g, k_norm_g, router_w, moe_w1, moe_w3, moe_w2, final_g):
    b, seq, d = x.shape
    assert ada_w.shape[0] == 2 and b <= 7, "written for the two-layer block (pool+dense, attention+experts)"

    cvecs = jnp.zeros((8, d), F32).at[:b].set(c).at[b].set(c_ctx)
    mods = _adaln(cvecs, ada_w, ada_b).reshape(2, 8, N_MOD, d)
    mods_x = mods[:, :b]
    mods_c = jnp.broadcast_to(mods[:, b:b + 1], (2, b, N_MOD, d))

    pool_wb = pool_w[0].astype(BF16)
    w1b, w3b, w2b = ffn_w1[0].astype(BF16), ffn_w3[0].astype(BF16), ffn_w2[0].astype(BF16)
    x = _pool_layer(x, mods_x[0], norm_g[0, 0], pool_wb, pool_scale[0])
    ctx = _pool_layer(ctx, mods_c[0], norm_g[0, 0], pool_wb, pool_scale[0])
    x = _ffn_dense_layer(x, mods_x[0], norm_g[0, 1], w1b, w3b, w2b)
    ctx = _ffn_dense_layer(ctx, mods_c[0], norm_g[0, 1], w1b, w3b, w2b)

    wqkv = w_qkv[0].astype(BF16)
    lk = seq + ctx.shape[1]
    kv = (jnp.zeros((b, lk, KV_SLAB), BF16), jnp.zeros((b, lk // KEY_CHUNK, KV_SLAB, KEY_CHUNK), BF16))
    q, *kv = _qkv_layer(x, mods_x[1], norm_g[1, 0], wqkv, q_norm_g[0], k_norm_g[0], _rope_tables(seq), 0, kv)
    _, k_all, vt = _qkv_layer(ctx, mods_c[1], norm_g[1, 0], wqkv[:, Q_DIM:], q_norm_g[0], k_norm_g[0], None, seq, kv)
    attn_t = _attention(q, k_all, vt)

    assert ATT_TQ == ROW_TILE, "the output projection reads one transposed attention tile per row tile"
    x3, hf, e_idx, gates, ranks, counts = _oproj_router_layer(
        attn_t, x, mods_x[1], norm_g[1, 1], w_o[0].astype(BF16), router_w[0])
    dest, blk_e, zero_blocks, cap = _routing_plan(e_idx, ranks, counts[:, 0])
    xs = _dispatch(hf, dest, zero_blocks, cap)
    ys = _ffn_experts(xs, blk_e, moe_w1[0].astype(BF16), moe_w3[0].astype(BF16), moe_w2[0].astype(BF16))
    out = _final_layer(x3, ys, dest, gates.T, mods_x[1], final_g, seq)
    return out.reshape(b, seq, d)
```

```python
import functools
import math

import jax
import jax.numpy as jnp
from jax import lax
from jax.experimental import pallas as pl
from jax.experimental.pallas import tpu as pltpu

F32 = jnp.float32
BF16 = jnp.bfloat16

D_MODEL = 1024
GRID_W = 64
POOL_WINDOWS = (2, 4, 8, 16)
POOL_GROUP_DIM = D_MODEL // len(POOL_WINDOWS)
HEAD_DIM = 64
N_KV_HEADS = 4
GROUP = 4
KV_SLAB = N_KV_HEADS * HEAD_DIM
Q_DIM = N_KV_HEADS * GROUP * HEAD_DIM
ROPE_THETA = 10000.0
Q_SCALE = HEAD_DIM ** -0.5 * math.log2(math.e)
D_FF = 2816
N_EXPERTS = 8
TOP_K = 2
NORM_EPS = 1e-6
N_MOD = 6

POOL_HALO = 16
ROW_TILE = 512
MXU_TILE = 256
FF_CHUNKS = ((0, 6 * MXU_TILE), (6 * MXU_TILE, D_FF))
MOE_ROWS = 512
DISPATCH_TILE = 2048
ATT_TQ = 512
ATT_TK = 1280
KEY_CHUNK = 256
DENOM_ROWS = 8
ATT_TILES = 16
MIB = 1024 * 1024
LANES = 128
ROW_SLABS = D_MODEL // LANES


def _dot(a, b):
    return jnp.dot(a, b, preferred_element_type=F32)


def _rms(x, g):
    ms = jnp.mean(x * x, axis=-1, keepdims=True)
    return x * lax.rsqrt(ms + NORM_EPS) * g


def _silu(x):
    return x * (1.0 / (1.0 + jnp.exp(-x)))


def _load_rows(ref, n_rows, first=0):
    return jnp.concatenate(
        [ref[pl.ds(ROW_SLABS * first + s, n_rows, stride=ROW_SLABS), :] for s in range(ROW_SLABS)], axis=1)


def _store_rows(ref, val):
    for s in range(ROW_SLABS):
        ref[pl.ds(s, val.shape[0], stride=ROW_SLABS), :] = val[:, s * LANES:(s + 1) * LANES]


def _tile_of_row(ref, row):
    return ref.at[pl.ds(pl.multiple_of(row * ROW_SLABS, ROW_SLABS), ROW_SLABS)]


def _params(semantics, vmem_mib):
    return pltpu.CompilerParams(dimension_semantics=semantics, vmem_limit_bytes=vmem_mib * MIB)


def _adaln_kernel(c_ref, w_ref, b_ref, o_ref):
    s = _silu(c_ref[...])
    o_ref[0] = jnp.dot(s, w_ref[0], preferred_element_type=F32, precision=lax.Precision.HIGHEST) + b_ref[0]


def _adaln(cvecs, ada_w, ada_b):
    depth, d, n = ada_w.shape
    tn = n // 4
    return pl.pallas_call(
        _adaln_kernel,
        grid=(depth, n // tn),
        in_specs=[
            pl.BlockSpec((8, d), lambda i, j: (0, 0)),
            pl.BlockSpec((1, d, tn), lambda i, j: (i, 0, j)),
            pl.BlockSpec((1, 1, tn), lambda i, j: (i, 0, j)),
        ],
        out_specs=pl.BlockSpec((1, 8, tn), lambda i, j: (i, 0, j)),
        out_shape=jax.ShapeDtypeStruct((depth, 8, n), F32),
        compiler_params=_params(("parallel", "parallel"), 32),
        name="adaln",
    )(cvecs, ada_w, ada_b.reshape(depth, 1, n))


def _pool_kernel(xp_ref, xc_ref, xn_ref, mod_ref, ng_ref, pw_ref, ps_ref, o_ref, *, seq, tm):
    t = pl.program_id(1)
    xc = xc_ref[0]
    xa = jnp.concatenate([xp_ref[0], xc, xn_ref[0]], axis=0)
    sh, sc, gate = mod_ref[0, 0:1, :], mod_ref[0, 1:2, :], mod_ref[0, 2:3, :]
    h = _rms(xa, ng_ref[...]) * (1.0 + sc) + sh
    hb = h.astype(BF16)
    wide = tm + 2 * POOL_HALO
    row = lax.broadcasted_iota(jnp.int32, (tm, wide), 0) + t * tm
    col = lax.broadcasted_iota(jnp.int32, (tm, wide), 1) + (t * tm - POOL_HALO)
    row1 = lax.broadcasted_iota(jnp.int32, (tm, 1), 0) + t * tm
    ys = []
    for g, w in enumerate(POOL_WINDOWS):
        half = w // 2
        lo = jnp.maximum(row - half, 0)
        hi = jnp.minimum(row + half, seq)
        band = jnp.where((col >= lo) & (col < hi), 1.0, 0.0).astype(BF16)
        cnt = (jnp.minimum(row1 + half, seq) - jnp.maximum(row1 - half, 0)).astype(F32)
        lanes = slice(g * POOL_GROUP_DIM, (g + 1) * POOL_GROUP_DIM)
        win_sum = _dot(band, hb[:, lanes])
        d = win_sum / cnt - h[POOL_HALO:POOL_HALO + tm, lanes]
        ys.append(_dot(d.astype(BF16), pw_ref[g]))
    y = jnp.concatenate(ys, axis=1) * ps_ref[...]
    o_ref[0] = xc + gate * y


def _pool_layer(x, mods, norm_g, pool_w, pool_scale):
    b, seq, d = x.shape
    tm = min(ROW_TILE, seq)
    per = tm // POOL_HALO
    last = seq // POOL_HALO - 1
    return pl.pallas_call(
        functools.partial(_pool_kernel, seq=seq, tm=tm),
        grid=(b, seq // tm),
        in_specs=[
            pl.BlockSpec((1, POOL_HALO, d), lambda i, t: (i, jnp.maximum(t * per - 1, 0), 0)),
            pl.BlockSpec((1, tm, d), lambda i, t: (i, t, 0)),
            pl.BlockSpec((1, POOL_HALO, d), lambda i, t: (i, jnp.minimum((t + 1) * per, last), 0)),
            pl.BlockSpec((1, N_MOD, d), lambda i, t: (i, 0, 0)),
            pl.BlockSpec((1, d), lambda i, t: (0, 0)),
            pl.BlockSpec(pool_w.shape, lambda i, t: (0, 0, 0)),
            pl.BlockSpec((1, d), lambda i, t: (0, 0)),
        ],
        out_specs=pl.BlockSpec((1, tm, d), lambda i, t: (i, t, 0)),
        out_shape=jax.ShapeDtypeStruct(x.shape, F32),
        compiler_params=_params(("parallel", "parallel"), 48),
        name="pool_mix",
    )(x, x, x, mods, norm_g.reshape(1, d), pool_w, pool_scale.reshape(1, d))


def _swiglu(h, w1_ref, w3_ref, w2_ref):
    acc = None
    for start, stop in FF_CHUNKS:
        cols = slice(start, stop)
        u = _silu(_dot(h, w1_ref[:, cols])) * _dot(h, w3_ref[:, cols])
        part = _dot(u.astype(BF16), w2_ref[cols, :])
        acc = part if acc is None else acc + part
    return acc


def _ffn_dense_kernel(x_ref, mod_ref, ng_ref, w1_ref, w3_ref, w2_ref, *rest):
    n_cast = (len(rest) - 1) // 2
    cast_in, o_ref, cast_out = rest[:n_cast], rest[n_cast], rest[n_cast + 1:]
    sh, sc, gate = mod_ref[0, 3:4, :], mod_ref[0, 4:5, :], mod_ref[0, 5:6, :]
    x = x_ref[...]
    h = (_rms(x, ng_ref[...]) * (1.0 + sc) + sh).astype(BF16)
    o_ref[...] = x + gate * _swiglu(h, w1_ref, w3_ref, w2_ref)
    for src, dst in zip(cast_in, cast_out):
        dst[...] = src[...].astype(BF16)


def _ffn_dense_layer(x, mods, norm_g, w1, w3, w2, cast_through=()):
    b, seq, d = x.shape
    t = b * seq
    tm = min(ROW_TILE, seq)
    per_batch = seq // tm
    steps = t // tm
    fixed = lambda i: (0, 0)
    cast_specs = []
    for w in cast_through:
        e, rows, cols = w.shape
        per_expert = steps // e
        assert steps % e == 0 and rows % (per_expert * 16) == 0
        cast_specs.append(pl.BlockSpec((1, rows // per_expert, cols),
                                       lambda i, per_expert=per_expert: (i // per_expert, i % per_expert, 0)))
    outs = pl.pallas_call(
        _ffn_dense_kernel,
        grid=(steps,),
        in_specs=[
            pl.BlockSpec((tm, d), lambda i: (i, 0)),
            pl.BlockSpec((1, N_MOD, d), lambda i: (i // per_batch, 0, 0)),
            pl.BlockSpec((1, d), fixed),
            pl.BlockSpec((d, D_FF), fixed),
            pl.BlockSpec((d, D_FF), fixed),
            pl.BlockSpec((D_FF, d), fixed),
        ] + cast_specs,
        out_specs=[pl.BlockSpec((tm, d), lambda i: (i, 0))] + cast_specs,
        out_shape=[jax.ShapeDtypeStruct((t, d), F32)] + [jax.ShapeDtypeStruct(w.shape, BF16) for w in cast_through],
        compiler_params=_params(("parallel",), 60),
        name="ffn_dense",
    )(x.reshape(t, d), mods, norm_g.reshape(1, d), w1, w3, w2, *cast_through)
    return [outs[0].reshape(b, seq, d)] + list(outs[1:])


def _head_norm_rope(z, ones_blk, gain, cos, sin):
    sq = z * z
    hi = sq.astype(BF16)
    lo = (sq - hi.astype(F32)).astype(BF16)
    ss = _dot(hi, ones_blk) + _dot(lo, ones_blk)
    zn = z * lax.rsqrt(ss * (1.0 / HEAD_DIM) + NORM_EPS) * gain
    if cos is None:
        return zn
    width = z.shape[1]
    fwd = pltpu.roll(zn, width - 16, 1)
    bwd = pltpu.roll(zn, 16, 1)
    lane = lax.broadcasted_iota(jnp.int32, zn.shape, 1)
    partner = jnp.where((lane & 31) < 16, fwd, bwd)
    return zn * cos + partner * sin


def _qkv_kernel(*refs, n_q, rope, n_aliased):
    x_ref, mod_ref, ng_ref, w_ref, e_ref, qg_ref, kg_ref = refs[:7]
    refs = refs[7:]
    if rope:
        cos_ref, sin_ref = refs[:2]
        refs = refs[2:]
        cos, sin = cos_ref[...], sin_ref[...]
    else:
        cos = sin = None
    refs = refs[n_aliased:]
    sh, sc = mod_ref[0, 0:1, :], mod_ref[0, 1:2, :]
    hb = (_rms(x_ref[0], ng_ref[...]) * (1.0 + sc) + sh).astype(BF16)
    qkv = _dot(hb, w_ref[...])
    ones_blk = e_ref[...]
    if n_q:
        q_ref, k_ref, v_ref = refs
        for c in range(n_q):
            z = _head_norm_rope(qkv[:, c * KV_SLAB:(c + 1) * KV_SLAB], ones_blk, qg_ref[...], cos, sin)
            q_ref[0, :, c * KV_SLAB:(c + 1) * KV_SLAB] = (z * Q_SCALE).astype(BF16)
    else:
        k_ref, v_ref = refs
    base = n_q * KV_SLAB
    k_ref[0] = _head_norm_rope(qkv[:, base:base + KV_SLAB], ones_blk, kg_ref[...], cos, sin).astype(BF16)
    v = qkv[:, base + KV_SLAB:base + 2 * KV_SLAB]
    for j in range(v_ref.shape[1]):
        v_ref[0, j] = v[j * KEY_CHUNK:(j + 1) * KEY_CHUNK, :].T.astype(BF16)


def _qkv_layer(x, mods, norm_g, w, q_gain, k_gain, tables, key_offset, kv_buffers):
    b, seq, d = x.shape
    tm = min(ROW_TILE, seq)
    width = w.shape[1]
    n_q = (width - 2 * KV_SLAB) // KV_SLAB
    rope = tables is not None
    total_keys = kv_buffers[0].shape[1]
    assert tm % KEY_CHUNK == 0 and key_offset % tm == 0 and total_keys % KEY_CHUNK == 0
    seg = lax.broadcasted_iota(jnp.int32, (KV_SLAB, KV_SLAB), 0) // HEAD_DIM
    ones_blk = (seg == seg.T).astype(BF16)
    in_specs = [
        pl.BlockSpec((1, tm, d), lambda i, t: (i, t, 0)),
        pl.BlockSpec((1, N_MOD, d), lambda i, t: (i, 0, 0)),
        pl.BlockSpec((1, d), lambda i, t: (0, 0)),
        pl.BlockSpec((d, width), lambda i, t: (0, 0)),
        pl.BlockSpec((KV_SLAB, KV_SLAB), lambda i, t: (0, 0)),
        pl.BlockSpec((1, KV_SLAB), lambda i, t: (0, 0)),
        pl.BlockSpec((1, KV_SLAB), lambda i, t: (0, 0)),
    ]
    args = [x, mods, norm_g.reshape(1, d), w, ones_blk,
            jnp.tile(q_gain, N_KV_HEADS).reshape(1, KV_SLAB), jnp.tile(k_gain, N_KV_HEADS).reshape(1, KV_SLAB)]
    if rope:
        in_specs += [pl.BlockSpec((tm, KV_SLAB), lambda i, t: (t, 0))] * 2
        args += list(tables)
    first_tile, chunks = key_offset // tm, tm // KEY_CHUNK
    out_specs = [pl.BlockSpec((1, tm, KV_SLAB), lambda i, t: (i, first_tile + t, 0)),
                 pl.BlockSpec((1, chunks, KV_SLAB, KEY_CHUNK), lambda i, t: (i, first_tile + t, 0, 0))]
    out_shape = [jax.ShapeDtypeStruct((b, total_keys, KV_SLAB), BF16),
                 jax.ShapeDtypeStruct((b, total_keys // KEY_CHUNK, KV_SLAB, KEY_CHUNK), BF16)]
    if n_q:
        out_specs = [pl.BlockSpec((1, tm, n_q * KV_SLAB), lambda i, t: (i, t, 0))] + out_specs
        out_shape = [jax.ShapeDtypeStruct((b, seq, n_q * KV_SLAB), BF16)] + out_shape
    aliases = {len(args) + j: len(out_shape) - 2 + j for j in range(2)}
    in_specs += [pl.BlockSpec(memory_space=pl.ANY)] * 2
    args += list(kv_buffers)
    outs = pl.pallas_call(
        functools.partial(_qkv_kernel, n_q=n_q, rope=rope, n_aliased=len(aliases)),
        grid=(b, seq // tm),
        in_specs=in_specs,
        out_specs=out_specs,
        out_shape=out_shape,
        input_output_aliases=aliases,
        compiler_params=_params(("parallel", "parallel"), 48),
        name="qkv_rope" if rope else "kv_ctx",
    )(*args)
    return outs if n_q else [None] + list(outs)


def _rope_tables(seq):
    half = HEAD_DIM // 4
    t = jnp.arange(seq, dtype=jnp.int32)
    pos = jnp.stack([t // GRID_W, t % GRID_W], axis=1).astype(F32)
    freqs = ROPE_THETA ** (-jnp.arange(half, dtype=F32) / half)
    ang = pos[:, :, None] * freqs[None, None, :]
    cos = jnp.cos(ang)
    sin = jnp.sin(ang)
    cos_h = jnp.concatenate([cos, cos], axis=2).reshape(seq, HEAD_DIM)
    sin_h = jnp.concatenate([-sin, sin], axis=2).reshape(seq, HEAD_DIM)
    return jnp.tile(cos_h, (1, N_KV_HEADS)), jnp.tile(sin_h, (1, N_KV_HEADS))


def _attn_kernel(q_ref, k_ref, vt_ref, o_ref, qt_scr, m_scr, acc_scr, sa_scr, sb_scr, smaxa_scr, smaxb_scr, *,
                 tq, tk, n_kblk, n_tiles):
    odd_head = lax.rem(pl.program_id(1), 2) == 1

    def load_queries(t):
        rows = pl.ds(pl.multiple_of(t * tq, tq), tq)
        for j in range(GROUP // 2):
            pair_t = q_ref[0, rows, j * 2 * HEAD_DIM:(j + 1) * 2 * HEAD_DIM].astype(F32).T
            for h in range(2):
                qt = pair_t[h * HEAD_DIM:(h + 1) * HEAD_DIM, :]
                zero = jnp.zeros_like(qt)
                low, high = jnp.concatenate([qt, zero], axis=0), jnp.concatenate([zero, qt], axis=0)
                qt_scr[2 * j + h] = jnp.where(odd_head, high, low).astype(BF16)

    def reset_state():
        m_scr[...] = jnp.full_like(m_scr, -jnp.inf)
        acc_scr[...] = jnp.zeros_like(acc_scr)

    def write_output(t):
        for g in range(GROUP):
            acc = acc_scr[g]
            out = acc[:HEAD_DIM] / jnp.sum(acc[HEAD_DIM:], axis=0, keepdims=True)
            o_ref[0, t, g * HEAD_DIM:(g + 1) * HEAD_DIM, :] = out.astype(BF16)

    n_chunks = tk // KEY_CHUNK

    def scores_chunk(g, k, c, buf):
        keys = k_ref[0, pl.ds(pl.multiple_of(k * tk + c * KEY_CHUNK, KEY_CHUNK), KEY_CHUNK), :]
        s = _dot(keys, qt_scr[g])
        buf[0][g, c * KEY_CHUNK:(c + 1) * KEY_CHUNK, :] = s
        return jnp.max(s, axis=0, keepdims=True)

    def values_chunk(g, k, c, buf, m_next):
        p = jnp.exp2(buf[0][g, c * KEY_CHUNK:(c + 1) * KEY_CHUNK, :] - m_next[0:1, :])
        psum = jnp.sum(p.reshape(KEY_CHUNK // DENOM_ROWS, DENOM_ROWS, tq), axis=0)
        pv = _dot(vt_ref[0, k * n_chunks + c], p.astype(BF16))
        return jnp.concatenate([pv, psum], axis=0)

    def scores(k, buf):
        for g in range(GROUP):
            smax = None
            for c in range(n_chunks):
                cmax = scores_chunk(g, k, c, buf)
                smax = cmax if smax is None else jnp.maximum(smax, cmax)
            buf[1][g] = jnp.broadcast_to(smax, (8, tq))

    def scores_and_accumulate(k_new, buf_new, k_old, buf_old):
        for g in range(GROUP):
            m_prev = m_scr[g]
            m_next = jnp.maximum(m_prev, buf_old[1][g])
            alpha = jnp.exp2(m_prev - m_next)
            smax = pv = None
            for c in range(n_chunks):
                cmax = scores_chunk(g, k_new, c, buf_new)
                part = values_chunk(g, k_old, c, buf_old, m_next)
                smax = cmax if smax is None else jnp.maximum(smax, cmax)
                pv = part if pv is None else pv + part
            buf_new[1][g] = jnp.broadcast_to(smax, (8, tq))
            acc_scr[g] = alpha[0:1, :] * acc_scr[g] + pv
            m_scr[g] = m_next

    def query_tile(t, first, second):
        def key_block_pair(j, carry):
            k = 2 * j
            scores_and_accumulate(k + 1, second, k, first)
            scores_and_accumulate(k + 2, first, k + 1, second)
            return carry

        lax.fori_loop(0, (n_kblk - 1) // 2, key_block_pair, 0, unroll=2)
        load_queries(jnp.minimum(t + 1, n_tiles - 1))
        scores_and_accumulate(0, second, n_kblk - 1, first)
        write_output(t)
        reset_state()

    buf_a, buf_b = (sa_scr, smaxa_scr), (sb_scr, smaxb_scr)
    load_queries(0)
    reset_state()
    scores(0, buf_a)

    def query_tile_pair(i, carry):
        query_tile(2 * i, buf_a, buf_b)
        query_tile(2 * i + 1, buf_b, buf_a)
        return carry

    lax.fori_loop(0, n_tiles // 2, query_tile_pair, 0)


def _attention(q, k, vt):
    b, seq, _ = q.shape
    lk = k.shape[1]
    tq, tk, n_tiles = ATT_TQ, ATT_TK, ATT_TILES
    n_kblk = lk // tk
    assert n_kblk % 2 == 1 and n_tiles % 2 == 0 and seq % (tq * n_tiles) == 0 and lk == n_kblk * tk
    assert tk % KEY_CHUNK == 0 and vt.shape == (b, lk // KEY_CHUNK, KV_SLAB, KEY_CHUNK)
    stat_rows = 8
    return pl.pallas_call(
        functools.partial(_attn_kernel, tq=tq, tk=tk, n_kblk=n_kblk, n_tiles=n_tiles),
        grid=(b, N_KV_HEADS, seq // (tq * n_tiles)),
        in_specs=[
            pl.BlockSpec((1, tq * n_tiles, KV_SLAB), lambda i, h, qi: (i, qi, h)),
            pl.BlockSpec((1, lk, 2 * HEAD_DIM), lambda i, h, qi: (i, 0, h // 2)),
            pl.BlockSpec((1, lk // KEY_CHUNK, HEAD_DIM, KEY_CHUNK), lambda i, h, qi: (i, 0, h, 0)),
        ],
        out_specs=pl.BlockSpec((1, n_tiles, KV_SLAB, tq), lambda i, h, qi: (i, qi, h, 0)),
        out_shape=jax.ShapeDtypeStruct((b, seq // tq, Q_DIM, tq), BF16),
        scratch_shapes=[
            pltpu.VMEM((GROUP, 2 * HEAD_DIM, tq), BF16),
            pltpu.VMEM((GROUP, stat_rows, tq), F32),
            pltpu.VMEM((GROUP, HEAD_DIM + DENOM_ROWS, tq), F32),
            pltpu.VMEM((GROUP, tk, tq), F32),
            pltpu.VMEM((GROUP, tk, tq), F32),
            pltpu.VMEM((GROUP, stat_rows, tq), F32),
            pltpu.VMEM((GROUP, stat_rows, tq), F32),
        ],
        compiler_params=_params(("parallel", "parallel", "parallel"), 60),
        name="gqa_attention",
    )(q, k, vt)


def _oproj_router_kernel(a_ref, x_ref, mod_ref, ng_ref, wo_ref, rw_ref, x_out, h_out, e_out, g_out, r_out, n_out,
                         count_scr):
    @pl.when(pl.program_id(0) == 0)
    def _():
        count_scr[...] = jnp.zeros_like(count_scr)

    g1 = mod_ref[0, 2:3, :]
    sh, sc = mod_ref[0, 3:4, :], mod_ref[0, 4:5, :]
    proj = lax.dot_general(a_ref[0], wo_ref[...], (((0,), (0,)), ((), ())), preferred_element_type=F32)
    x = x_ref[...] + g1 * proj
    x_out[...] = x
    h = _rms(x, ng_ref[...]) * (1.0 + sc) + sh
    _store_rows(h_out, h)
    logits = lax.dot_general(rw_ref[...], h, (((1,), (1,)), ((), ())),
                             preferred_element_type=F32, precision=lax.Precision.HIGHEST)
    eid = lax.broadcasted_iota(jnp.int32, logits.shape, 0)
    m1 = jnp.max(logits, axis=0, keepdims=True)
    i1 = jnp.min(jnp.where(logits == m1, eid, N_EXPERTS), axis=0, keepdims=True)
    rest = jnp.where(eid == i1, -jnp.inf, logits)
    m2 = jnp.max(rest, axis=0, keepdims=True)
    i2 = jnp.min(jnp.where(rest == m2, eid, N_EXPERTS), axis=0, keepdims=True)
    e = jnp.exp(m2 - m1)
    e_out[...] = jnp.concatenate([i1, i2], axis=0)
    g_out[...] = jnp.concatenate([1.0 / (1.0 + e), e / (1.0 + e)], axis=0)
    tm = logits.shape[1]
    chose = jnp.where((eid == i1) | (eid == i2), 1.0, 0.0)
    earlier = (lax.broadcasted_iota(jnp.int32, (tm, tm), 0) < lax.broadcasted_iota(jnp.int32, (tm, tm), 1))
    before = _dot(chose.astype(BF16), jnp.where(earlier, 1.0, 0.0).astype(BF16)) + count_scr[:, 0:1]
    r1 = jnp.sum(jnp.where(eid == i1, before, 0.0), axis=0, keepdims=True)
    r2 = jnp.sum(jnp.where(eid == i2, before, 0.0), axis=0, keepdims=True)
    r_out[...] = jnp.concatenate([r1, r2], axis=0).astype(jnp.int32)
    count_scr[...] += jnp.sum(chose, axis=1, keepdims=True)
    n_out[...] = count_scr[...].astype(jnp.int32)


def _oproj_router_layer(attn_t, x, mods, norm_g, w_o, router_w):
    b, seq, d = x.shape
    t = b * seq
    tm = ROW_TILE
    per_batch = seq // tm
    row = lambda i: (i, 0)
    fixed = lambda i: (0, 0)
    return pl.pallas_call(
        _oproj_router_kernel,
        grid=(t // tm,),
        in_specs=[
            pl.BlockSpec((1, d, tm), lambda i: (i, 0, 0)),
            pl.BlockSpec((tm, d), row),
            pl.BlockSpec((1, N_MOD, d), lambda i: (i // per_batch, 0, 0)),
            pl.BlockSpec((1, d), fixed),
            pl.BlockSpec((d, d), fixed),
            pl.BlockSpec((N_EXPERTS, d), fixed),
        ],
        out_specs=[
            pl.BlockSpec((tm, d), row),
            pl.BlockSpec((tm * ROW_SLABS, LANES), row),
            pl.BlockSpec((TOP_K, tm), lambda i: (0, i)),
            pl.BlockSpec((TOP_K, tm), lambda i: (0, i)),
            pl.BlockSpec((TOP_K, tm), lambda i: (0, i)),
            pl.BlockSpec((N_EXPERTS, LANES), fixed),
        ],
        out_shape=[
            jax.ShapeDtypeStruct((t, d), F32),
            jax.ShapeDtypeStruct((t * ROW_SLABS, LANES), F32),
            jax.ShapeDtypeStruct((TOP_K, t), jnp.int32),
            jax.ShapeDtypeStruct((TOP_K, t), F32),
            jax.ShapeDtypeStruct((TOP_K, t), jnp.int32),
            jax.ShapeDtypeStruct((N_EXPERTS, LANES), jnp.int32),
        ],
        scratch_shapes=[pltpu.VMEM((N_EXPERTS, LANES), F32)],
        compiler_params=_params(("arbitrary",), 48),
        name="oproj_router",
    )(attn_t.reshape(t // tm, d, tm), x.reshape(t, d), mods, norm_g.reshape(1, d), w_o, router_w.T)


def _row_copy(src_ref, dst_ref, src_row, dst_row, sem):
    return pltpu.make_async_copy(_tile_of_row(src_ref, src_row), _tile_of_row(dst_ref, dst_row), sem)


def _rows_copy(src_ref, dst_ref, rows, sem):
    return pltpu.make_async_copy(src_ref.at[pl.ds(0, rows * ROW_SLABS)], dst_ref.at[pl.ds(0, rows * ROW_SLABS)], sem)


def _dest_spec(tm):
    return pl.BlockSpec((1, 1, TOP_K * tm), lambda i: (i, 0, 0), memory_space=pltpu.SMEM)


def _dispatch_kernel(dest_ref, zero_ref, h_ref, xs_ref, zeros_scr, sem):
    tm = h_ref.shape[0] // ROW_SLABS

    @pl.when(pl.program_id(0) == 0)
    def _():
        zeros_scr[...] = jnp.zeros_like(zeros_scr)
        block_rows = MOE_ROWS * ROW_SLABS

        def zero_block(j):
            return pltpu.make_async_copy(
                zeros_scr, xs_ref.at[pl.ds(pl.multiple_of(j * block_rows, block_rows), block_rows)], sem)

        def start(j, carry):
            @pl.when(zero_ref[j] != 0)
            def _():
                zero_block(j).start()
            return carry

        def drain(j, carry):
            @pl.when(zero_ref[j] != 0)
            def _():
                zero_block(j).wait()
            return carry

        lax.fori_loop(0, zero_ref.shape[0], start, 0)
        lax.fori_loop(0, zero_ref.shape[0], drain, 0)

    def issue(r, carry):
        for k in range(TOP_K):
            _row_copy(h_ref, xs_ref, r, dest_ref[0, 0, TOP_K * r + k], sem).start(priority=k)
        return carry

    lax.fori_loop(0, tm, issue, 0)
    for _ in range(TOP_K):
        _rows_copy(h_ref, xs_ref, tm, sem).wait()


def _dispatch(h, dest, zero_blocks, cap):
    t = h.shape[0] // ROW_SLABS
    tm = DISPATCH_TILE
    return pl.pallas_call(
        _dispatch_kernel,
        grid=(t // tm,),
        in_specs=[
            _dest_spec(tm),
            pl.BlockSpec(memory_space=pltpu.SMEM),
            pl.BlockSpec((tm * ROW_SLABS, LANES), lambda i: (i, 0)),
        ],
        out_specs=pl.BlockSpec(memory_space=pl.ANY),
        out_shape=jax.ShapeDtypeStruct((cap * ROW_SLABS, LANES), h.dtype),
        scratch_shapes=[pltpu.VMEM((MOE_ROWS * ROW_SLABS, LANES), h.dtype), pltpu.SemaphoreType.DMA(())],
        compiler_params=_params(("arbitrary",), 32),
        name="moe_dispatch",
    )(dest.reshape(t // tm, 1, TOP_K * tm), zero_blocks, h)


def _ffn_expert_kernel(blk_ref, x_ref, w1_ref, w3_ref, w2_ref, o_ref):
    in_use = pl.program_id(0) < blk_ref[pl.num_programs(0)]

    @pl.when(in_use)
    def _():
        h = _load_rows(x_ref, MOE_ROWS).astype(BF16)
        _store_rows(o_ref, _swiglu(h, w1_ref.at[0], w3_ref.at[0], w2_ref.at[0]))

    @pl.when(jnp.logical_not(in_use))
    def _():
        o_ref[...] = jnp.zeros_like(o_ref)


def _ffn_experts(xs, blk_e, w1, w3, w2):
    cap = xs.shape[0] // ROW_SLABS
    d = D_MODEL
    n_blk = cap // MOE_ROWS
    row_spec = pl.BlockSpec((MOE_ROWS * ROW_SLABS, LANES), lambda i, e: (i, 0))
    grid_spec = pltpu.PrefetchScalarGridSpec(
        num_scalar_prefetch=1,
        grid=(n_blk,),
        in_specs=[
            row_spec,
            pl.BlockSpec((1, d, D_FF), lambda i, e: (e[i], 0, 0)),
            pl.BlockSpec((1, d, D_FF), lambda i, e: (e[i], 0, 0)),
            pl.BlockSpec((1, D_FF, d), lambda i, e: (e[i], 0, 0)),
        ],
        out_specs=row_spec,
    )
    return pl.pallas_call(
        _ffn_expert_kernel,
        grid_spec=grid_spec,
        out_shape=jax.ShapeDtypeStruct(xs.shape, F32),
        compiler_params=_params(("arbitrary",), 60),
        name="ffn_experts",
    )(blk_e, xs, w1, w3, w2)


def _routing_plan(e_idx, ranks, sizes):
    n_assign = e_idx.size
    padded = (sizes + MOE_ROWS - 1) // MOE_ROWS * MOE_ROWS
    ends = jnp.cumsum(padded)
    pstart = ends - padded
    seg_start = sum(jnp.where(e_idx == e, pstart[e], 0) for e in range(N_EXPERTS))
    dest = (ranks + seg_start).T.reshape(-1).astype(jnp.int32)
    cap = n_assign + N_EXPERTS * MOE_ROWS
    blk_start = jnp.arange(cap // MOE_ROWS, dtype=jnp.int32) * MOE_ROWS
    blk_e = jnp.sum((ends[None, :] <= blk_start[:, None]).astype(jnp.int32), axis=1)
    blk_e = jnp.minimum(blk_e, N_EXPERTS - 1)
    real_end = jnp.where(blk_start < ends[-1], (pstart + sizes)[blk_e], 0)
    zero_blocks = (real_end - blk_start < MOE_ROWS).astype(jnp.int32)
    blk_info = jnp.concatenate([blk_e, ends[-1:] // MOE_ROWS]).astype(jnp.int32)
    return dest, blk_info, zero_blocks, cap


def _final_kernel(dest_ref, dest_next_ref, x_ref, gt_ref, mod_ref, fg_ref, y_ref, o_ref, y_buf, sems):
    tm = x_ref.shape[0]
    i = pl.program_id(0)
    slot = lax.rem(i, 2)
    half = TOP_K * tm

    def fetch(idx_ref, into):
        def issue(r, carry):
            for k in range(TOP_K):
                _row_copy(y_ref, y_buf, idx_ref[0, 0, TOP_K * r + k], into * half + k * tm + r,
                          sems.at[into]).start(priority=k)
            return carry

        lax.fori_loop(0, tm, issue, 0, unroll=8)

    @pl.when(i == 0)
    def _():
        fetch(dest_ref, slot)

    @pl.when(i + 1 < pl.num_programs(0))
    def _():
        fetch(dest_next_ref, 1 - slot)

    _rows_copy(y_ref, y_buf, half, sems.at[slot]).wait()
    gt = gt_ref[...]
    moe = (gt[:, 0:1] * _load_rows(y_buf, tm, slot * half) + gt[:, 1:2] * _load_rows(y_buf, tm, slot * half + tm))
    x = x_ref[...] + mod_ref[0, 5:6, :] * moe
    o_ref[...] = _rms(x, fg_ref[...])


def _final_layer(x, ys, dest, gates_t, mods, final_g, seq):
    t, d = x.shape
    tm = ROW_TILE
    per_batch = seq // tm
    n_steps = t // tm
    dest_tiles = dest.reshape(n_steps, 1, TOP_K * tm)
    return pl.pallas_call(
        _final_kernel,
        grid=(n_steps,),
        in_specs=[
            _dest_spec(tm),
            pl.BlockSpec((1, 1, TOP_K * tm), lambda i: (jnp.minimum(i + 1, n_steps - 1), 0, 0),
                         memory_space=pltpu.SMEM),
            pl.BlockSpec((tm, d), lambda i: (i, 0)),
            pl.BlockSpec((tm, TOP_K), lambda i: (i, 0)),
            pl.BlockSpec((1, N_MOD, d), lambda i: (i // per_batch, 0, 0)),
            pl.BlockSpec((1, d), lambda i: (0, 0)),
            pl.BlockSpec(memory_space=pl.ANY),
        ],
        out_specs=pl.BlockSpec((tm, d), lambda i: (i, 0)),
        out_shape=jax.ShapeDtypeStruct((t, d), F32),
        scratch_shapes=[pltpu.VMEM((2 * TOP_K * tm * ROW_SLABS, LANES), F32), pltpu.SemaphoreType.DMA((2,))],
        compiler_params=_params(("arbitrary",), 48),
        name="moe_sum_final_norm",
    )(dest_tiles, dest_tiles, x, gates_t, mods, final_g.reshape(1, d), ys)


def kernel(x, c, ctx, c_ctx, ada_w, ada_b, norm_g, pool_w, pool_scale, ffn_w1, ffn_w3, ffn_w2, w_qkv, w_o,
           q_norm_g, k_norm_g, router_w, moe_w1, moe_w3, moe_w2, final_g):
    b, seq, d = x.shape
    assert ada_w.shape[0] == 2 and b <= 7, "written for the two-layer block (pool+dense, attention+experts)"

    cvecs = jnp.zeros((8, d), F32).at[:b].set(c).at[b].set(c_ctx)
    mods = _adaln(cvecs, ada_w, ada_b).reshape(2, 8, N_MOD, d)
    mods_x = mods[:, :b]
    mods_c = jnp.broadcast_to(mods[:, b:b + 1], (2, b, N_MOD, d))

    pool_wb = pool_w[0].astype(BF16)
    w1b, w3b, w2b = ffn_w1[0].astype(BF16), ffn_w3[0].astype(BF16), ffn_w2[0].astype(BF16)
    x = _pool_layer(x, mods_x[0], norm_g[0, 0], pool_wb, pool_scale[0])
    ctx = _pool_layer(ctx, mods_c[0], norm_g[0, 0], pool_wb, pool_scale[0])
    x, moe_w1b, moe_w3b, moe_w2b = _ffn_dense_layer(x, mods_x[0], norm_g[0, 1], w1b, w3b, w2b,
                                                   cast_through=(moe_w1[0], moe_w3[0], moe_w2[0]))
    ctx, = _ffn_dense_layer(ctx, mods_c[0], norm_g[0, 1], w1b, w3b, w2b)

    wqkv = w_qkv[0].astype(BF16)
    lk = seq + ctx.shape[1]
    kv = (jnp.zeros((b, lk, KV_SLAB), BF16), jnp.zeros((b, lk // KEY_CHUNK, KV_SLAB, KEY_CHUNK), BF16))
    q, *kv = _qkv_layer(x, mods_x[1], norm_g[1, 0], wqkv, q_norm_g[0], k_norm_g[0], _rope_tables(seq), 0, kv)
    _, k_all, vt = _qkv_layer(ctx, mods_c[1], norm_g[1, 0], wqkv[:, Q_DIM:], q_norm_g[0], k_norm_g[0], None, seq, kv)
    attn_t = _attention(q, k_all, vt)

    assert ATT_TQ == ROW_TILE, "the output projection reads one transposed attention tile per row tile"
    x3, hf, e_idx, gates, ranks, counts = _oproj_router_layer(
        attn_t, x, mods_x[1], norm_g[1, 1], w_o[0].astype(BF16), router_w[0])
    dest, blk_e, zero_blocks, cap = _routing_plan(e_idx, ranks, counts[:, 0])
    xs = _dispatch(hf, dest, zero_blocks, cap)
    ys = _ffn_experts(xs, blk_e, moe_w1b, moe_w3b, moe_w2b)
    out = _final_layer(x3, ys, dest, gates.T, mods_x[1], final_g, seq)
    return out.reshape(b, seq, d)
```
